```python
import jax, jax.numpy as jnp
from jax import lax
import numpy as np

D_MODEL = 1024
BATCH = 8
SEQ = 4096
DEPTH = 4
DEC_BATCH = 8
DEC_SEQ = 16
PAST_LEN = 1024

CHUNK = 64
N_EVEN = (DEPTH + 1) // 2
N_ODD = DEPTH // 2
D_CONV = D_MODEL // 2
CONV_WIDTH = 31
D_POOL = D_MODEL // 2
POOL_WINDOWS = (2, 4, 8, 16)
POOL_GROUP = D_POOL // len(POOL_WINDOWS)
POOL_HIST = POOL_WINDOWS[-1] - 1
D_SGU = D_MODEL // 2
SGU_CHUNK = 128
N_SGU_HEADS = 4
SGU_HEAD = D_SGU // N_SGU_HEADS
HEAD_DIM = 64
N_Q_HEADS = D_MODEL // 2 // HEAD_DIM
N_KV_HEADS = 2
Q_PER_KV = N_Q_HEADS // N_KV_HEADS
WINDOW = 128
D_FF = 11 * D_MODEL // 4
N_EXPERTS = 8
TOP_K = 2
D_FF_EXPERT = D_FF // TOP_K
EPS = 1e-6

EVEN_IN = 2 * D_CONV + D_POOL
EVEN_OUT = D_CONV + D_POOL
ODD_IN = 2 * D_SGU + (N_Q_HEADS + 2 * N_KV_HEADS) * HEAD_DIM
ODD_OUT = D_SGU + N_Q_HEADS * HEAD_DIM

kernel_name = "hybrid_streaming_encoder_step"


def rms_norm(x, g):
    xf = x.astype(jnp.float32)
    y = xf * lax.rsqrt(jnp.mean(xf * xf, axis=-1, keepdims=True) + EPS)
    return (y * g.astype(jnp.float32)).astype(x.dtype)


def layer_norm(x, g, b):
    xf = x.astype(jnp.float32)
    mu = jnp.mean(xf, axis=-1, keepdims=True)
    xc = xf - mu
    var = jnp.mean(xc * xc, axis=-1, keepdims=True)
    y = xc * lax.rsqrt(var + EPS) * g.astype(jnp.float32) + b.astype(jnp.float32)
    return y.astype(x.dtype)


def causal_depthwise_conv(a, hist, w, b):
    full = jnp.concatenate([hist, a], axis=1)
    y = lax.conv_general_dilated(full, w[:, None, :], window_strides=(1,), padding="VALID",
                                 dimension_numbers=("NWC", "WIO", "NWC"),
                                 feature_group_count=a.shape[-1])
    return y + b, full[:, -(CONV_WIDTH - 1):]


def multiscale_pool(p, hist, start_pos, w_grp, scale):
    B, T, _ = p.shape
    H = hist.shape[1]
    full = jnp.concatenate([hist, p], axis=1)
    ff = full.astype(jnp.float32)
    cs = jnp.cumsum(jnp.pad(ff, ((0, 0), (1, 0), (0, 0))), axis=1)
    pos = start_pos + jnp.arange(T)
    outs = []
    for g, w in enumerate(POOL_WINDOWS):
        lo, hi = g * POOL_GROUP, (g + 1) * POOL_GROUP
        win_sum = cs[:, H + 1:H + 1 + T, lo:hi] - cs[:, H + 1 - w:H + 1 - w + T, lo:hi]
        count = jnp.minimum(w, pos + 1).astype(jnp.float32)[None, :, None]
        outs.append(win_sum / count - ff[:, H:, lo:hi])
    pooled = jnp.stack(outs, axis=2).astype(p.dtype)
    mixed = jnp.einsum("btgc,gcd->btgd", pooled, w_grp).reshape(B, T, D_POOL)
    return mixed * scale, full[:, -H:]


def spatial_gating(z, ln_g, ln_b, w_s, b_s):
    B, T, _ = z.shape
    u, v = z[..., :D_SGU], z[..., D_SGU:]
    v = layer_norm(v, ln_g, ln_b)
    n = -(-T // SGU_CHUNK)
    vp = jnp.pad(v, ((0, 0), (0, n * SGU_CHUNK - T), (0, 0)))
    vp = vp.reshape(B, n, SGU_CHUNK, N_SGU_HEADS, SGU_HEAD)
    mask = jnp.tril(jnp.ones((SGU_CHUNK, SGU_CHUNK), dtype=bool))
    ws = jnp.where(mask[None], w_s, jnp.zeros_like(w_s))
    mixed = jnp.einsum("gts,bnsgc->bntgc", ws, vp) + b_s.T[:, :, None]
    mixed = mixed.reshape(B, n * SGU_CHUNK, D_SGU)[:, :T]
    return u * mixed, v


def sink_softmax(s, sinks):
    sk = sinks.astype(jnp.float32).reshape(N_KV_HEADS, Q_PER_KV)[:, :, None, None]
    m = jnp.maximum(jnp.max(s, axis=-1, keepdims=True), sk)
    e = jnp.exp(s - m)
    return e / (jnp.sum(e, axis=-1, keepdims=True) + jnp.exp(sk - m))


def swa_prompt(q, k, v, sinks):
    B, T = q.shape[:2]
    nc = T // CHUNK
    nb = WINDOW // CHUNK
    pad = ((0, 0), (WINDOW, 0), (0, 0), (0, 0))
    kp = jnp.pad(k, pad).reshape(B, nc + nb, CHUNK, N_KV_HEADS, HEAD_DIM)
    vp = jnp.pad(v, pad).reshape(B, nc + nb, CHUNK, N_KV_HEADS, HEAD_DIM)
    kb = jnp.concatenate([kp[:, j:j + nc] for j in range(nb + 1)], axis=2)
    vb = jnp.concatenate([vp[:, j:j + nc] for j in range(nb + 1)], axis=2)
    qb = q.reshape(B, nc, CHUNK, N_KV_HEADS, Q_PER_KV, HEAD_DIM)
    s = jnp.einsum("bcqhgd,bckhd->bchgqk", qb, kb).astype(jnp.float32) * (HEAD_DIM ** -0.5)
    key_pos = (jnp.arange(nc)[:, None] - nb) * CHUNK + jnp.arange((nb + 1) * CHUNK)[None, :]
    valid = (key_pos >= 0)[None, :, None, None, None, :]
    s = jnp.where(valid, s, -jnp.inf)
    p = sink_softmax(s, sinks).astype(v.dtype)
    o = jnp.einsum("bchgqk,bckhd->bcqhgd", p, vb).reshape(B, T, N_Q_HEADS * HEAD_DIM)
    keep = min(WINDOW, T)
    return o, k[:, -keep:], v[:, -keep:]


def swa_sample(q, k, v, k_cache, v_cache, sinks):
    B, T = q.shape[:2]
    kf = jnp.concatenate([k_cache, k], axis=1)
    vf = jnp.concatenate([v_cache, v], axis=1)
    qh = q.reshape(B, T, N_KV_HEADS, Q_PER_KV, HEAD_DIM)
    s = jnp.einsum("bqhgd,bkhd->bhgqk", qh, kf).astype(jnp.float32) * (HEAD_DIM ** -0.5)
    p = sink_softmax(s, sinks).astype(v.dtype)
    o = jnp.einsum("bhgqk,bkhd->bqhgd", p, vf).reshape(B, T, N_Q_HEADS * HEAD_DIM)
    return o, k, v


def even_mixer(h, conv_hist, pool_hist, start_pos, w_in, conv_w, conv_b, conv_ln_g, conv_ln_b,
               pool_w, pool_scale, w_out):
    proj = h @ w_in
    a = proj[..., :D_CONV] * jax.nn.sigmoid(proj[..., D_CONV:2 * D_CONV])
    p = proj[..., 2 * D_CONV:]
    c, conv_state = causal_depthwise_conv(a, conv_hist, conv_w, conv_b)
    c = jax.nn.silu(layer_norm(c, conv_ln_g, conv_ln_b))
    m, pool_state = multiscale_pool(p, pool_hist, start_pos, pool_w, pool_scale)
    return jnp.concatenate([c, m], axis=-1) @ w_out, conv_state, pool_state


def odd_mixer(h, kv_cache, w_in, ln_g, ln_b, w_s, b_s, sinks, w_out):
    B, T, _ = h.shape
    proj = h @ w_in
    o0 = 2 * D_SGU
    o1 = o0 + N_Q_HEADS * HEAD_DIM
    o2 = o1 + N_KV_HEADS * HEAD_DIM
    z = jax.nn.gelu(proj[..., :o0], approximate=False)
    q = proj[..., o0:o1].reshape(B, T, N_Q_HEADS, HEAD_DIM)
    k = proj[..., o1:o2].reshape(B, T, N_KV_HEADS, HEAD_DIM)
    v = proj[..., o2:].reshape(B, T, N_KV_HEADS, HEAD_DIM)
    gated, v_rows = spatial_gating(z, ln_g, ln_b, w_s, b_s)
    if kv_cache is None:
        attn, k_new, v_new = swa_prompt(q, k, v, sinks)
    else:
        attn, k_new, v_new = swa_sample(q, k, v, kv_cache[0], kv_cache[1], sinks)
    return jnp.concatenate([gated, attn], axis=-1) @ w_out, k_new, v_new, v_rows


def swiglu(x, wg, wu, wd):
    return (jax.nn.silu(x @ wg) * (x @ wu)) @ wd


def moe_swiglu(x, router, wg, wu, wd):
    logits = (x @ router).astype(jnp.float32)
    top_v, top_i = lax.top_k(logits, TOP_K)
    gates = jax.nn.softmax(top_v, axis=-1)
    combine = jnp.sum(jax.nn.one_hot(top_i, N_EXPERTS, dtype=jnp.float32) * gates[..., None], axis=-2)
    combine = combine.astype(x.dtype)
    y = jnp.zeros_like(x)
    for e in range(N_EXPERTS):
        y = y + combine[..., e:e + 1] * swiglu(x, wg[e], wu[e], wd[e])
    return y


def setup_inputs(seed: int = 0) -> dict:
    key = jax.random.key(seed)
    ks = jax.random.split(key, 40)
    counter = [0]

    def nrm(shape, scale=1.0):
        k = ks[counter[0]]
        counter[0] += 1
        return scale * jax.random.normal(k, shape, jnp.float32)

    def gain(shape):
        return 1.0 + 0.1 * nrm(shape)

    win_rows = min(WINDOW, PAST_LEN)
    return {
        "x_prompt": nrm((BATCH, SEQ, D_MODEL)),
        "x_sample": nrm((DEC_BATCH, DEC_SEQ, D_MODEL)),
        "state_conv": nrm((N_EVEN, DEC_BATCH, CONV_WIDTH - 1, D_CONV), 0.5),
        "state_pool": nrm((N_EVEN, DEC_BATCH, POOL_HIST, D_POOL)),
        "cache_swa_k": nrm((N_ODD, DEC_BATCH, win_rows, N_KV_HEADS, HEAD_DIM)),
        "cache_swa_v": nrm((N_ODD, DEC_BATCH, win_rows, N_KV_HEADS, HEAD_DIM)),
        "norm_mix": gain((DEPTH, D_MODEL)),
        "norm_ffn": gain((DEPTH, D_MODEL)),
        "norm_final": gain((D_MODEL,)),
        "even_w_in": nrm((N_EVEN, D_MODEL, EVEN_IN), D_MODEL ** -0.5),
        "conv_w": nrm((N_EVEN, CONV_WIDTH, D_CONV), CONV_WIDTH ** -0.5),
        "conv_b": nrm((N_EVEN, D_CONV), 0.02),
        "conv_ln_g": gain((N_EVEN, D_CONV)),
        "conv_ln_b": nrm((N_EVEN, D_CONV), 0.02),
        "pool_w": nrm((N_EVEN, len(POOL_WINDOWS), POOL_GROUP, POOL_GROUP), POOL_GROUP ** -0.5),
        "pool_scale": gain((N_EVEN, D_POOL)),
        "even_w_out": nrm((N_EVEN, EVEN_OUT, D_MODEL), EVEN_OUT ** -0.5),
        "ffn_w_gate": nrm((N_EVEN, D_MODEL, D_FF), D_MODEL ** -0.5),
        "ffn_w_up": nrm((N_EVEN, D_MODEL, D_FF), D_MODEL ** -0.5),
        "ffn_w_down": nrm((N_EVEN, D_FF, D_MODEL), D_FF ** -0.5),
        "odd_w_in": nrm((N_ODD, D_MODEL, ODD_IN), D_MODEL ** -0.5),
        "sgu_ln_g": gain((N_ODD, D_SGU)),
        "sgu_ln_b": nrm((N_ODD, D_SGU), 0.02),
        "sgu_w": nrm((N_ODD, N_SGU_HEADS, SGU_CHUNK, SGU_CHUNK), SGU_CHUNK ** -0.5),
        "sgu_b": gain((N_ODD, N_SGU_HEADS, SGU_CHUNK)),
        "attn_sinks": nrm((N_ODD, N_Q_HEADS)),
        "odd_w_out": nrm((N_ODD, ODD_OUT, D_MODEL), ODD_OUT ** -0.5),
        "router_w": nrm((N_ODD, D_MODEL, N_EXPERTS), D_MODEL ** -0.5),
        "exp_w_gate": nrm((N_ODD, N_EXPERTS, D_MODEL, D_FF_EXPERT), D_MODEL ** -0.5),
        "exp_w_up": nrm((N_ODD, N_EXPERTS, D_MODEL, D_FF_EXPERT), D_MODEL ** -0.5),
        "exp_w_down": nrm((N_ODD, N_EXPERTS, D_FF_EXPERT, D_MODEL), D_FF_EXPERT ** -0.5),
    }


def reference(x_prompt, x_sample, state_conv, state_pool, cache_swa_k, cache_swa_v,
              norm_mix, norm_ffn, norm_final,
              even_w_in, conv_w, conv_b, conv_ln_g, conv_ln_b, pool_w, pool_scale, even_w_out,
              ffn_w_gate, ffn_w_up, ffn_w_down,
              odd_w_in, sgu_ln_g, sgu_ln_b, sgu_w, sgu_b, attn_sinks, odd_w_out,
              router_w, exp_w_gate, exp_w_up, exp_w_down):
    b_p = x_prompt.shape[0]
    h_p, h_s = x_prompt, x_sample
    conv_p, conv_s, pool_p, pool_s = [], [], [], []
    k_p, v_p, k_s, v_s, sgu_s = [], [], [], [], []
    for layer in range(DEPTH):
        i = layer // 2
        n_p = rms_norm(h_p, norm_mix[layer])
        n_s = rms_norm(h_s, norm_mix[layer])
        if layer % 2 == 0:
            w = (even_w_in[i], conv_w[i], conv_b[i], conv_ln_g[i], conv_ln_b[i],
                 pool_w[i], pool_scale[i], even_w_out[i])
            conv_zero = jnp.zeros((b_p, CONV_WIDTH - 1, D_CONV), x_prompt.dtype)
            pool_zero = jnp.zeros((b_p, POOL_HIST, D_POOL), x_prompt.dtype)
            m_p, cst_p, pst_p = even_mixer(n_p, conv_zero, pool_zero, 0, *w)
            m_s, cst_s, pst_s = even_mixer(n_s, state_conv[i], state_pool[i], PAST_LEN, *w)
            conv_p.append(cst_p)
            conv_s.append(cst_s)
            pool_p.append(pst_p)
            pool_s.append(pst_s)
        else:
            w = (odd_w_in[i], sgu_ln_g[i], sgu_ln_b[i], sgu_w[i], sgu_b[i], attn_sinks[i], odd_w_out[i])
            m_p, kn_p, vn_p, _ = odd_mixer(n_p, None, *w)
            m_s, kn_s, vn_s, sv_s = odd_mixer(n_s, (cache_swa_k[i], cache_swa_v[i]), *w)
            k_p.append(kn_p)
            v_p.append(vn_p)
            k_s.append(kn_s)
            v_s.append(vn_s)
            sgu_s.append(sv_s)
        h_p = h_p + m_p
        h_s = h_s + m_s
        n_p = rms_norm(h_p, norm_ffn[layer])
        n_s = rms_norm(h_s, norm_ffn[layer])
        if layer % 2 == 0:
            h_p = h_p + swiglu(n_p, ffn_w_gate[i], ffn_w_up[i], ffn_w_down[i])
            h_s = h_s + swiglu(n_s, ffn_w_gate[i], ffn_w_up[i], ffn_w_down[i])
        else:
            h_p = h_p + moe_swiglu(n_p, router_w[i], exp_w_gate[i], exp_w_up[i], exp_w_down[i])
            h_s = h_s + moe_swiglu(n_s, router_w[i], exp_w_gate[i], exp_w_up[i], exp_w_down[i])
    y_prompt = rms_norm(h_p, norm_final)
    y_sample = rms_norm(h_s, norm_final)
    return (y_prompt, y_sample,
            jnp.stack(conv_p), jnp.stack(conv_s),
            jnp.stack(pool_p), jnp.stack(pool_s),
            jnp.stack(k_p), jnp.stack(v_p),
            jnp.stack(k_s), jnp.stack(v_s),
            jnp.stack(sgu_s))
```

```python
import functools

import jax
import jax.numpy as jnp
from jax import lax
from jax.experimental import pallas as pl
from jax.experimental.pallas import tpu as pltpu

F32 = jnp.float32
BF16 = jnp.bfloat16

D_MODEL = 1024
PAST_LEN = 1024
CHUNK = 64
D_CONV = 512
CONV_WIDTH = 31
CONV_HIST = CONV_WIDTH - 1
D_POOL = 512
POOL_WINDOWS = (2, 4, 8, 16)
POOL_GROUP = 128
POOL_HIST = 15
D_SGU = 512
SGU_CHUNK = 128
N_SGU_HEADS = 4
SGU_HEAD = 128
HEAD_DIM = 64
N_Q_HEADS = 8
N_KV_HEADS = 2
Q_PER_KV = 4
WINDOW = 128
D_FF = 2816
N_EXPERTS = 8
D_FF_EXPERT = 1408
EPS = 1e-6

LANES = 128
A_PAD = 32
P_PAD = 16
VMEM_LIMIT = 56 * 1024 * 1024


def _sigmoid(x):
    return 1.0 / (1.0 + jnp.exp(-x))


def _rms(x, g):
    ms = jnp.mean(x * x, axis=-1, keepdims=True)
    return x * lax.rsqrt(ms + EPS) * g


def _layer_norm(x, g, b):
    mu = jnp.mean(x, axis=-1, keepdims=True)
    xc = x - mu
    var = jnp.mean(xc * xc, axis=-1, keepdims=True)
    return xc * lax.rsqrt(var + EPS) * g + b


def _even_mixer_kernel(h_ref, g_ref, win_ref, cw_ref, cb_ref, lng_ref, lnb_ref, pw_ref, ps_ref,
                       wout_ref, chist_ref, phist_ref,
                       out_ref, cstate_ref, pstate_ref,
                       abuf, pbuf, cat, *, tm, rc, start_pos):
    t = pl.program_id(1)

    @pl.when(t == 0)
    def _():
        abuf[0:A_PAD - CONV_HIST, :] = jnp.zeros((A_PAD - CONV_HIST, D_CONV), F32)
        abuf[A_PAD - CONV_HIST:A_PAD, :] = chist_ref[0]
        pbuf[0:P_PAD - POOL_HIST, :] = jnp.zeros((P_PAD - POOL_HIST, D_POOL), F32)
        pbuf[P_PAD - POOL_HIST:P_PAD, :] = phist_ref[0]

    h = h_ref[...]
    n = _rms(h, g_ref[...])
    proj = jnp.dot(n.astype(BF16), win_ref[...], preferred_element_type=F32)
    a = proj[:, :D_CONV] * _sigmoid(proj[:, D_CONV:2 * D_CONV])
    abuf[A_PAD:A_PAD + tm, :] = a
    pbuf[P_PAD:P_PAD + tm, :] = proj[:, 2 * D_CONV:]

    off = A_PAD - CONV_HIST
    for r0 in range(0, tm, rc):
        cparts = []
        for lb in range(D_CONV // LANES):
            ls = slice(lb * LANES, (lb + 1) * LANES)
            acc = abuf[r0 + off:r0 + off + rc, ls] * cw_ref[0:1, ls]
            for k in range(1, CONV_WIDTH):
                acc = acc + abuf[r0 + off + k:r0 + off + k + rc, ls] * cw_ref[k:k + 1, ls]
            cparts.append(acc)
        c = jnp.concatenate(cparts, axis=-1) + cb_ref[...]
        c = _layer_norm(c, lng_ref[...], lnb_ref[...])
        c = c * _sigmoid(c)
        cat[r0:r0 + rc, 0:D_CONV] = c.astype(BF16)

        pos1 = (start_pos + 1 + t * tm + r0
                + lax.broadcasted_iota(jnp.int32, (rc, 1), 0)).astype(F32)
        for gi, w in enumerate(POOL_WINDOWS):
            ls = slice(gi * POOL_GROUP, (gi + 1) * POOL_GROUP)
            cur = pbuf[P_PAD + r0:P_PAD + r0 + rc, ls]
            s = cur
            for i in range(1, w):
                s = s + pbuf[P_PAD + r0 - i:P_PAD + r0 - i + rc, ls]
            pooled = s / jnp.minimum(float(w), pos1) - cur
            mixed = jnp.dot(pooled.astype(BF16), pw_ref[gi], preferred_element_type=F32)
            cat[r0:r0 + rc, D_CONV + gi * POOL_GROUP:D_CONV + (gi + 1) * POOL_GROUP] = (
                mixed * ps_ref[0:1, ls]).astype(BF16)

    out_ref[...] = h + jnp.dot(cat[...], wout_ref[...], preferred_element_type=F32)

    atail = abuf[tm:tm + A_PAD, :]
    ptail = pbuf[tm:tm + P_PAD, :]
    abuf[0:A_PAD, :] = atail
    pbuf[0:P_PAD, :] = ptail
    cstate_ref[0] = atail[A_PAD - CONV_HIST:, :]
    pstate_ref[0] = ptail[P_PAD - POOL_HIST:, :]


def _even_mixer(h, nb, tlen, start_pos, g, w_in, cw, cb, lng, lnb, pw, ps, w_out, chist, phist, tm):
    rc = min(64, tm)
    nt = tlen // tm
    row = lambda b, t: (b * nt + t, 0)
    const2 = lambda b, t: (0, 0)
    const3 = lambda b, t: (0, 0, 0)
    per_b = lambda b, t: (b, 0, 0)
    kern = functools.partial(_even_mixer_kernel, tm=tm, rc=rc, start_pos=start_pos)
    return pl.pallas_call(
        kern,
        grid=(nb, nt),
        in_specs=[
            pl.BlockSpec((tm, D_MODEL), row),
            pl.BlockSpec((1, D_MODEL), const2),
            pl.BlockSpec((D_MODEL, 3 * D_CONV), const2),
            pl.BlockSpec((CONV_WIDTH, D_CONV), const2),
            pl.BlockSpec((1, D_CONV), const2),
            pl.BlockSpec((1, D_CONV), const2),
            pl.BlockSpec((1, D_CONV), const2),
            pl.BlockSpec((len(POOL_WINDOWS), POOL_GROUP, POOL_GROUP), const3),
            pl.BlockSpec((1, D_POOL), const2),
            pl.BlockSpec((D_MODEL, D_MODEL), const2),
            pl.BlockSpec((1, CONV_HIST, D_CONV), per_b),
            pl.BlockSpec((1, POOL_HIST, D_POOL), per_b),
        ],
        out_specs=[
            pl.BlockSpec((tm, D_MODEL), row),
            pl.BlockSpec((1, CONV_HIST, D_CONV), per_b),
            pl.BlockSpec((1, POOL_HIST, D_POOL), per_b),
        ],
        out_shape=[
            jax.ShapeDtypeStruct((nb * tlen, D_MODEL), F32),
            jax.ShapeDtypeStruct((nb, CONV_HIST, D_CONV), F32),
            jax.ShapeDtypeStruct((nb, POOL_HIST, D_POOL), F32),
        ],
        scratch_shapes=[
            pltpu.VMEM((A_PAD + tm, D_CONV), F32),
            pltpu.VMEM((P_PAD + tm, D_POOL), F32),
            pltpu.VMEM((tm, D_MODEL), BF16),
        ],
        compiler_params=pltpu.CompilerParams(
            dimension_semantics=("arbitrary", "arbitrary"), vmem_limit_bytes=VMEM_LIMIT),
        name="even_mixer",
    )(h, g, w_in, cw, cb, lng, lnb, pw, ps, w_out, chist, phist)


def _odd_mixer_kernel(sink_ref, h_ref, g_ref, win_ref, lng_ref, lnb_ref, ws_ref, bs_ref, wout_ref,
                      kc_ref, vc_ref, *rest, tm, cq, kv_rows, mask_first, emit_sgu_v):
    if emit_sgu_v:
        out_ref, knew_ref, vnew_ref, sguv_ref, kbuf, vbuf, vs_scr, attn, cat = rest
    else:
        out_ref, knew_ref, vnew_ref, kbuf, vbuf, vs_scr, attn, cat = rest
    t = pl.program_id(1)

    @pl.when(t == 0)
    def _():
        for gi in range(N_KV_HEADS):
            hs = slice(gi * HEAD_DIM, (gi + 1) * HEAD_DIM)
            kbuf[gi, 0:WINDOW, :] = kc_ref[0, :, hs].astype(BF16)
            vbuf[gi, 0:WINDOW, :] = vc_ref[0, :, hs].astype(BF16)

    h = h_ref[...]
    n = _rms(h, g_ref[...])
    proj = jnp.dot(n.astype(BF16), win_ref[...], preferred_element_type=F32)
    o0 = 2 * D_SGU
    o1 = o0 + N_Q_HEADS * HEAD_DIM
    o2 = o1 + N_KV_HEADS * HEAD_DIM
    zp = proj[:, :o0]
    z = 0.5 * zp * (1.0 + lax.erf(zp * (0.5 ** 0.5)))
    u = z[:, :D_SGU]
    v = _layer_norm(z[:, D_SGU:], lng_ref[...], lnb_ref[...])
    if emit_sgu_v:
        sguv_ref[...] = v

    n_sgu = vs_scr.shape[0] // SGU_CHUNK
    if tm % SGU_CHUNK:
        vs_scr[...] = jnp.zeros(vs_scr.shape, BF16)
    vs_scr[0:tm, :] = v.astype(BF16)
    ri = lax.broadcasted_iota(jnp.int32, (SGU_CHUNK, SGU_CHUNK), 0)
    ci = lax.broadcasted_iota(jnp.int32, (SGU_CHUNK, SGU_CHUNK), 1)
    for gi in range(N_SGU_HEADS):
        ls = slice(gi * SGU_HEAD, (gi + 1) * SGU_HEAD)
        wsg = jnp.where(ri >= ci, ws_ref[gi], 0.0).astype(BF16)
        for c in range(n_sgu):
            rows = min(SGU_CHUNK, tm - c * SGU_CHUNK)
            mixed = jnp.dot(wsg, vs_scr[c * SGU_CHUNK:(c + 1) * SGU_CHUNK, ls],
                            preferred_element_type=F32) + bs_ref[:, ls]
            r0 = c * SGU_CHUNK
            cat[r0:r0 + rows, ls] = (u[r0:r0 + rows, ls] * mixed[0:rows]).astype(BF16)

    k = proj[:, o1:o2]
    vv = proj[:, o2:]
    knew_ref[0] = k[tm - kv_rows:, :]
    vnew_ref[0] = vv[tm - kv_rows:, :]
    for gi in range(N_KV_HEADS):
        hs = slice(gi * HEAD_DIM, (gi + 1) * HEAD_DIM)
        kbuf[gi, WINDOW:WINDOW + tm, :] = k[:, hs].astype(BF16)
        vbuf[gi, WINDOW:WINDOW + tm, :] = vv[:, hs].astype(BF16)
    q = proj[:, o0:o1].astype(BF16)
    nk = WINDOW + cq
    qrow = lax.broadcasted_iota(jnp.int32, (Q_PER_KV * cq, 1), 0)
    for gi in range(N_KV_HEADS):
        sk = jnp.zeros((Q_PER_KV * cq, 1), F32)
        for i in range(Q_PER_KV):
            sk = jnp.where((qrow >= i * cq) & (qrow < (i + 1) * cq), sink_ref[gi * Q_PER_KV + i], sk)
        for c in range(tm // cq):
            r0 = c * cq
            qs = jnp.concatenate(
                [q[r0:r0 + cq, (gi * Q_PER_KV + i) * HEAD_DIM:(gi * Q_PER_KV + i + 1) * HEAD_DIM]
                 for i in range(Q_PER_KV)], axis=0)
            s = lax.dot_general(qs, kbuf[gi, r0:r0 + nk, :], (((1,), (1,)), ((), ())),
                                preferred_element_type=F32) * (HEAD_DIM ** -0.5)
            if mask_first and r0 < WINDOW:
                key_pos = t * tm + (r0 - WINDOW) + lax.broadcasted_iota(jnp.int32, (1, nk), 1)
                s = jnp.where(key_pos >= 0, s, -jnp.inf)
            m = jnp.maximum(jnp.max(s, axis=-1, keepdims=True), sk)
            e = jnp.exp(s - m)
            p = e / (jnp.sum(e, axis=-1, keepdims=True) + jnp.exp(sk - m))
            o = jnp.dot(p.astype(BF16), vbuf[gi, r0:r0 + nk, :], preferred_element_type=F32)
            for i in range(Q_PER_KV):
                hd = gi * Q_PER_KV + i
                attn[r0:r0 + cq, hd * HEAD_DIM:(hd + 1) * HEAD_DIM] = o[i * cq:(i + 1) * cq, :]
    cat[:, D_SGU:] = attn[...].astype(BF16)

    out_ref[...] = h + jnp.dot(cat[...], wout_ref[...], preferred_element_type=F32)

    for gi in range(N_KV_HEADS):
        ktail = kbuf[gi, tm:tm + WINDOW, :]
        vtail = vbuf[gi, tm:tm + WINDOW, :]
        kbuf[gi, 0:WINDOW, :] = ktail
        vbuf[gi, 0:WINDOW, :] = vtail


def _odd_mixer(h, nb, tlen, sinks, g, w_in, lng, lnb, ws, bs, w_out, kc, vc, tm, cq, mask_first,
               emit_sgu_v):
    nt = tlen // tm
    kv_rows = min(WINDOW, tlen)
    odd_in = w_in.shape[1]
    row = lambda b, t: (b * nt + t, 0)
    const2 = lambda b, t: (0, 0)
    const3 = lambda b, t: (0, 0, 0)
    per_b = lambda b, t: (b, 0, 0)
    kern = functools.partial(_odd_mixer_kernel, tm=tm, cq=cq, kv_rows=kv_rows,
                             mask_first=mask_first, emit_sgu_v=emit_sgu_v)
    kvw = N_KV_HEADS * HEAD_DIM
    out_specs = [
        pl.BlockSpec((tm, D_MODEL), row),
        pl.BlockSpec((1, kv_rows, kvw), per_b),
        pl.BlockSpec((1, kv_rows, kvw), per_b),
    ]
    out_shape = [
        jax.ShapeDtypeStruct((nb * tlen, D_MODEL), F32),
        jax.ShapeDtypeStruct((nb, kv_rows, kvw), F32),
        jax.ShapeDtypeStruct((nb, kv_rows, kvw), F32),
    ]
    if emit_sgu_v:
        out_specs.append(pl.BlockSpec((tm, D_SGU), row))
        out_shape.append(jax.ShapeDtypeStruct((nb * tlen, D_SGU), F32))
    n_sgu = -(-tm // SGU_CHUNK)
    return pl.pallas_call(
        kern,
        grid=(nb, nt),
        in_specs=[
            pl.BlockSpec(memory_space=pltpu.SMEM),
            pl.BlockSpec((tm, D_MODEL), row),
            pl.BlockSpec((1, D_MODEL), const2),
            pl.BlockSpec((D_MODEL, odd_in), const2),
            pl.BlockSpec((1, D_SGU), const2),
            pl.BlockSpec((1, D_SGU), const2),
            pl.BlockSpec((N_SGU_HEADS, SGU_CHUNK, SGU_CHUNK), const3),
            pl.BlockSpec((SGU_CHUNK, D_SGU), const2),
            pl.BlockSpec((D_MODEL, D_MODEL), const2),
            pl.BlockSpec((1, WINDOW, kvw), per_b),
            pl.BlockSpec((1, WINDOW, kvw), per_b),
        ],
        out_specs=out_specs,
        out_shape=out_shape,
        scratch_shapes=[
            pltpu.VMEM((N_KV_HEADS, WINDOW + tm, HEAD_DIM), BF16),
            pltpu.VMEM((N_KV_HEADS, WINDOW + tm, HEAD_DIM), BF16),
            pltpu.VMEM((n_sgu * SGU_CHUNK, D_SGU), BF16),
            pltpu.VMEM((tm, N_Q_HEADS * HEAD_DIM), F32),
            pltpu.VMEM((tm, D_MODEL), BF16),
        ],
        compiler_params=pltpu.CompilerParams(
            dimension_semantics=("arbitrary", "arbitrary"), vmem_limit_bytes=VMEM_LIMIT),
        name="odd_mixer",
    )(sinks, h, g, w_in, lng, lnb, ws, bs, w_out, kc, vc)


def _ffn_kernel(*refs, routed, final_norm, n_chunks):
    refs = list(refs)
    h_ref, g_ref = refs[0], refs[1]
    pos = 2
    if routed:
        router_ref = refs[pos]
        pos += 1
    wg_ref, wu_ref, wd_ref = refs[pos:pos + 3]
    pos += 3
    if final_norm:
        gf_ref = refs[pos]
        pos += 1
    out_ref = refs[pos]
    n_scr, acc = refs[pos + 1], refs[pos + 2]
    if routed:
        comb = refs[pos + 3]
    j = pl.program_id(1)

    @pl.when(j == 0)
    def _():
        n = _rms(h_ref[...], g_ref[...])
        nb16 = n.astype(BF16)
        n_scr[...] = nb16
        if routed:
            logits = jnp.dot(nb16, router_ref[...], preferred_element_type=F32)
            lane = lax.broadcasted_iota(jnp.int32, logits.shape, 1)
            logits = jnp.where(lane < N_EXPERTS, logits, -jnp.inf)
            m1 = jnp.max(logits, axis=-1, keepdims=True)
            i1 = jnp.min(jnp.where(logits == m1, lane, LANES), axis=-1, keepdims=True)
            rest = jnp.where(lane == i1, -jnp.inf, logits)
            m2 = jnp.max(rest, axis=-1, keepdims=True)
            i2 = jnp.min(jnp.where(rest == m2, lane, LANES), axis=-1, keepdims=True)
            e2 = jnp.exp(m2 - m1)
            den = 1.0 + e2
            comb[...] = jnp.where(lane == i1, 1.0 / den, 0.0) + jnp.where(lane == i2, e2 / den, 0.0)

    x = n_scr[...]
    gate = jnp.dot(x, wg_ref[...], preferred_element_type=F32)
    up = jnp.dot(x, wu_ref[...], preferred_element_type=F32)
    act = (gate * _sigmoid(gate)) * up
    y = jnp.dot(act.astype(BF16), wd_ref[...], preferred_element_type=F32)
    if routed:
        cmb = comb[...]
        lane = lax.broadcasted_iota(jnp.int32, cmb.shape, 1)
        y = y * jnp.sum(jnp.where(lane == j, cmb, 0.0), axis=-1, keepdims=True)

    @pl.when(j == 0)
    def _():
        acc[...] = y

    @pl.when(j > 0)
    def _():
        acc[...] += y

    @pl.when(j == n_chunks - 1)
    def _():
        res = h_ref[...] + acc[...]
        if final_norm:
            res = _rms(res, gf_ref[...])
        out_ref[...] = res


def _ffn(h, g, wg, wu, wd, tm, router=None, g_final=None):
    rows = h.shape[0]
    routed = router is not None
    final_norm = g_final is not None
    row = lambda i, j: (i, 0)
    const2 = lambda i, j: (0, 0)
    in_specs = [pl.BlockSpec((tm, D_MODEL), row), pl.BlockSpec((1, D_MODEL), const2)]
    args = [h, g]
    if routed:
        n_chunks = N_EXPERTS
        in_specs.append(pl.BlockSpec((D_MODEL, LANES), const2))
        args.append(router)
        in_specs += [
            pl.BlockSpec((None, D_MODEL, D_FF_EXPERT), lambda i, j: (j, 0, 0)),
            pl.BlockSpec((None, D_MODEL, D_FF_EXPERT), lambda i, j: (j, 0, 0)),
            pl.BlockSpec((None, D_FF_EXPERT, D_MODEL), lambda i, j: (j, 0, 0)),
        ]
    else:
        n_chunks = D_FF // D_FF_EXPERT
        in_specs += [
            pl.BlockSpec((D_MODEL, D_FF_EXPERT), lambda i, j: (0, j)),
            pl.BlockSpec((D_MODEL, D_FF_EXPERT), lambda i, j: (0, j)),
            pl.BlockSpec((D_FF_EXPERT, D_MODEL), lambda i, j: (j, 0)),
        ]
    args += [wg, wu, wd]
    if final_norm:
        in_specs.append(pl.BlockSpec((1, D_MODEL), const2))
        args.append(g_final)
    scratch = [pltpu.VMEM((tm, D_MODEL), BF16), pltpu.VMEM((tm, D_MODEL), F32)]
    if routed:
        scratch.append(pltpu.VMEM((tm, LANES), F32))
    kern = functools.partial(_ffn_kernel, routed=routed, final_norm=final_norm, n_chunks=n_chunks)
    return pl.pallas_call(
        kern,
        grid=(rows // tm, n_chunks),
        in_specs=in_specs,
        out_specs=pl.BlockSpec((tm, D_MODEL), row),
        out_shape=jax.ShapeDtypeStruct((rows, D_MODEL), F32),
        scratch_shapes=scratch,
        compiler_params=pltpu.CompilerParams(
            dimension_semantics=("arbitrary", "arbitrary"), vmem_limit_bytes=VMEM_LIMIT),
        name="moe_ffn" if routed else "dense_ffn",
    )(*args)


def kernel(x_prompt, x_sample, state_conv, state_pool, cache_swa_k, cache_swa_v, norm_mix, norm_ffn, norm_final, even_w_in, conv_w, conv_b, conv_ln_g, conv_ln_b, pool_w, pool_scale, even_w_out, ffn_w_gate, ffn_w_up, ffn_w_down, odd_w_in, sgu_ln_g, sgu_ln_b, sgu_w, sgu_b, attn_sinks, odd_w_out, router_w, exp_w_gate, exp_w_up, exp_w_down):
    bp, tp, _ = x_prompt.shape
    bs, ts, _ = x_sample.shape
    depth = norm_mix.shape[0]
    past_len = PAST_LEN
    tm_p = 512
    kvw = N_KV_HEADS * HEAD_DIM

    h_p = x_prompt.reshape(bp * tp, D_MODEL)
    h_s = x_sample.reshape(bs * ts, D_MODEL)
    b16 = lambda w: w.astype(BF16)
    r2 = lambda v: v.reshape(1, -1)

    conv_p, conv_s, pool_p, pool_s = [], [], [], []
    k_p, v_p, k_s, v_s, sgu_s = [], [], [], [], []
    for layer in range(depth):
        i = layer // 2
        last = layer == depth - 1
        if layer % 2 == 0:
            w = (r2(norm_mix[layer]), b16(even_w_in[i]), conv_w[i], r2(conv_b[i]), r2(conv_ln_g[i]),
                 r2(conv_ln_b[i]), b16(pool_w[i]), r2(pool_scale[i]), b16(even_w_out[i]))
            h_p, cst, pst = _even_mixer(
                h_p, bp, tp, 0, *w,
                jnp.zeros((bp, CONV_HIST, D_CONV), F32), jnp.zeros((bp, POOL_HIST, D_POOL), F32), tm_p)
            conv_p.append(cst)
            pool_p.append(pst)
            h_s, cst, pst = _even_mixer(h_s, bs, ts, past_len, *w, state_conv[i], state_pool[i], ts)
            conv_s.append(cst)
            pool_s.append(pst)
            fw = (r2(norm_ffn[layer]), b16(ffn_w_gate[i]), b16(ffn_w_up[i]), b16(ffn_w_down[i]))
            gf = r2(norm_final) if last else None
            h_p = _ffn(h_p, *fw, tm_p, g_final=gf)
            h_s = _ffn(h_s, *fw, bs * ts, g_final=gf)
        else:
            bias = jnp.repeat(sgu_b[i].T, SGU_HEAD, axis=1)
            w = (attn_sinks[i], r2(norm_mix[layer]), b16(odd_w_in[i]), r2(sgu_ln_g[i]),
                 r2(sgu_ln_b[i]), sgu_w[i], bias, b16(odd_w_out[i]))
            zkv = jnp.zeros((bp, WINDOW, kvw), F32)
            h_p, kn, vn = _odd_mixer(h_p, bp, tp, *w, zkv, zkv, tm_p, CHUNK, True, False)
            k_p.append(kn.reshape(bp, -1, N_KV_HEADS, HEAD_DIM))
            v_p.append(vn.reshape(bp, -1, N_KV_HEADS, HEAD_DIM))
            h_s, kn, vn, sv = _odd_mixer(
                h_s, bs, ts, *w, cache_swa_k[i].reshape(bs, WINDOW, kvw),
                cache_swa_v[i].reshape(bs, WINDOW, kvw), ts, ts, False, True)
            k_s.append(kn.reshape(bs, ts, N_KV_HEADS, HEAD_DIM))
            v_s.append(vn.reshape(bs, ts, N_KV_HEADS, HEAD_DIM))
            sgu_s.append(sv.reshape(bs, ts, D_SGU))
            router = jnp.pad(b16(router_w[i]), ((0, 0), (0, LANES - N_EXPERTS)))
            fw = (r2(norm_ffn[layer]), b16(exp_w_gate[i]), b16(exp_w_up[i]), b16(exp_w_down[i]))
            gf = r2(norm_final) if last else None
            h_p = _ffn(h_p, *fw, tm_p, router=router, g_final=gf)
            h_s = _ffn(h_s, *fw, bs * ts, router=router, g_final=gf)

    return (h_p.reshape(bp, tp, D_MODEL), h_s.reshape(bs, ts, D_MODEL),
            jnp.stack(conv_p), jnp.stack(conv_s),
            jnp.stack(pool_p), jnp.stack(pool_s),
            jnp.stack(k_p), jnp.stack(v_p),
            jnp.stack(k_s), jnp.stack(v_s),
            jnp.stack(sgu_s))
```

```python
import functools

import jax
import jax.numpy as jnp
from jax import lax
from jax.experimental import pallas as pl
from jax.experimental.pallas import tpu as pltpu

F32 = jnp.float32
BF16 = jnp.bfloat16

D_MODEL = 1024
PAST_LEN = 1024
CHUNK = 64
D_CONV = 512
CONV_WIDTH = 31
CONV_HIST = CONV_WIDTH - 1
D_POOL = 512
POOL_WINDOWS = (2, 4, 8, 16)
POOL_GROUP = 128
POOL_HIST = 15
D_SGU = 512
SGU_CHUNK = 128
N_SGU_HEADS = 4
SGU_HEAD = 128
HEAD_DIM = 64
N_Q_HEADS = 8
N_KV_HEADS = 2
Q_PER_KV = 4
WINDOW = 128
D_FF = 2816
N_EXPERTS = 8
D_FF_EXPERT = 1408
EPS = 1e-6

LANES = 128
A_PAD = 32
P_PAD = 16
VMEM_LIMIT = 56 * 1024 * 1024


def _sigmoid(x):
    return 1.0 / (1.0 + jnp.exp(-x))


def _rms(x, g):
    ms = jnp.mean(x * x, axis=-1, keepdims=True)
    return x * lax.rsqrt(ms + EPS) * g


def _layer_norm(x, g, b):
    mu = jnp.mean(x, axis=-1, keepdims=True)
    xc = x - mu
    var = jnp.mean(xc * xc, axis=-1, keepdims=True)
    return xc * lax.rsqrt(var + EPS) * g + b


def _mm(x, w, hp=False):
    if hp:
        return jnp.dot(x, w, preferred_element_type=F32, precision=lax.Precision.HIGHEST)
    return jnp.dot(x.astype(BF16), w.astype(BF16), preferred_element_type=F32)


def _mm_t(x, y, hp=False):
    dims = (((1,), (1,)), ((), ()))
    if hp:
        return lax.dot_general(x, y, dims, preferred_element_type=F32,
                               precision=lax.Precision.HIGHEST)
    return lax.dot_general(x.astype(BF16), y.astype(BF16), dims, preferred_element_type=F32)


def _even_mixer_kernel(h_ref, g_ref, win_ref, cw_ref, cb_ref, lng_ref, lnb_ref, pw_ref, ps_ref,
                       wout_ref, chist_ref, phist_ref,
                       out_ref, cstate_ref, pstate_ref,
                       abuf, pbuf, cat, *, nseq, tm, rc, start_pos, hp):
    t = pl.program_id(1)

    @pl.when(t == 0)
    def _():
        for s in range(nseq):
            abuf[s, 0:A_PAD - CONV_HIST, :] = jnp.zeros((A_PAD - CONV_HIST, D_CONV), F32)
            abuf[s, A_PAD - CONV_HIST:A_PAD, :] = chist_ref[s]
            pbuf[s, 0:P_PAD - POOL_HIST, :] = jnp.zeros((P_PAD - POOL_HIST, D_POOL), F32)
            pbuf[s, P_PAD - POOL_HIST:P_PAD, :] = phist_ref[s]

    h = h_ref[...]
    n = _rms(h, g_ref[...])
    proj = _mm(n, win_ref[...], hp)
    a = proj[:, :D_CONV] * _sigmoid(proj[:, D_CONV:2 * D_CONV])
    for s in range(nseq):
        abuf[s, A_PAD:A_PAD + tm, :] = a[s * tm:(s + 1) * tm]
        pbuf[s, P_PAD:P_PAD + tm, :] = proj[s * tm:(s + 1) * tm, 2 * D_CONV:]

    off = A_PAD - CONV_HIST
    for s in range(nseq):
        for r0 in range(0, tm, rc):
            o0 = s * tm + r0
            cparts = []
            for lb in range(D_CONV // LANES):
                ls = slice(lb * LANES, (lb + 1) * LANES)
                acc = abuf[s, r0 + off:r0 + off + rc, ls] * cw_ref[0:1, ls]
                for k in range(1, CONV_WIDTH):
                    acc = acc + abuf[s, r0 + off + k:r0 + off + k + rc, ls] * cw_ref[k:k + 1, ls]
                cparts.append(acc)
            c = jnp.concatenate(cparts, axis=-1) + cb_ref[...]
            c = _layer_norm(c, lng_ref[...], lnb_ref[...])
            c = c * _sigmoid(c)
            cat[o0:o0 + rc, 0:D_CONV] = c.astype(cat.dtype)

            pos1 = (start_pos + 1 + t * tm + r0
                    + lax.broadcasted_iota(jnp.int32, (rc, 1), 0)).astype(F32)
            for gi, w in enumerate(POOL_WINDOWS):
                ls = slice(gi * POOL_GROUP, (gi + 1) * POOL_GROUP)
                cur = pbuf[s, P_PAD + r0:P_PAD + r0 + rc, ls]
                acc = cur
                for i in range(1, w):
                    acc = acc + pbuf[s, P_PAD + r0 - i:P_PAD + r0 - i + rc, ls]
                pooled = acc / jnp.minimum(float(w), pos1) - cur
                mixed = _mm(pooled, pw_ref[gi], hp)
                cat[o0:o0 + rc, D_CONV + gi * POOL_GROUP:D_CONV + (gi + 1) * POOL_GROUP] = (
                    mixed * ps_ref[0:1, ls]).astype(cat.dtype)

    out_ref[...] = h + _mm(cat[...], wout_ref[...], hp)

    for s in range(nseq):
        atail = abuf[s, tm:tm + A_PAD, :]
        ptail = pbuf[s, tm:tm + P_PAD, :]
        abuf[s, 0:A_PAD, :] = atail
        pbuf[s, 0:P_PAD, :] = ptail
        cstate_ref[s] = atail[A_PAD - CONV_HIST:, :]
        pstate_ref[s] = ptail[P_PAD - POOL_HIST:, :]


def _even_mixer(h, nb, tlen, start_pos, g, w_in, cw, cb, lng, lnb, pw, ps, w_out, chist, phist, tm,
                nseq, hp):
    rc = min(64, tm)
    nt = tlen // tm
    rows = nseq * tm
    row = lambda b, t: (b * nt + t, 0)
    const2 = lambda b, t: (0, 0)
    const3 = lambda b, t: (0, 0, 0)
    per_b = lambda b, t: (b, 0, 0)
    kern = functools.partial(_even_mixer_kernel, nseq=nseq, tm=tm, rc=rc, start_pos=start_pos, hp=hp)
    return pl.pallas_call(
        kern,
        grid=(nb // nseq, nt),
        in_specs=[
            pl.BlockSpec((rows, D_MODEL), row),
            pl.BlockSpec((1, D_MODEL), const2),
            pl.BlockSpec((D_MODEL, 3 * D_CONV), const2),
            pl.BlockSpec((CONV_WIDTH, D_CONV), const2),
            pl.BlockSpec((1, D_CONV), const2),
            pl.BlockSpec((1, D_CONV), const2),
            pl.BlockSpec((1, D_CONV), const2),
            pl.BlockSpec((len(POOL_WINDOWS), POOL_GROUP, POOL_GROUP), const3),
            pl.BlockSpec((1, D_POOL), const2),
            pl.BlockSpec((D_MODEL, D_MODEL), const2),
            pl.BlockSpec((nseq, CONV_HIST, D_CONV), per_b),
            pl.BlockSpec((nseq, POOL_HIST, D_POOL), per_b),
        ],
        out_specs=[
            pl.BlockSpec((rows, D_MODEL), row),
            pl.BlockSpec((nseq, CONV_HIST, D_CONV), per_b),
            pl.BlockSpec((nseq, POOL_HIST, D_POOL), per_b),
        ],
        out_shape=[
            jax.ShapeDtypeStruct((nb * tlen, D_MODEL), F32),
            jax.ShapeDtypeStruct((nb, CONV_HIST, D_CONV), F32),
            jax.ShapeDtypeStruct((nb, POOL_HIST, D_POOL), F32),
        ],
        scratch_shapes=[
            pltpu.VMEM((nseq, A_PAD + tm, D_CONV), F32),
            pltpu.VMEM((nseq, P_PAD + tm, D_POOL), F32),
            pltpu.VMEM((rows, D_MODEL), F32 if hp else BF16),
        ],
        compiler_params=pltpu.CompilerParams(
            dimension_semantics=("arbitrary", "arbitrary"), vmem_limit_bytes=VMEM_LIMIT),
        name="even_mixer",
    )(h, g, w_in, cw, cb, lng, lnb, pw, ps, w_out, chist, phist)


def _odd_mixer_kernel(sink_ref, h_ref, g_ref, win_ref, lng_ref, lnb_ref, ws_ref, bs_ref, wout_ref,
                      kc_ref, vc_ref, *rest, nseq, tm, cq, kv_rows, mask_first, emit_sgu_v, hp):
    if emit_sgu_v:
        out_ref, knew_ref, vnew_ref, sguv_ref, kbuf, vbuf, vs_scr, attn, cat = rest
    else:
        out_ref, knew_ref, vnew_ref, kbuf, vbuf, vs_scr, attn, cat = rest
    t = pl.program_id(1)
    op_dtype = kbuf.dtype

    @pl.when(t == 0)
    def _():
        for s in range(nseq):
            for gi in range(N_KV_HEADS):
                hs = slice(gi * HEAD_DIM, (gi + 1) * HEAD_DIM)
                kbuf[s * N_KV_HEADS + gi, 0:WINDOW, :] = kc_ref[s, :, hs].astype(op_dtype)
                vbuf[s * N_KV_HEADS + gi, 0:WINDOW, :] = vc_ref[s, :, hs].astype(op_dtype)

    h = h_ref[...]
    n = _rms(h, g_ref[...])
    proj = _mm(n, win_ref[...], hp)
    o0 = 2 * D_SGU
    o1 = o0 + N_Q_HEADS * HEAD_DIM
    o2 = o1 + N_KV_HEADS * HEAD_DIM
    zp = proj[:, :o0]
    z = 0.5 * zp * (1.0 + lax.erf(zp * (0.5 ** 0.5)))
    u = z[:, :D_SGU]
    v = _layer_norm(z[:, D_SGU:], lng_ref[...], lnb_ref[...])
    if emit_sgu_v:
        sguv_ref[...] = v

    n_sgu = -(-tm // SGU_CHUNK)
    if tm % SGU_CHUNK:
        vs_scr[...] = jnp.zeros(vs_scr.shape, vs_scr.dtype)
    for s in range(nseq):
        v0 = s * n_sgu * SGU_CHUNK
        vs_scr[v0:v0 + tm, :] = v[s * tm:(s + 1) * tm].astype(vs_scr.dtype)
    ri = lax.broadcasted_iota(jnp.int32, (SGU_CHUNK, SGU_CHUNK), 0)
    ci = lax.broadcasted_iota(jnp.int32, (SGU_CHUNK, SGU_CHUNK), 1)
    for gi in range(N_SGU_HEADS):
        ls = slice(gi * SGU_HEAD, (gi + 1) * SGU_HEAD)
        wsg = jnp.where(ri >= ci, ws_ref[gi], 0.0).astype(vs_scr.dtype)
        for s in range(nseq):
            for c in range(n_sgu):
                rows = min(SGU_CHUNK, tm - c * SGU_CHUNK)
                v0 = (s * n_sgu + c) * SGU_CHUNK
                mixed = _mm(wsg, vs_scr[v0:v0 + SGU_CHUNK, ls], hp) + bs_ref[:, ls]
                r0 = s * tm + c * SGU_CHUNK
                cat[r0:r0 + rows, ls] = (u[r0:r0 + rows, ls] * mixed[0:rows]).astype(cat.dtype)

    k = proj[:, o1:o2]
    vv = proj[:, o2:]
    q = proj[:, o0:o1].astype(op_dtype)
    nk = WINDOW + cq
    qrow = lax.broadcasted_iota(jnp.int32, (Q_PER_KV * cq, 1), 0)
    for s in range(nseq):
        knew_ref[s] = k[(s + 1) * tm - kv_rows:(s + 1) * tm, :]
        vnew_ref[s] = vv[(s + 1) * tm - kv_rows:(s + 1) * tm, :]
        for gi in range(N_KV_HEADS):
            hs = slice(gi * HEAD_DIM, (gi + 1) * HEAD_DIM)
            kbuf[s * N_KV_HEADS + gi, WINDOW:WINDOW + tm, :] = k[s * tm:(s + 1) * tm, hs].astype(op_dtype)
            vbuf[s * N_KV_HEADS + gi, WINDOW:WINDOW + tm, :] = vv[s * tm:(s + 1) * tm, hs].astype(op_dtype)
    for gi in range(N_KV_HEADS):
        sk = jnp.zeros((Q_PER_KV * cq, 1), F32)
        for i in range(Q_PER_KV):
            sk = jnp.where((qrow >= i * cq) & (qrow < (i + 1) * cq), sink_ref[gi * Q_PER_KV + i], sk)
        for s in range(nseq):
            kv = s * N_KV_HEADS + gi
            for c in range(tm // cq):
                r0 = c * cq
                g0 = s * tm + r0
                qs = jnp.concatenate(
                    [q[g0:g0 + cq, (gi * Q_PER_KV + i) * HEAD_DIM:(gi * Q_PER_KV + i + 1) * HEAD_DIM]
                     for i in range(Q_PER_KV)], axis=0)
                sc = _mm_t(qs, kbuf[kv, r0:r0 + nk, :], hp) * (HEAD_DIM ** -0.5)
                if mask_first and r0 < WINDOW:
                    key_pos = t * tm + (r0 - WINDOW) + lax.broadcasted_iota(jnp.int32, (1, nk), 1)
                    sc = jnp.where(key_pos >= 0, sc, -jnp.inf)
                m = jnp.maximum(jnp.max(sc, axis=-1, keepdims=True), sk)
                e = jnp.exp(sc - m)
                p = e / (jnp.sum(e, axis=-1, keepdims=True) + jnp.exp(sk - m))
                o = _mm(p, vbuf[kv, r0:r0 + nk, :], hp)
                for i in range(Q_PER_KV):
                    hd = gi * Q_PER_KV + i
                    attn[g0:g0 + cq, hd * HEAD_DIM:(hd + 1) * HEAD_DIM] = o[i * cq:(i + 1) * cq, :]
    cat[:, D_SGU:] = attn[...].astype(cat.dtype)

    out_ref[...] = h + _mm(cat[...], wout_ref[...], hp)

    for kv in range(nseq * N_KV_HEADS):
        ktail = kbuf[kv, tm:tm + WINDOW, :]
        vtail = vbuf[kv, tm:tm + WINDOW, :]
        kbuf[kv, 0:WINDOW, :] = ktail
        vbuf[kv, 0:WINDOW, :] = vtail


def _odd_mixer(h, nb, tlen, sinks, g, w_in, lng, lnb, ws, bs, w_out, kc, vc, tm, cq, mask_first,
               emit_sgu_v, nseq, hp):
    nt = tlen // tm
    rows = nseq * tm
    kv_rows = min(WINDOW, tlen)
    odd_in = w_in.shape[1]
    row = lambda b, t: (b * nt + t, 0)
    const2 = lambda b, t: (0, 0)
    const3 = lambda b, t: (0, 0, 0)
    per_b = lambda b, t: (b, 0, 0)
    kern = functools.partial(_odd_mixer_kernel, nseq=nseq, tm=tm, cq=cq, kv_rows=kv_rows,
                             mask_first=mask_first, emit_sgu_v=emit_sgu_v, hp=hp)
    kvw = N_KV_HEADS * HEAD_DIM
    out_specs = [
        pl.BlockSpec((rows, D_MODEL), row),
        pl.BlockSpec((nseq, kv_rows, kvw), per_b),
        pl.BlockSpec((nseq, kv_rows, kvw), per_b),
    ]
    out_shape = [
        jax.ShapeDtypeStruct((nb * tlen, D_MODEL), F32),
        jax.ShapeDtypeStruct((nb, kv_rows, kvw), F32),
        jax.ShapeDtypeStruct((nb, kv_rows, kvw), F32),
    ]
    if emit_sgu_v:
        out_specs.append(pl.BlockSpec((rows, D_SGU), row))
        out_shape.append(jax.ShapeDtypeStruct((nb * tlen, D_SGU), F32))
    n_sgu = -(-tm // SGU_CHUNK)
    op_dtype = F32 if hp else BF16
    return pl.pallas_call(
        kern,
        grid=(nb // nseq, nt),
        in_specs=[
            pl.BlockSpec(memory_space=pltpu.SMEM),
            pl.BlockSpec((rows, D_MODEL), row),
            pl.BlockSpec((1, D_MODEL), const2),
            pl.BlockSpec((D_MODEL, odd_in), const2),
            pl.BlockSpec((1, D_SGU), const2),
            pl.BlockSpec((1, D_SGU), const2),
            pl.BlockSpec((N_SGU_HEADS, SGU_CHUNK, SGU_CHUNK), const3),
            pl.BlockSpec((SGU_CHUNK, D_SGU), const2),
            pl.BlockSpec((D_MODEL, D_MODEL), const2),
            pl.BlockSpec((nseq, WINDOW, kvw), per_b),
            pl.BlockSpec((nseq, WINDOW, kvw), per_b),
        ],
        out_specs=out_specs,
        out_shape=out_shape,
        scratch_shapes=[
            pltpu.VMEM((nseq * N_KV_HEADS, WINDOW + tm, HEAD_DIM), op_dtype),
            pltpu.VMEM((nseq * N_KV_HEADS, WINDOW + tm, HEAD_DIM), op_dtype),
            pltpu.VMEM((nseq * n_sgu * SGU_CHUNK, D_SGU), op_dtype),
            pltpu.VMEM((rows, N_Q_HEADS * HEAD_DIM), F32),
            pltpu.VMEM((rows, D_MODEL), op_dtype),
        ],
        compiler_params=pltpu.CompilerParams(
            dimension_semantics=("arbitrary", "arbitrary"), vmem_limit_bytes=VMEM_LIMIT),
        name="odd_mixer",
    )(sinks, h, g, w_in, lng, lnb, ws, bs, w_out, kc, vc)


def _ffn_kernel(*refs, routed, final_norm, n_chunks, hp_router, hp_experts):
    refs = list(refs)
    h_ref, g_ref = refs[0], refs[1]
    pos = 2
    if routed:
        router_ref = refs[pos]
        pos += 1
    wg_ref, wu_ref, wd_ref = refs[pos:pos + 3]
    pos += 3
    if final_norm:
        gf_ref = refs[pos]
        pos += 1
    out_ref = refs[pos]
    n_scr, acc = refs[pos + 1], refs[pos + 2]
    if routed:
        comb = refs[pos + 3]
    j = pl.program_id(1)

    @pl.when(j == 0)
    def _():
        n = _rms(h_ref[...], g_ref[...])
        n_scr[...] = n.astype(n_scr.dtype)
        if routed:
            logits = _mm(n, router_ref[...], hp_router)
            lane = lax.broadcasted_iota(jnp.int32, logits.shape, 1)
            logits = jnp.where(lane < N_EXPERTS, logits, -jnp.inf)
            m1 = jnp.max(logits, axis=-1, keepdims=True)
            i1 = jnp.min(jnp.where(logits == m1, lane, LANES), axis=-1, keepdims=True)
            rest = jnp.where(lane == i1, -jnp.inf, logits)
            m2 = jnp.max(rest, axis=-1, keepdims=True)
            i2 = jnp.min(jnp.where(rest == m2, lane, LANES), axis=-1, keepdims=True)
            e2 = jnp.exp(m2 - m1)
            den = 1.0 + e2
            comb[...] = jnp.where(lane == i1, 1.0 / den, 0.0) + jnp.where(lane == i2, e2 / den, 0.0)

    x = n_scr[...]
    gate = _mm(x, wg_ref[...], hp_experts)
    up = _mm(x, wu_ref[...], hp_experts)
    act = (gate * _sigmoid(gate)) * up
    y = _mm(act, wd_ref[...], hp_experts)
    if routed:
        cmb = comb[...]
        lane = lax.broadcasted_iota(jnp.int32, cmb.shape, 1)
        y = y * jnp.sum(jnp.where(lane == j, cmb, 0.0), axis=-1, keepdims=True)

    @pl.when(j == 0)
    def _():
        acc[...] = y

    @pl.when(j > 0)
    def _():
        acc[...] += y

    @pl.when(j == n_chunks - 1)
    def _():
        res = h_ref[...] + acc[...]
        if final_norm:
            res = _rms(res, gf_ref[...])
        out_ref[...] = res


def _ffn(h, g, wg, wu, wd, tm, router=None, g_final=None, hp_router=False, hp_experts=False):
    rows = h.shape[0]
    routed = router is not None
    final_norm = g_final is not None
    row = lambda i, j: (i, 0)
    const2 = lambda i, j: (0, 0)
    in_specs = [pl.BlockSpec((tm, D_MODEL), row), pl.BlockSpec((1, D_MODEL), const2)]
    args = [h, g]
    if routed:
        n_chunks = N_EXPERTS
        in_specs.append(pl.BlockSpec((D_MODEL, LANES), const2))
        args.append(router)
        in_specs += [
            pl.BlockSpec((None, D_MODEL, D_FF_EXPERT), lambda i, j: (j, 0, 0)),
            pl.BlockSpec((None, D_MODEL, D_FF_EXPERT), lambda i, j: (j, 0, 0)),
            pl.BlockSpec((None, D_FF_EXPERT, D_MODEL), lambda i, j: (j, 0, 0)),
        ]
    else:
        n_chunks = D_FF // D_FF_EXPERT
        in_specs += [
            pl.BlockSpec((D_MODEL, D_FF_EXPERT), lambda i, j: (0, j)),
            pl.BlockSpec((D_MODEL, D_FF_EXPERT), lambda i, j: (0, j)),
            pl.BlockSpec((D_FF_EXPERT, D_MODEL), lambda i, j: (j, 0)),
        ]
    args += [wg, wu, wd]
    if final_norm:
        in_specs.append(pl.BlockSpec((1, D_MODEL), const2))
        args.append(g_final)
    scratch = [pltpu.VMEM((tm, D_MODEL), F32 if hp_experts else BF16), pltpu.VMEM((tm, D_MODEL), F32)]
    if routed:
        scratch.append(pltpu.VMEM((tm, LANES), F32))
    kern = functools.partial(_ffn_kernel, routed=routed, final_norm=final_norm, n_chunks=n_chunks,
                             hp_router=hp_router, hp_experts=hp_experts)
    return pl.pallas_call(
        kern,
        grid=(rows // tm, n_chunks),
        in_specs=in_specs,
        out_specs=pl.BlockSpec((tm, D_MODEL), row),
        out_shape=jax.ShapeDtypeStruct((rows, D_MODEL), F32),
        scratch_shapes=scratch,
        compiler_params=pltpu.CompilerParams(
            dimension_semantics=("arbitrary", "arbitrary"), vmem_limit_bytes=VMEM_LIMIT),
        name="moe_ffn" if routed else "dense_ffn",
    )(*args)


N_GROUPS = N_EXPERTS * N_EXPERTS
PAIR_TM = 256
PERM_CHUNK = 4096
PERM_WINDOW = 32


def _route_kernel(h_ref, g_ref, router_ref, grp_ref, rank_ref, cnt_ref, carry, *, tm):
    i = pl.program_id(0)

    @pl.when(i == 0)
    def _():
        carry[...] = jnp.zeros(carry.shape, F32)

    n = _rms(h_ref[...], g_ref[...])
    logits = _mm(n, router_ref[...])
    l8 = logits.T[0:N_EXPERTS, :]
    sub = lax.broadcasted_iota(jnp.int32, l8.shape, 0)
    m1 = jnp.max(l8, axis=0, keepdims=True)
    i1 = jnp.min(jnp.where(l8 == m1, sub, N_EXPERTS), axis=0, keepdims=True)
    rest = jnp.where(sub == i1, -jnp.inf, l8)
    m2 = jnp.max(rest, axis=0, keepdims=True)
    i2 = jnp.min(jnp.where(rest == m2, sub, N_EXPERTS), axis=0, keepdims=True)
    grp = jnp.minimum(i1, i2) * N_EXPERTS + jnp.maximum(i1, i2)
    grp_ref[...] = grp

    gid = lax.broadcasted_iota(jnp.int32, (N_GROUPS, tm), 0)
    onehot = jnp.where(gid == grp, 1.0, 0.0)
    rs = lax.broadcasted_iota(jnp.int32, (tm, tm), 0)
    cs = lax.broadcasted_iota(jnp.int32, (tm, tm), 1)
    upper = jnp.where(rs <= cs, 1.0, 0.0)
    incl = _mm(onehot, upper)
    before = carry[...]
    rank = jnp.sum(onehot * (incl - 1.0 + before), axis=0, keepdims=True)
    rank_ref[...] = rank.astype(jnp.int32)
    total = before + jnp.sum(onehot, axis=1, keepdims=True)
    carry[...] = total
    cnt_ref[...] = jnp.broadcast_to(total, cnt_ref.shape).astype(jnp.int32)


def _route(h, g, router, tm):
    rows = h.shape[0]
    const2 = lambda i: (0, 0)
    return pl.pallas_call(
        functools.partial(_route_kernel, tm=tm),
        grid=(rows // tm,),
        in_specs=[
            pl.BlockSpec((tm, D_MODEL), lambda i: (i, 0)),
            pl.BlockSpec((1, D_MODEL), const2),
            pl.BlockSpec((D_MODEL, LANES), const2),
        ],
        out_specs=[
            pl.BlockSpec((1, tm), lambda i: (0, i)),
            pl.BlockSpec((1, tm), lambda i: (0, i)),
            pl.BlockSpec((N_GROUPS, LANES), const2),
        ],
        out_shape=[
            jax.ShapeDtypeStruct((1, rows), jnp.int32),
            jax.ShapeDtypeStruct((1, rows), jnp.int32),
            jax.ShapeDtypeStruct((N_GROUPS, LANES), jnp.int32),
        ],
        scratch_shapes=[pltpu.VMEM((N_GROUPS, 1), F32)],
        compiler_params=pltpu.CompilerParams(
            dimension_semantics=("arbitrary",), vmem_limit_bytes=VMEM_LIMIT),
        name="route",
    )(h, g, router)


def _permute_kernel(idx_ref, x_hbm, *rest, gather, chunk):
    out_hbm, sem = rest[-2], rest[-1]
    base = pl.program_id(0) * chunk

    def row_copy(r, j):
        if gather:
            return pltpu.make_async_copy(x_hbm.at[pl.ds(j, 1)], out_hbm.at[pl.ds(base + r, 1)], sem)
        return pltpu.make_async_copy(x_hbm.at[pl.ds(base + r, 1)], out_hbm.at[pl.ds(j, 1)], sem)

    def issue(r, carry):
        row_copy(r, idx_ref[0, 0, r]).start()

        @pl.when(r >= PERM_WINDOW)
        def _():
            row_copy(0, 0).wait()
        return carry

    lax.fori_loop(0, chunk, issue, 0, unroll=8)

    def drain(r, carry):
        row_copy(0, 0).wait()
        return carry

    lax.fori_loop(0, PERM_WINDOW, drain, 0, unroll=8)


def _permute(x, idx, out_rows, gather, init=None):
    n = idx.shape[0]
    chunk = min(PERM_CHUNK, n)
    idx3 = idx.reshape(n // chunk, 1, chunk)
    in_specs = [
        pl.BlockSpec((1, 1, chunk), lambda i: (i, 0, 0), memory_space=pltpu.SMEM),
        pl.BlockSpec(memory_space=pl.ANY),
    ]
    args = [idx3, x]
    aliases = {}
    if init is not None:
        in_specs.append(pl.BlockSpec(memory_space=pl.ANY))
        args.append(init)
        aliases = {2: 0}
    return pl.pallas_call(
        functools.partial(_permute_kernel, gather=gather, chunk=chunk),
        grid=(n // chunk,),
        in_specs=in_specs,
        out_specs=pl.BlockSpec(memory_space=pl.ANY),
        out_shape=jax.ShapeDtypeStruct((out_rows, x.shape[1]), x.dtype),
        scratch_shapes=[pltpu.SemaphoreType.DMA],
        input_output_aliases=aliases,
        compiler_params=pltpu.CompilerParams(dimension_semantics=("arbitrary",)),
        name="gather_rows" if gather else "scatter_rows",
    )(*args)


def _pair_ffn_kernel(ta_ref, tb_ref, nv_ref, x_ref, g_ref, router_ref,
                     wga_ref, wua_ref, wda_ref, wgb_ref, wub_ref, wdb_ref, *rest, final_norm):
    if final_norm:
        gf_ref, out_ref = rest
    else:
        (out_ref,) = rest
    i = pl.program_id(0)

    @pl.when(i < nv_ref[0])
    def _():
        x = x_ref[...]
        nb16 = _rms(x, g_ref[...]).astype(BF16)
        logits = _mm(nb16, router_ref[...])
        lane = lax.broadcasted_iota(jnp.int32, logits.shape, 1)
        la = jnp.sum(jnp.where(lane == ta_ref[i], logits, 0.0), axis=-1, keepdims=True)
        lb = jnp.sum(jnp.where(lane == tb_ref[i], logits, 0.0), axis=-1, keepdims=True)
        m = jnp.maximum(la, lb)
        ea = jnp.exp(la - m)
        eb = jnp.exp(lb - m)
        den = ea + eb

        def expert(wg_ref, wu_ref, wd_ref):
            gate = _mm(nb16, wg_ref[...])
            up = _mm(nb16, wu_ref[...])
            act = (gate * _sigmoid(gate)) * up
            return _mm(act, wd_ref[...])

        y = (ea / den) * expert(wga_ref, wua_ref, wda_ref)
        y = y + (eb / den) * expert(wgb_ref, wub_ref, wdb_ref)
        res = x + y
        if final_norm:
            res = _rms(res, gf_ref[...])
        out_ref[...] = res

    @pl.when(i >= nv_ref[0])
    def _():
        out_ref[...] = jnp.zeros(out_ref.shape, F32)


def _pair_ffn(xs, ta, tb, nvalid, g, router, wg, wu, wd, g_final=None):
    rows = xs.shape[0]
    tm = PAIR_TM
    final_norm = g_final is not None
    row = lambda i, ta, tb, nv: (i, 0)
    const2 = lambda i, ta, tb, nv: (0, 0)
    wa = lambda i, ta, tb, nv: (ta[i], 0, 0)
    wb = lambda i, ta, tb, nv: (tb[i], 0, 0)
    in_specs = [
        pl.BlockSpec((tm, D_MODEL), row),
        pl.BlockSpec((1, D_MODEL), const2),
        pl.BlockSpec((D_MODEL, LANES), const2),
        pl.BlockSpec((None, D_MODEL, D_FF_EXPERT), wa),
        pl.BlockSpec((None, D_MODEL, D_FF_EXPERT), wa),
        pl.BlockSpec((None, D_FF_EXPERT, D_MODEL), wa),
        pl.BlockSpec((None, D_MODEL, D_FF_EXPERT), wb),
        pl.BlockSpec((None, D_MODEL, D_FF_EXPERT), wb),
        pl.BlockSpec((None, D_FF_EXPERT, D_MODEL), wb),
    ]
    args = [xs, g, router, wg, wu, wd, wg, wu, wd]
    if final_norm:
        in_specs.append(pl.BlockSpec((1, D_MODEL), const2))
        args.append(g_final)
    return pl.pallas_call(
        functools.partial(_pair_ffn_kernel, final_norm=final_norm),
        grid_spec=pltpu.PrefetchScalarGridSpec(
            num_scalar_prefetch=3,
            grid=(rows // tm,),
            in_specs=in_specs,
            out_specs=pl.BlockSpec((tm, D_MODEL), row),
        ),
        out_shape=jax.ShapeDtypeStruct((rows, D_MODEL), F32),
        compiler_params=pltpu.CompilerParams(
            dimension_semantics=("arbitrary",), vmem_limit_bytes=VMEM_LIMIT),
        name="pair_ffn",
    )(ta, tb, nvalid, *args)


def _sparse_moe(h, g, router, wg, wu, wd, g_final=None):
    rows = h.shape[0]
    tm = PAIR_TM
    n_pairs = N_EXPERTS * (N_EXPERTS - 1) // 2
    n_tiles = rows // tm + n_pairs
    grp, rank, cnt = _route(h, g, router, 512)
    counts = cnt[:, 0]
    padded = (counts + tm - 1) // tm * tm
    ends = jnp.cumsum(padded)
    dest = (ends - padded)[grp[0]] + rank[0]
    nvalid = ends[-1] // tm
    tile_grp = jnp.searchsorted(ends, jnp.arange(n_tiles, dtype=jnp.int32) * tm, side="right")
    last_grp = jnp.max(jnp.where(counts > 0, jnp.arange(N_GROUPS), 0))
    tile_grp = jnp.minimum(tile_grp, last_grp).astype(jnp.int32)
    ta = tile_grp // N_EXPERTS
    tb = tile_grp % N_EXPERTS
    xs = _permute(h, dest, n_tiles * tm, gather=False,
                  init=jnp.zeros((n_tiles * tm, D_MODEL), F32))
    ys = _pair_ffn(xs, ta, tb, nvalid.reshape(1).astype(jnp.int32), g, router, wg, wu, wd,
                   g_final=g_final)
    return _permute(ys, dest, rows, gather=True)


def kernel(x_prompt, x_sample, state_conv, state_pool, cache_swa_k, cache_swa_v, norm_mix, norm_ffn, norm_final, even_w_in, conv_w, conv_b, conv_ln_g, conv_ln_b, pool_w, pool_scale, even_w_out, ffn_w_gate, ffn_w_up, ffn_w_down, odd_w_in, sgu_ln_g, sgu_ln_b, sgu_w, sgu_b, attn_sinks, odd_w_out, router_w, exp_w_gate, exp_w_up, exp_w_down):
    bp, tp, _ = x_prompt.shape
    bs, ts, _ = x_sample.shape
    depth = norm_mix.shape[0]
    tm_p = 512
    kvw = N_KV_HEADS * HEAD_DIM

    h_p = x_prompt.reshape(bp * tp, D_MODEL)
    h_s = x_sample.reshape(bs * ts, D_MODEL)
    b16 = lambda w: w.astype(BF16)
    r2 = lambda v: v.reshape(1, -1)
    pad_router = lambda r: jnp.pad(r, ((0, 0), (0, LANES - N_EXPERTS)))

    conv_p, conv_s, pool_p, pool_s = [], [], [], []
    k_p, v_p, k_s, v_s, sgu_s = [], [], [], [], []
    for layer in range(depth):
        i = layer // 2
        last = layer == depth - 1
        gf = r2(norm_final) if last else None
        if layer % 2 == 0:
            small = (conv_w[i], r2(conv_b[i]), r2(conv_ln_g[i]), r2(conv_ln_b[i]))
            h_p, cst, pst = _even_mixer(
                h_p, bp, tp, 0, r2(norm_mix[layer]), b16(even_w_in[i]), *small, b16(pool_w[i]),
                r2(pool_scale[i]), b16(even_w_out[i]),
                jnp.zeros((bp, CONV_HIST, D_CONV), F32), jnp.zeros((bp, POOL_HIST, D_POOL), F32),
                tm_p, nseq=1, hp=False)
            conv_p.append(cst)
            pool_p.append(pst)
            h_s, cst, pst = _even_mixer(
                h_s, bs, ts, PAST_LEN, r2(norm_mix[layer]), even_w_in[i], *small, pool_w[i],
                r2(pool_scale[i]), even_w_out[i], state_conv[i], state_pool[i],
                ts, nseq=bs, hp=True)
            conv_s.append(cst)
            pool_s.append(pst)
            gn = r2(norm_ffn[layer])
            h_p = _ffn(h_p, gn, b16(ffn_w_gate[i]), b16(ffn_w_up[i]), b16(ffn_w_down[i]), tm_p,
                       g_final=gf)
            h_s = _ffn(h_s, gn, ffn_w_gate[i], ffn_w_up[i], ffn_w_down[i], bs * ts, g_final=gf,
                       hp_experts=True)
        else:
            bias = jnp.repeat(sgu_b[i].T, SGU_HEAD, axis=1)
            small = (r2(sgu_ln_g[i]), r2(sgu_ln_b[i]), sgu_w[i], bias)
            zkv = jnp.zeros((bp, WINDOW, kvw), F32)
            h_p, kn, vn = _odd_mixer(
                h_p, bp, tp, attn_sinks[i], r2(norm_mix[layer]), b16(odd_w_in[i]), *small,
                b16(odd_w_out[i]), zkv, zkv, tm_p, CHUNK, True, False, nseq=1, hp=False)
            k_p.append(kn.reshape(bp, -1, N_KV_HEADS, HEAD_DIM))
            v_p.append(vn.reshape(bp, -1, N_KV_HEADS, HEAD_DIM))
            h_s, kn, vn, sv = _odd_mixer(
                h_s, bs, ts, attn_sinks[i], r2(norm_mix[layer]), odd_w_in[i], *small, odd_w_out[i],
                cache_swa_k[i].reshape(bs, WINDOW, kvw), cache_swa_v[i].reshape(bs, WINDOW, kvw),
                ts, ts, False, True, nseq=bs, hp=True)
            k_s.append(kn.reshape(bs, ts, N_KV_HEADS, HEAD_DIM))
            v_s.append(vn.reshape(bs, ts, N_KV_HEADS, HEAD_DIM))
            sgu_s.append(sv.reshape(bs, ts, D_SGU))
            gn = r2(norm_ffn[layer])
            wg, wu, wd = b16(exp_w_gate[i]), b16(exp_w_up[i]), b16(exp_w_down[i])
            h_p = _sparse_moe(h_p, gn, pad_router(b16(router_w[i])), wg, wu, wd, g_final=gf)
            if last:
                h_s = _ffn(h_s, gn, wg, wu, wd, bs * ts, router=pad_router(router_w[i]),
                           g_final=gf, hp_router=True)
            else:
                h_s = _ffn(h_s, gn, exp_w_gate[i], exp_w_up[i], exp_w_down[i], bs * ts,
                           router=pad_router(router_w[i]), g_final=gf, hp_router=True,
                           hp_experts=True)

    return (h_p.reshape(bp, tp, D_MODEL), h_s.reshape(bs, ts, D_MODEL),
            jnp.stack(conv_p), jnp.stack(conv_s),
            jnp.stack(pool_p), jnp.stack(pool_s),
            jnp.stack(k_p), jnp.stack(v_p),
            jnp.stack(k_s), jnp.stack(v_s),
            jnp.stack(sgu_s))
```

```python
import functools

import jax
import jax.numpy as jnp
from jax import lax
from jax.experimental import pallas as pl
from jax.experimental.pallas import tpu as pltpu

F32 = jnp.float32
BF16 = jnp.bfloat16

D_MODEL = 1024
PAST_LEN = 1024
CHUNK = 64
D_CONV = 512
CONV_WIDTH = 31
CONV_HIST = CONV_WIDTH - 1
D_POOL = 512
POOL_WINDOWS = (2, 4, 8, 16)
POOL_GROUP = 128
POOL_HIST = 15
D_SGU = 512
SGU_CHUNK = 128
N_SGU_HEADS = 4
SGU_HEAD = 128
HEAD_DIM = 64
N_Q_HEADS = 8
N_KV_HEADS = 2
Q_PER_KV = 4
WINDOW = 128
D_FF = 2816
N_EXPERTS = 8
D_FF_EXPERT = 1408
EPS = 1e-6

LANES = 128
SUBLANES = 8
A_PAD = 32
P_PAD = 16
VMEM_LIMIT = 56 * 1024 * 1024


def _sigmoid(x):
    return 1.0 / (1.0 + jnp.exp(-x))


def _rms(x, g):
    ms = jnp.mean(x * x, axis=-1, keepdims=True)
    return x * lax.rsqrt(ms + EPS) * g


def _layer_norm(x, g, b):
    mu = jnp.mean(x, axis=-1, keepdims=True)
    xc = x - mu
    var = jnp.mean(xc * xc, axis=-1, keepdims=True)
    return xc * lax.rsqrt(var + EPS) * g + b


def _mm(x, w, hp=False):
    if hp:
        return jnp.dot(x, w, preferred_element_type=F32, precision=lax.Precision.HIGHEST)
    return jnp.dot(x.astype(BF16), w.astype(BF16), preferred_element_type=F32)


def _mm_t(x, y, hp=False):
    dims = (((1,), (1,)), ((), ()))
    if hp:
        return lax.dot_general(x, y, dims, preferred_element_type=F32,
                               precision=lax.Precision.HIGHEST)
    return lax.dot_general(x.astype(BF16), y.astype(BF16), dims, preferred_element_type=F32)


def _even_mixer_kernel(h_ref, g_ref, win_ref, cw_ref, cb_ref, lng_ref, lnb_ref, pw_ref, ps_ref,
                       wout_ref, chist_ref, phist_ref,
                       out_ref, cstate_ref, pstate_ref,
                       abuf, pbuf, ash, cat, *, nseq, tm, rc, start_pos, hp):
    t = pl.program_id(1)

    @pl.when(t == 0)
    def _():
        for s in range(nseq):
            abuf[s, 0:A_PAD - CONV_HIST, :] = jnp.zeros((A_PAD - CONV_HIST, D_CONV), F32)
            abuf[s, A_PAD - CONV_HIST:A_PAD, :] = chist_ref[s]
            pbuf[s, 0:P_PAD - POOL_HIST, :] = jnp.zeros((P_PAD - POOL_HIST, D_POOL), F32)
            pbuf[s, P_PAD - POOL_HIST:P_PAD, :] = phist_ref[s]

    h = h_ref[...]
    n = _rms(h, g_ref[...])
    proj = _mm(n, win_ref[...], hp)
    a = proj[:, :D_CONV] * _sigmoid(proj[:, D_CONV:2 * D_CONV])
    for s in range(nseq):
        abuf[s, A_PAD:A_PAD + tm, :] = a[s * tm:(s + 1) * tm]
        pbuf[s, P_PAD:P_PAD + tm, :] = proj[s * tm:(s + 1) * tm, 2 * D_CONV:]

    off = A_PAD - CONV_HIST
    n_sh = A_PAD + tm - SUBLANES
    for s in range(nseq):
        for j in range(1, SUBLANES):
            for c0 in range(0, n_sh, rc):
                nr = min(rc, n_sh - c0)
                ash[j - 1, c0:c0 + nr, :] = abuf[s, c0 + j:c0 + j + nr, :]

        def tap(k, r0, ls):
            q, j = divmod(off + k, SUBLANES)
            lo = r0 + q * SUBLANES
            if j == 0:
                return abuf[s, lo:lo + rc, ls]
            return ash[j - 1, lo:lo + rc, ls]

        for r0 in range(0, tm, rc):
            o0 = s * tm + r0
            cparts = []
            for lb in range(D_CONV // LANES):
                ls = slice(lb * LANES, (lb + 1) * LANES)
                acc = tap(0, r0, ls) * cw_ref[0:1, ls]
                for k in range(1, CONV_WIDTH):
                    acc = acc + tap(k, r0, ls) * cw_ref[k:k + 1, ls]
                cparts.append(acc)
            c = jnp.concatenate(cparts, axis=-1) + cb_ref[...]
            c = _layer_norm(c, lng_ref[...], lnb_ref[...])
            c = c * _sigmoid(c)
            cat[o0:o0 + rc, 0:D_CONV] = c.astype(cat.dtype)

            pos1 = (start_pos + 1 + t * tm + r0
                    + lax.broadcasted_iota(jnp.int32, (rc, 1), 0)).astype(F32)
            for gi, w in enumerate(POOL_WINDOWS):
                ls = slice(gi * POOL_GROUP, (gi + 1) * POOL_GROUP)
                cur = pbuf[s, P_PAD + r0:P_PAD + r0 + rc, ls]
                acc = cur
                for i in range(1, w):
                    acc = acc + pbuf[s, P_PAD + r0 - i:P_PAD + r0 - i + rc, ls]
                pooled = acc / jnp.minimum(float(w), pos1) - cur
                mixed = _mm(pooled, pw_ref[gi], hp)
                cat[o0:o0 + rc, D_CONV + gi * POOL_GROUP:D_CONV + (gi + 1) * POOL_GROUP] = (
                    mixed * ps_ref[0:1, ls]).astype(cat.dtype)

    out_ref[...] = h + _mm(cat[...], wout_ref[...], hp)

    for s in range(nseq):
        atail = abuf[s, tm:tm + A_PAD, :]
        ptail = pbuf[s, tm:tm + P_PAD, :]
        abuf[s, 0:A_PAD, :] = atail
        pbuf[s, 0:P_PAD, :] = ptail
        cstate_ref[s] = atail[A_PAD - CONV_HIST:, :]
        pstate_ref[s] = ptail[P_PAD - POOL_HIST:, :]


def _even_mixer(h, nb, tlen, start_pos, g, w_in, cw, cb, lng, lnb, pw, ps, w_out, chist, phist, tm,
                nseq, hp):
    rc = min(64, tm)
    nt = tlen // tm
    rows = nseq * tm
    row = lambda b, t: (b * nt + t, 0)
    const2 = lambda b, t: (0, 0)
    const3 = lambda b, t: (0, 0, 0)
    per_b = lambda b, t: (b, 0, 0)
    kern = functools.partial(_even_mixer_kernel, nseq=nseq, tm=tm, rc=rc, start_pos=start_pos, hp=hp)
    return pl.pallas_call(
        kern,
        grid=(nb // nseq, nt),
        in_specs=[
            pl.BlockSpec((rows, D_MODEL), row),
            pl.BlockSpec((1, D_MODEL), const2),
            pl.BlockSpec((D_MODEL, 3 * D_CONV), const2),
            pl.BlockSpec((CONV_WIDTH, D_CONV), const2),
            pl.BlockSpec((1, D_CONV), const2),
            pl.BlockSpec((1, D_CONV), const2),
            pl.BlockSpec((1, D_CONV), const2),
            pl.BlockSpec((len(POOL_WINDOWS), POOL_GROUP, POOL_GROUP), const3),
            pl.BlockSpec((1, D_POOL), const2),
            pl.BlockSpec((D_MODEL, D_MODEL), const2),
            pl.BlockSpec((nseq, CONV_HIST, D_CONV), per_b),
            pl.BlockSpec((nseq, POOL_HIST, D_POOL), per_b),
        ],
        out_specs=[
            pl.BlockSpec((rows, D_MODEL), row),
            pl.BlockSpec((nseq, CONV_HIST, D_CONV), per_b),
            pl.BlockSpec((nseq, POOL_HIST, D_POOL), per_b),
        ],
        out_shape=[
            jax.ShapeDtypeStruct((nb * tlen, D_MODEL), F32),
            jax.ShapeDtypeStruct((nb, CONV_HIST, D_CONV), F32),
            jax.ShapeDtypeStruct((nb, POOL_HIST, D_POOL), F32),
        ],
        scratch_shapes=[
            pltpu.VMEM((nseq, A_PAD + tm, D_CONV), F32),
            pltpu.VMEM((nseq, P_PAD + tm, D_POOL), F32),
            pltpu.VMEM((SUBLANES - 1, A_PAD + tm - SUBLANES, D_CONV), F32),
            pltpu.VMEM((rows, D_MODEL), F32 if hp else BF16),
        ],
        compiler_params=pltpu.CompilerParams(
            dimension_semantics=("arbitrary", "arbitrary"), vmem_limit_bytes=VMEM_LIMIT),
        name="even_mixer",
    )(h, g, w_in, cw, cb, lng, lnb, pw, ps, w_out, chist, phist)


def _odd_mixer_kernel(sink_ref, h_ref, g_ref, win_ref, lng_ref, lnb_ref, ws_ref, bs_ref, wout_ref,
                      kc_ref, vc_ref, *rest, nseq, tm, cq, kv_rows, mask_first, emit_sgu_v, hp):
    if emit_sgu_v:
        out_ref, knew_ref, vnew_ref, sguv_ref, kbuf, vbuf, vs_scr, attn, cat = rest
    else:
        out_ref, knew_ref, vnew_ref, kbuf, vbuf, vs_scr, attn, cat = rest
    t = pl.program_id(1)
    op_dtype = kbuf.dtype

    @pl.when(t == 0)
    def _():
        for s in range(nseq):
            for gi in range(N_KV_HEADS):
                hs = slice(gi * HEAD_DIM, (gi + 1) * HEAD_DIM)
                kbuf[s * N_KV_HEADS + gi, 0:WINDOW, :] = kc_ref[s, :, hs].astype(op_dtype)
                vbuf[s * N_KV_HEADS + gi, 0:WINDOW, :] = vc_ref[s, :, hs].astype(op_dtype)

    h = h_ref[...]
    n = _rms(h, g_ref[...])
    proj = _mm(n, win_ref[...], hp)
    o0 = 2 * D_SGU
    o1 = o0 + N_Q_HEADS * HEAD_DIM
    o2 = o1 + N_KV_HEADS * HEAD_DIM
    zp = proj[:, :o0]
    z = 0.5 * zp * (1.0 + lax.erf(zp * (0.5 ** 0.5)))
    u = z[:, :D_SGU]
    v = _layer_norm(z[:, D_SGU:], lng_ref[...], lnb_ref[...])
    if emit_sgu_v:
        sguv_ref[...] = v

    n_sgu = -(-tm // SGU_CHUNK)
    if tm % SGU_CHUNK:
        vs_scr[...] = jnp.zeros(vs_scr.shape, vs_scr.dtype)
    for s in range(nseq):
        v0 = s * n_sgu * SGU_CHUNK
        vs_scr[v0:v0 + tm, :] = v[s * tm:(s + 1) * tm].astype(vs_scr.dtype)
    ri = lax.broadcasted_iota(jnp.int32, (SGU_CHUNK, SGU_CHUNK), 0)
    ci = lax.broadcasted_iota(jnp.int32, (SGU_CHUNK, SGU_CHUNK), 1)
    for gi in range(N_SGU_HEADS):
        ls = slice(gi * SGU_HEAD, (gi + 1) * SGU_HEAD)
        wsg = jnp.where(ri >= ci, ws_ref[gi], 0.0).astype(vs_scr.dtype)
        for s in range(nseq):
            for c in range(n_sgu):
                rows = min(SGU_CHUNK, tm - c * SGU_CHUNK)
                v0 = (s * n_sgu + c) * SGU_CHUNK
                mixed = _mm(wsg, vs_scr[v0:v0 + SGU_CHUNK, ls], hp) + bs_ref[:, ls]
                r0 = s * tm + c * SGU_CHUNK
                cat[r0:r0 + rows, ls] = (u[r0:r0 + rows, ls] * mixed[0:rows]).astype(cat.dtype)

    k = proj[:, o1:o2]
    vv = proj[:, o2:]
    q = proj[:, o0:o1].astype(op_dtype)
    nk = WINDOW + cq
    qrow = lax.broadcasted_iota(jnp.int32, (Q_PER_KV * cq, 1), 0)
    for s in range(nseq):
        knew_ref[s] = k[(s + 1) * tm - kv_rows:(s + 1) * tm, :]
        vnew_ref[s] = vv[(s + 1) * tm - kv_rows:(s + 1) * tm, :]
        for gi in range(N_KV_HEADS):
            hs = slice(gi * HEAD_DIM, (gi + 1) * HEAD_DIM)
            kbuf[s * N_KV_HEADS + gi, WINDOW:WINDOW + tm, :] = k[s * tm:(s + 1) * tm, hs].astype(op_dtype)
            vbuf[s * N_KV_HEADS + gi, WINDOW:WINDOW + tm, :] = vv[s * tm:(s + 1) * tm, hs].astype(op_dtype)
    for gi in range(N_KV_HEADS):
        sk = jnp.zeros((Q_PER_KV * cq, 1), F32)
        for i in range(Q_PER_KV):
            sk = jnp.where((qrow >= i * cq) & (qrow < (i + 1) * cq), sink_ref[gi * Q_PER_KV + i], sk)
        for s in range(nseq):
            kv = s * N_KV_HEADS + gi
            for c in range(tm // cq):
                r0 = c * cq
                g0 = s * tm + r0
                qs = jnp.concatenate(
                    [q[g0:g0 + cq, (gi * Q_PER_KV + i) * HEAD_DIM:(gi * Q_PER_KV + i + 1) * HEAD_DIM]
                     for i in range(Q_PER_KV)], axis=0)
                sc = _mm_t(qs, kbuf[kv, r0:r0 + nk, :], hp) * (HEAD_DIM ** -0.5)
                if mask_first and r0 < WINDOW:
                    key_pos = t * tm + (r0 - WINDOW) + lax.broadcasted_iota(jnp.int32, (1, nk), 1)
                    sc = jnp.where(key_pos >= 0, sc, -jnp.inf)
                m = jnp.maximum(jnp.max(sc, axis=-1, keepdims=True), sk)
                e = jnp.exp(sc - m)
                p = e / (jnp.sum(e, axis=-1, keepdims=True) + jnp.exp(sk - m))
                o = _mm(p, vbuf[kv, r0:r0 + nk, :], hp)
                for i in range(Q_PER_KV):
                    hd = gi * Q_PER_KV + i
                    attn[g0:g0 + cq, hd * HEAD_DIM:(hd + 1) * HEAD_DIM] = o[i * cq:(i + 1) * cq, :]
    cat[:, D_SGU:] = attn[...].astype(cat.dtype)

    out_ref[...] = h + _mm(cat[...], wout_ref[...], hp)

    for kv in range(nseq * N_KV_HEADS):
        ktail = kbuf[kv, tm:tm + WINDOW, :]
        vtail = vbuf[kv, tm:tm + WINDOW, :]
        kbuf[kv, 0:WINDOW, :] = ktail
        vbuf[kv, 0:WINDOW, :] = vtail


def _odd_mixer(h, nb, tlen, sinks, g, w_in, lng, lnb, ws, bs, w_out, kc, vc, tm, cq, mask_first,
               emit_sgu_v, nseq, hp):
    nt = tlen // tm
    rows = nseq * tm
    kv_rows = min(WINDOW, tlen)
    odd_in = w_in.shape[1]
    row = lambda b, t: (b * nt + t, 0)
    const2 = lambda b, t: (0, 0)
    const3 = lambda b, t: (0, 0, 0)
    per_b = lambda b, t: (b, 0, 0)
    kern = functools.partial(_odd_mixer_kernel, nseq=nseq, tm=tm, cq=cq, kv_rows=kv_rows,
                             mask_first=mask_first, emit_sgu_v=emit_sgu_v, hp=hp)
    kvw = N_KV_HEADS * HEAD_DIM
    out_specs = [
        pl.BlockSpec((rows, D_MODEL), row),
        pl.BlockSpec((nseq, kv_rows, kvw), per_b),
        pl.BlockSpec((nseq, kv_rows, kvw), per_b),
    ]
    out_shape = [
        jax.ShapeDtypeStruct((nb * tlen, D_MODEL), F32),
        jax.ShapeDtypeStruct((nb, kv_rows, kvw), F32),
        jax.ShapeDtypeStruct((nb, kv_rows, kvw), F32),
    ]
    if emit_sgu_v:
        out_specs.append(pl.BlockSpec((rows, D_SGU), row))
        out_shape.append(jax.ShapeDtypeStruct((nb * tlen, D_SGU), F32))
    n_sgu = -(-tm // SGU_CHUNK)
    op_dtype = F32 if hp else BF16
    return pl.pallas_call(
        kern,
        grid=(nb // nseq, nt),
        in_specs=[
            pl.BlockSpec(memory_space=pltpu.SMEM),
            pl.BlockSpec((rows, D_MODEL), row),
            pl.BlockSpec((1, D_MODEL), const2),
            pl.BlockSpec((D_MODEL, odd_in), const2),
            pl.BlockSpec((1, D_SGU), const2),
            pl.BlockSpec((1, D_SGU), const2),
            pl.BlockSpec((N_SGU_HEADS, SGU_CHUNK, SGU_CHUNK), const3),
            pl.BlockSpec((SGU_CHUNK, D_SGU), const2),
            pl.BlockSpec((D_MODEL, D_MODEL), const2),
            pl.BlockSpec((nseq, WINDOW, kvw), per_b),
            pl.BlockSpec((nseq, WINDOW, kvw), per_b),
        ],
        out_specs=out_specs,
        out_shape=out_shape,
        scratch_shapes=[
            pltpu.VMEM((nseq * N_KV_HEADS, WINDOW + tm, HEAD_DIM), op_dtype),
            pltpu.VMEM((nseq * N_KV_HEADS, WINDOW + tm, HEAD_DIM), op_dtype),
            pltpu.VMEM((nseq * n_sgu * SGU_CHUNK, D_SGU), op_dtype),
            pltpu.VMEM((rows, N_Q_HEADS * HEAD_DIM), F32),
            pltpu.VMEM((rows, D_MODEL), op_dtype),
        ],
        compiler_params=pltpu.CompilerParams(
            dimension_semantics=("arbitrary", "arbitrary"), vmem_limit_bytes=VMEM_LIMIT),
        name="odd_mixer",
    )(sinks, h, g, w_in, lng, lnb, ws, bs, w_out, kc, vc)


def _ffn_kernel(*refs, routed, final_norm, n_chunks, hp_router, hp_experts):
    refs = list(refs)
    h_ref, g_ref = refs[0], refs[1]
    pos = 2
    if routed:
        router_ref = refs[pos]
        pos += 1
    wg_ref, wu_ref, wd_ref = refs[pos:pos + 3]
    pos += 3
    if final_norm:
        gf_ref = refs[pos]
        pos += 1
    out_ref = refs[pos]
    n_scr, acc = refs[pos + 1], refs[pos + 2]
    if routed:
        comb = refs[pos + 3]
    j = pl.program_id(1)

    @pl.when(j == 0)
    def _():
        n = _rms(h_ref[...], g_ref[...])
        n_scr[...] = n.astype(n_scr.dtype)
        if routed:
            logits = _mm(n, router_ref[...], hp_router)
            lane = lax.broadcasted_iota(jnp.int32, logits.shape, 1)
            logits = jnp.where(lane < N_EXPERTS, logits, -jnp.inf)
            m1 = jnp.max(logits, axis=-1, keepdims=True)
            i1 = jnp.min(jnp.where(logits == m1, lane, LANES), axis=-1, keepdims=True)
            rest = jnp.where(lane == i1, -jnp.inf, logits)
            m2 = jnp.max(rest, axis=-1, keepdims=True)
            i2 = jnp.min(jnp.where(rest == m2, lane, LANES), axis=-1, keepdims=True)
            e2 = jnp.exp(m2 - m1)
            den = 1.0 + e2
            comb[...] = jnp.where(lane == i1, 1.0 / den, 0.0) + jnp.where(lane == i2, e2 / den, 0.0)

    x = n_scr[...]
    gate = _mm(x, wg_ref[...], hp_experts)
    up = _mm(x, wu_ref[...], hp_experts)
    act = (gate * _sigmoid(gate)) * up
    y = _mm(act, wd_ref[...], hp_experts)
    if routed:
        cmb = comb[...]
        lane = lax.broadcasted_iota(jnp.int32, cmb.shape, 1)
        y = y * jnp.sum(jnp.where(lane == j, cmb, 0.0), axis=-1, keepdims=True)

    @pl.when(j == 0)
    def _():
        acc[...] = y

    @pl.when(j > 0)
    def _():
        acc[...] += y

    @pl.when(j == n_chunks - 1)
    def _():
        res = h_ref[...] + acc[...]
        if final_norm:
            res = _rms(res, gf_ref[...])
        out_ref[...] = res


def _ffn(h, g, wg, wu, wd, tm, router=None, g_final=None, hp_router=False, hp_experts=False):
    rows = h.shape[0]
    routed = router is not None
    final_norm = g_final is not None
    row = lambda i, j: (i, 0)
    const2 = lambda i, j: (0, 0)
    in_specs = [pl.BlockSpec((tm, D_MODEL), row), pl.BlockSpec((1, D_MODEL), const2)]
    args = [h, g]
    if routed:
        n_chunks = N_EXPERTS
        in_specs.append(pl.BlockSpec((D_MODEL, LANES), const2))
        args.append(router)
        in_specs += [
            pl.BlockSpec((None, D_MODEL, D_FF_EXPERT), lambda i, j: (j, 0, 0)),
            pl.BlockSpec((None, D_MODEL, D_FF_EXPERT), lambda i, j: (j, 0, 0)),
            pl.BlockSpec((None, D_FF_EXPERT, D_MODEL), lambda i, j: (j, 0, 0)),
        ]
    else:
        n_chunks = D_FF // D_FF_EXPERT
        in_specs += [
            pl.BlockSpec((D_MODEL, D_FF_EXPERT), lambda i, j: (0, j)),
            pl.BlockSpec((D_MODEL, D_FF_EXPERT), lambda i, j: (0, j)),
            pl.BlockSpec((D_FF_EXPERT, D_MODEL), lambda i, j: (j, 0)),
        ]
    args += [wg, wu, wd]
    if final_norm:
        in_specs.append(pl.BlockSpec((1, D_MODEL), const2))
        args.append(g_final)
    scratch = [pltpu.VMEM((tm, D_MODEL), F32 if hp_experts else BF16), pltpu.VMEM((tm, D_MODEL), F32)]
    if routed:
        scratch.append(pltpu.VMEM((tm, LANES), F32))
    kern = functools.partial(_ffn_kernel, routed=routed, final_norm=final_norm, n_chunks=n_chunks,
                             hp_router=hp_router, hp_experts=hp_experts)
    return pl.pallas_call(
        kern,
        grid=(rows // tm, n_chunks),
        in_specs=in_specs,
        out_specs=pl.BlockSpec((tm, D_MODEL), row),
        out_shape=jax.ShapeDtypeStruct((rows, D_MODEL), F32),
        scratch_shapes=scratch,
        compiler_params=pltpu.CompilerParams(
            dimension_semantics=("arbitrary", "arbitrary"), vmem_limit_bytes=VMEM_LIMIT),
        name="moe_ffn" if routed else "dense_ffn",
    )(*args)


N_GROUPS = N_EXPERTS * N_EXPERTS
N_PAIRS = N_EXPERTS * (N_EXPERTS - 1) // 2
PAIR_TM = 256


def _route_kernel(h_ref, g_ref, router_ref, grp_ref, rank_ref, cnt_ref, carry, *, tm):
    i = pl.program_id(0)

    @pl.when(i == 0)
    def _():
        carry[...] = jnp.zeros(carry.shape, F32)

    n = _rms(h_ref[...], g_ref[...])
    logits = _mm(n, router_ref[...])
    l8 = logits.T[0:N_EXPERTS, :]
    sub = lax.broadcasted_iota(jnp.int32, l8.shape, 0)
    m1 = jnp.max(l8, axis=0, keepdims=True)
    i1 = jnp.min(jnp.where(l8 == m1, sub, N_EXPERTS), axis=0, keepdims=True)
    rest = jnp.where(sub == i1, -jnp.inf, l8)
    m2 = jnp.max(rest, axis=0, keepdims=True)
    i2 = jnp.min(jnp.where(rest == m2, sub, N_EXPERTS), axis=0, keepdims=True)
    grp = jnp.minimum(i1, i2) * N_EXPERTS + jnp.maximum(i1, i2)
    grp_ref[...] = grp

    gid = lax.broadcasted_iota(jnp.int32, (N_GROUPS, tm), 0)
    onehot = jnp.where(gid == grp, 1.0, 0.0)
    rs = lax.broadcasted_iota(jnp.int32, (tm, tm), 0)
    cs = lax.broadcasted_iota(jnp.int32, (tm, tm), 1)
    upper = jnp.where(rs <= cs, 1.0, 0.0)
    incl = _mm(onehot, upper)
    before = carry[...]
    rank = jnp.sum(onehot * (incl - 1.0 + before), axis=0, keepdims=True)
    rank_ref[...] = rank.astype(jnp.int32)
    total = before + jnp.sum(onehot, axis=1, keepdims=True)
    carry[...] = total
    cnt_ref[...] = jnp.broadcast_to(total, cnt_ref.shape).astype(jnp.int32)


def _route(h, rows, g, router, tm):
    const2 = lambda i: (0, 0)
    return pl.pallas_call(
        functools.partial(_route_kernel, tm=tm),
        grid=(rows // tm,),
        in_specs=[
            pl.BlockSpec((tm, D_MODEL), lambda i: (i, 0)),
            pl.BlockSpec((1, D_MODEL), const2),
            pl.BlockSpec((D_MODEL, LANES), const2),
        ],
        out_specs=[
            pl.BlockSpec((1, tm), lambda i: (0, i)),
            pl.BlockSpec((1, tm), lambda i: (0, i)),
            pl.BlockSpec((N_GROUPS, LANES), const2),
        ],
        out_shape=[
            jax.ShapeDtypeStruct((1, rows), jnp.int32),
            jax.ShapeDtypeStruct((1, rows), jnp.int32),
            jax.ShapeDtypeStruct((N_GROUPS, LANES), jnp.int32),
        ],
        scratch_shapes=[pltpu.VMEM((N_GROUPS, 1), F32)],
        compiler_params=pltpu.CompilerParams(
            dimension_semantics=("arbitrary",), vmem_limit_bytes=VMEM_LIMIT),
        name="route",
    )(h, g, router)


def _pair_ffn_kernel(ta_ref, tb_ref, nv_ref, src_ref, dst_ref, h_hbm, g_ref, router_ref,
                     wga_ref, wua_ref, wda_ref, wgb_ref, wub_ref, wdb_ref, *rest,
                     final_norm, tm, n_tiles):
    if final_norm:
        gf_ref, out_hbm, xbuf, obuf, gsem, ssem = rest
    else:
        out_hbm, xbuf, obuf, gsem, ssem = rest
    i = pl.program_id(0)
    nv = nv_ref[0]

    def start_gather(slot):
        for r in range(tm):
            pltpu.make_async_copy(h_hbm.at[pl.ds(src_ref[0, 0, r], 1)],
                                  xbuf.at[slot, pl.ds(r, 1)], gsem.at[slot]).start()

    def wait_gather(slot):
        pltpu.make_async_copy(h_hbm.at[pl.ds(0, tm)], xbuf.at[slot], gsem.at[slot]).wait()

    def start_scatter(slot):
        for r in range(tm):
            pltpu.make_async_copy(obuf.at[slot, pl.ds(r, 1)],
                                  out_hbm.at[pl.ds(dst_ref[0, 0, r], 1)], ssem.at[slot]).start()

    def wait_scatter(slot):
        pltpu.make_async_copy(obuf.at[slot], out_hbm.at[pl.ds(0, tm)], ssem.at[slot]).wait()

    @pl.when(i == 0)
    def _():
        start_gather(0)

    @pl.when((i >= 1) & (i <= nv))
    def _():
        k = i - 1
        slot = k % 2
        wait_gather(slot)

        @pl.when(k >= 2)
        def _():
            wait_scatter(slot)

        start_gather(1 - slot)
        x = xbuf[slot]
        nb16 = _rms(x, g_ref[...]).astype(BF16)
        logits = _mm(nb16, router_ref[...])
        lane = lax.broadcasted_iota(jnp.int32, logits.shape, 1)
        la = jnp.sum(jnp.where(lane == ta_ref[k], logits, 0.0), axis=-1, keepdims=True)
        lb = jnp.sum(jnp.where(lane == tb_ref[k], logits, 0.0), axis=-1, keepdims=True)
        m = jnp.maximum(la, lb)
        ea = jnp.exp(la - m)
        eb = jnp.exp(lb - m)
        den = ea + eb

        def expert(wg_ref, wu_ref, wd_ref):
            gate = _mm(nb16, wg_ref[...])
            up = _mm(nb16, wu_ref[...])
            act = (gate * _sigmoid(gate)) * up
            return _mm(act, wd_ref[...])

        y = (ea / den) * expert(wga_ref, wua_ref, wda_ref)
        y = y + (eb / den) * expert(wgb_ref, wub_ref, wdb_ref)
        res = x + y
        if final_norm:
            res = _rms(res, gf_ref[...])
        obuf[slot] = res
        start_scatter(slot)

    @pl.when(i == n_tiles)
    def _():
        wait_gather(nv % 2)
        wait_scatter((nv - 1) % 2)

        @pl.when(nv >= 2)
        def _():
            wait_scatter(nv % 2)

        obuf[0] = jnp.zeros((tm, D_MODEL), F32)
        n_rows = out_hbm.shape[0] - N_PAIRS * tm
        fills = [pltpu.make_async_copy(obuf.at[0], out_hbm.at[pl.ds(n_rows + c * tm, tm)], ssem.at[0])
                 for c in range(N_PAIRS)]
        for f in fills:
            f.start()
        for f in fills:
            f.wait()


def _pair_ffn(h, src, dst, out_rows, ta, tb, nvalid, g, router, wg, wu, wd, g_final=None):
    tm = PAIR_TM
    n_tiles = src.shape[0] // tm
    final_norm = g_final is not None
    const2 = lambda i, ta, tb, nv: (0, 0)
    prev = lambda i: jnp.maximum(i - 1, 0)
    wa = lambda i, ta, tb, nv: (ta[prev(i)], 0, 0)
    wb = lambda i, ta, tb, nv: (tb[prev(i)], 0, 0)
    in_specs = [
        pl.BlockSpec((1, 1, tm), lambda i, ta, tb, nv: (jnp.minimum(i, n_tiles - 1), 0, 0),
                     memory_space=pltpu.SMEM),
        pl.BlockSpec((1, 1, tm), lambda i, ta, tb, nv: (prev(i), 0, 0), memory_space=pltpu.SMEM),
        pl.BlockSpec(memory_space=pl.ANY),
        pl.BlockSpec((1, D_MODEL), const2),
        pl.BlockSpec((D_MODEL, LANES), const2),
        pl.BlockSpec((None, D_MODEL, D_FF_EXPERT), wa),
        pl.BlockSpec((None, D_MODEL, D_FF_EXPERT), wa),
        pl.BlockSpec((None, D_FF_EXPERT, D_MODEL), wa),
        pl.BlockSpec((None, D_MODEL, D_FF_EXPERT), wb),
        pl.BlockSpec((None, D_MODEL, D_FF_EXPERT), wb),
        pl.BlockSpec((None, D_FF_EXPERT, D_MODEL), wb),
    ]
    args = [src.reshape(n_tiles, 1, tm), dst.reshape(n_tiles, 1, tm), h, g, router,
            wg, wu, wd, wg, wu, wd]
    if final_norm:
        in_specs.append(pl.BlockSpec((1, D_MODEL), const2))
        args.append(g_final)
    return pl.pallas_call(
        functools.partial(_pair_ffn_kernel, final_norm=final_norm, tm=tm, n_tiles=n_tiles),
        grid_spec=pltpu.PrefetchScalarGridSpec(
            num_scalar_prefetch=3,
            grid=(n_tiles + 1,),
            in_specs=in_specs,
            out_specs=pl.BlockSpec(memory_space=pl.ANY),
            scratch_shapes=[
                pltpu.VMEM((2, tm, D_MODEL), F32),
                pltpu.VMEM((2, tm, D_MODEL), F32),
                pltpu.SemaphoreType.DMA((2,)),
                pltpu.SemaphoreType.DMA((2,)),
            ],
        ),
        out_shape=jax.ShapeDtypeStruct((out_rows, D_MODEL), F32),
        compiler_params=pltpu.CompilerParams(
            dimension_semantics=("arbitrary",), vmem_limit_bytes=VMEM_LIMIT),
        name="pair_ffn",
    )(ta, tb, nvalid, *args)


def _sparse_moe(h, rows, g, router, wg, wu, wd, g_final=None):
    tm = PAIR_TM
    n_tiles = rows // tm + N_PAIRS
    n_sorted = n_tiles * tm
    grp, rank, cnt = _route(h, rows, g, router, 512)
    counts = cnt[:, 0]
    padded = (counts + tm - 1) // tm * tm
    ends = jnp.cumsum(padded)
    dest = (ends - padded)[grp[0]] + rank[0]
    nvalid = (ends[-1] // tm).reshape(1).astype(jnp.int32)
    tile_start = jnp.arange(n_tiles, dtype=jnp.int32) * tm
    tile_grp = jnp.sum((ends[None, :] <= tile_start[:, None]).astype(jnp.int32), axis=1)
    last_grp = jnp.max(jnp.where(counts > 0, jnp.arange(N_GROUPS), 0))
    tile_grp = jnp.minimum(tile_grp, last_grp).astype(jnp.int32)
    token = jnp.full((n_sorted,), -1, jnp.int32).at[dest].set(jnp.arange(rows, dtype=jnp.int32))
    is_pad = token < 0
    src = jnp.where(is_pad, 0, token)
    dst = jnp.where(is_pad, rows - 1 + jnp.cumsum(is_pad.astype(jnp.int32)), token)
    return _pair_ffn(h, src, dst, rows + N_PAIRS * tm, tile_grp // N_EXPERTS, tile_grp % N_EXPERTS,
                     nvalid, g, router, wg, wu, wd, g_final=g_final)


def kernel(x_prompt, x_sample, state_conv, state_pool, cache_swa_k, cache_swa_v, norm_mix, norm_ffn, norm_final, even_w_in, conv_w, conv_b, conv_ln_g, conv_ln_b, pool_w, pool_scale, even_w_out, ffn_w_gate, ffn_w_up, ffn_w_down, odd_w_in, sgu_ln_g, sgu_ln_b, sgu_w, sgu_b, attn_sinks, odd_w_out, router_w, exp_w_gate, exp_w_up, exp_w_down):
    bp, tp, _ = x_prompt.shape
    bs, ts, _ = x_sample.shape
    depth = norm_mix.shape[0]
    tm_p = 512
    kvw = N_KV_HEADS * HEAD_DIM

    h_p = x_prompt.reshape(bp * tp, D_MODEL)
    h_s = x_sample.reshape(bs * ts, D_MODEL)
    b16 = lambda w: w.astype(BF16)
    r2 = lambda v: v.reshape(1, -1)
    pad_router = lambda r: jnp.pad(r, ((0, 0), (0, LANES - N_EXPERTS)))

    conv_p, conv_s, pool_p, pool_s = [], [], [], []
    k_p, v_p, k_s, v_s, sgu_s = [], [], [], [], []
    for layer in range(depth):
        i = layer // 2
        last = layer == depth - 1
        gf = r2(norm_final) if last else None
        if layer % 2 == 0:
            small = (conv_w[i], r2(conv_b[i]), r2(conv_ln_g[i]), r2(conv_ln_b[i]))
            h_p, cst, pst = _even_mixer(
                h_p, bp, tp, 0, r2(norm_mix[layer]), b16(even_w_in[i]), *small, b16(pool_w[i]),
                r2(pool_scale[i]), b16(even_w_out[i]),
                jnp.zeros((bp, CONV_HIST, D_CONV), F32), jnp.zeros((bp, POOL_HIST, D_POOL), F32),
                tm_p, nseq=1, hp=False)
            conv_p.append(cst)
            pool_p.append(pst)
            h_s, cst, pst = _even_mixer(
                h_s, bs, ts, PAST_LEN, r2(norm_mix[layer]), even_w_in[i], *small, pool_w[i],
                r2(pool_scale[i]), even_w_out[i], state_conv[i], state_pool[i],
                ts, nseq=bs, hp=True)
            conv_s.append(cst)
            pool_s.append(pst)
            gn = r2(norm_ffn[layer])
            h_p = _ffn(h_p, gn, b16(ffn_w_gate[i]), b16(ffn_w_up[i]), b16(ffn_w_down[i]), tm_p,
                       g_final=gf)
            h_s = _ffn(h_s, gn, ffn_w_gate[i], ffn_w_up[i], ffn_w_down[i], bs * ts, g_final=gf,
                       hp_experts=True)
        else:
            bias = jnp.repeat(sgu_b[i].T, SGU_HEAD, axis=1)
            small = (r2(sgu_ln_g[i]), r2(sgu_ln_b[i]), sgu_w[i], bias)
            zkv = jnp.zeros((bp, WINDOW, kvw), F32)
            h_p, kn, vn = _odd_mixer(
                h_p, bp, tp, attn_sinks[i], r2(norm_mix[layer]), b16(odd_w_in[i]), *small,
                b16(odd_w_out[i]), zkv, zkv, tm_p, CHUNK, True, False, nseq=1, hp=False)
            k_p.append(kn.reshape(bp, -1, N_KV_HEADS, HEAD_DIM))
            v_p.append(vn.reshape(bp, -1, N_KV_HEADS, HEAD_DIM))
            h_s, kn, vn, sv = _odd_mixer(
                h_s, bs, ts, attn_sinks[i], r2(norm_mix[layer]), odd_w_in[i], *small, odd_w_out[i],
                cache_swa_k[i].reshape(bs, WINDOW, kvw), cache_swa_v[i].reshape(bs, WINDOW, kvw),
                ts, ts, False, True, nseq=bs, hp=True)
            k_s.append(kn.reshape(bs, ts, N_KV_HEADS, HEAD_DIM))
            v_s.append(vn.reshape(bs, ts, N_KV_HEADS, HEAD_DIM))
            sgu_s.append(sv.reshape(bs, ts, D_SGU))
            gn = r2(norm_ffn[layer])
            wg, wu, wd = b16(exp_w_gate[i]), b16(exp_w_up[i]), b16(exp_w_down[i])
            h_p = _sparse_moe(h_p, bp * tp, gn, pad_router(b16(router_w[i])), wg, wu, wd,
                              g_final=gf)
            if last:
                h_s = _ffn(h_s, gn, wg, wu, wd, bs * ts, router=pad_router(router_w[i]),
                           g_final=gf, hp_router=True)
            else:
                h_s = _ffn(h_s, gn, exp_w_gate[i], exp_w_up[i], exp_w_down[i], bs * ts,
                           router=pad_router(router_w[i]), g_final=gf, hp_router=True,
                           hp_experts=True)

    return (h_p[:bp * tp].reshape(bp, tp, D_MODEL), h_s.reshape(bs, ts, D_MODEL),
            jnp.stack(conv_p), jnp.stack(conv_s),
            jnp.stack(pool_p), jnp.stack(pool_s),
            jnp.stack(k_p), jnp.stack(v_p),
            jnp.stack(k_s), jnp.stack(v_s),
            jnp.stack(sgu_s))
```

```python
import functools

import jax
import jax.numpy as jnp
from jax import lax
from jax.experimental import pallas as pl
from jax.experimental.pallas import tpu as pltpu

F32 = jnp.float32
BF16 = jnp.bfloat16

D_MODEL = 1024
PAST_LEN = 1024
CHUNK = 64
D_CONV = 512
CONV_WIDTH = 31
CONV_HIST = CONV_WIDTH - 1
D_POOL = 512
POOL_WINDOWS = (2, 4, 8, 16)
POOL_GROUP = 128
POOL_HIST = 15
D_SGU = 512
SGU_CHUNK = 128
N_SGU_HEADS = 4
SGU_HEAD = 128
HEAD_DIM = 64
N_Q_HEADS = 8
N_KV_HEADS = 2
Q_PER_KV = 4
WINDOW = 128
D_FF = 2816
N_EXPERTS = 8
D_FF_EXPERT = 1408
EPS = 1e-6

LANES = 128
SUBLANES = 8
A_PAD = 32
P_PAD = 16
VMEM_LIMIT = 56 * 1024 * 1024


def _sigmoid(x):
    return 1.0 / (1.0 + jnp.exp(-x))


def _rms(x, g):
    ms = jnp.mean(x * x, axis=-1, keepdims=True)
    return x * lax.rsqrt(ms + EPS) * g


def _layer_norm(x, g, b):
    mu = jnp.mean(x, axis=-1, keepdims=True)
    xc = x - mu
    var = jnp.mean(xc * xc, axis=-1, keepdims=True)
    return xc * lax.rsqrt(var + EPS) * g + b


def _mm(x, w, hp=False):
    if hp:
        return jnp.dot(x, w, preferred_element_type=F32, precision=lax.Precision.HIGHEST)
    return jnp.dot(x.astype(BF16), w.astype(BF16), preferred_element_type=F32)


def _mm_t(x, y, hp=False):
    dims = (((1,), (1,)), ((), ()))
    if hp:
        return lax.dot_general(x, y, dims, preferred_element_type=F32,
                               precision=lax.Precision.HIGHEST)
    return lax.dot_general(x.astype(BF16), y.astype(BF16), dims, preferred_element_type=F32)


def _even_mixer_kernel(h_ref, g_ref, win_ref, cw_ref, cb_ref, lng_ref, lnb_ref, pw_ref, ps_ref,
                       wout_ref, chist_ref, phist_ref,
                       out_ref, cstate_ref, pstate_ref,
                       abuf, pbuf, ash, cat, *, nseq, tm, rc, start_pos, hp):
    t = pl.program_id(1)

    @pl.when(t == 0)
    def _():
        for s in range(nseq):
            abuf[s, 0:A_PAD - CONV_HIST, :] = jnp.zeros((A_PAD - CONV_HIST, D_CONV), F32)
            abuf[s, A_PAD - CONV_HIST:A_PAD, :] = chist_ref[s]
            pbuf[s, 0:P_PAD - POOL_HIST, :] = jnp.zeros((P_PAD - POOL_HIST, D_POOL), F32)
            pbuf[s, P_PAD - POOL_HIST:P_PAD, :] = phist_ref[s]

    h = h_ref[...]
    n = _rms(h, g_ref[...])
    proj = _mm(n, win_ref[...], hp)
    a = proj[:, :D_CONV] * _sigmoid(proj[:, D_CONV:2 * D_CONV])
    for s in range(nseq):
        abuf[s, A_PAD:A_PAD + tm, :] = a[s * tm:(s + 1) * tm]
        pbuf[s, P_PAD:P_PAD + tm, :] = proj[s * tm:(s + 1) * tm, 2 * D_CONV:]

    off = A_PAD - CONV_HIST
    n_sh = A_PAD + tm - SUBLANES
    for s in range(nseq):
        for j in range(1, SUBLANES):
            for c0 in range(0, n_sh, rc):
                nr = min(rc, n_sh - c0)
                ash[j - 1, c0:c0 + nr, :] = abuf[s, c0 + j:c0 + j + nr, :]

        def tap(k, r0, ls):
            q, j = divmod(off + k, SUBLANES)
            lo = r0 + q * SUBLANES
            if j == 0:
                return abuf[s, lo:lo + rc, ls]
            return ash[j - 1, lo:lo + rc, ls]

        for r0 in range(0, tm, rc):
            o0 = s * tm + r0
            cparts = []
            for lb in range(D_CONV // LANES):
                ls = slice(lb * LANES, (lb + 1) * LANES)
                acc = tap(0, r0, ls) * cw_ref[0:1, ls]
                for k in range(1, CONV_WIDTH):
                    acc = acc + tap(k, r0, ls) * cw_ref[k:k + 1, ls]
                cparts.append(acc)
            c = jnp.concatenate(cparts, axis=-1) + cb_ref[...]
            c = _layer_norm(c, lng_ref[...], lnb_ref[...])
            c = c * _sigmoid(c)
            cat[o0:o0 + rc, 0:D_CONV] = c.astype(cat.dtype)

            pos1 = (start_pos + 1 + t * tm + r0
                    + lax.broadcasted_iota(jnp.int32, (rc, 1), 0)).astype(F32)
            for gi, w in enumerate(POOL_WINDOWS):
                ls = slice(gi * POOL_GROUP, (gi + 1) * POOL_GROUP)
                cur = pbuf[s, P_PAD + r0:P_PAD + r0 + rc, ls]
                acc = cur
                for i in range(1, w):
                    acc = acc + pbuf[s, P_PAD + r0 - i:P_PAD + r0 - i + rc, ls]
                pooled = acc / jnp.minimum(float(w), pos1) - cur
                mixed = _mm(pooled, pw_ref[gi], hp)
                cat[o0:o0 + rc, D_CONV + gi * POOL_GROUP:D_CONV + (gi + 1) * POOL_GROUP] = (
                    mixed * ps_ref[0:1, ls]).astype(cat.dtype)

    out_ref[...] = h + _mm(cat[...], wout_ref[...], hp)

    for s in range(nseq):
        atail = abuf[s, tm:tm + A_PAD, :]
        ptail = pbuf[s, tm:tm + P_PAD, :]
        abuf[s, 0:A_PAD, :] = atail
        pbuf[s, 0:P_PAD, :] = ptail
        cstate_ref[s] = atail[A_PAD - CONV_HIST:, :]
        pstate_ref[s] = ptail[P_PAD - POOL_HIST:, :]


def _even_mixer(h, nb, tlen, start_pos, g, w_in, cw, cb, lng, lnb, pw, ps, w_out, chist, phist, tm,
                nseq, hp):
    rc = min(64, tm)
    nt = tlen // tm
    rows = nseq * tm
    row = lambda b, t: (b * nt + t, 0)
    const2 = lambda b, t: (0, 0)
    const3 = lambda b, t: (0, 0, 0)
    per_b = lambda b, t: (b, 0, 0)
    kern = functools.partial(_even_mixer_kernel, nseq=nseq, tm=tm, rc=rc, start_pos=start_pos, hp=hp)
    return pl.pallas_call(
        kern,
        grid=(nb // nseq, nt),
        in_specs=[
            pl.BlockSpec((rows, D_MODEL), row),
            pl.BlockSpec((1, D_MODEL), const2),
            pl.BlockSpec((D_MODEL, 3 * D_CONV), const2),
            pl.BlockSpec((CONV_WIDTH, D_CONV), const2),
            pl.BlockSpec((1, D_CONV), const2),
            pl.BlockSpec((1, D_CONV), const2),
            pl.BlockSpec((1, D_CONV), const2),
            pl.BlockSpec((len(POOL_WINDOWS), POOL_GROUP, POOL_GROUP), const3),
            pl.BlockSpec((1, D_POOL), const2),
            pl.BlockSpec((D_MODEL, D_MODEL), const2),
            pl.BlockSpec((nseq, CONV_HIST, D_CONV), per_b),
            pl.BlockSpec((nseq, POOL_HIST, D_POOL), per_b),
        ],
        out_specs=[
            pl.BlockSpec((rows, D_MODEL), row),
            pl.BlockSpec((nseq, CONV_HIST, D_CONV), per_b),
            pl.BlockSpec((nseq, POOL_HIST, D_POOL), per_b),
        ],
        out_shape=[
            jax.ShapeDtypeStruct((nb * tlen, D_MODEL), F32),
            jax.ShapeDtypeStruct((nb, CONV_HIST, D_CONV), F32),
            jax.ShapeDtypeStruct((nb, POOL_HIST, D_POOL), F32),
        ],
        scratch_shapes=[
            pltpu.VMEM((nseq, A_PAD + tm, D_CONV), F32),
            pltpu.VMEM((nseq, P_PAD + tm, D_POOL), F32),
            pltpu.VMEM((SUBLANES - 1, A_PAD + tm - SUBLANES, D_CONV), F32),
            pltpu.VMEM((rows, D_MODEL), F32 if hp else BF16),
        ],
        compiler_params=pltpu.CompilerParams(
            dimension_semantics=("arbitrary", "arbitrary"), vmem_limit_bytes=VMEM_LIMIT),
        name="even_mixer",
    )(h, g, w_in, cw, cb, lng, lnb, pw, ps, w_out, chist, phist)


def _odd_mixer_kernel(sink_ref, h_ref, g_ref, win_ref, lng_ref, lnb_ref, ws_ref, bs_ref, wout_ref,
                      kc_ref, vc_ref, *rest, nseq, tm, cq, kv_rows, mask_first, emit_sgu_v, hp):
    if emit_sgu_v:
        out_ref, knew_ref, vnew_ref, sguv_ref, kbuf, vbuf, vs_scr, attn, cat = rest
    else:
        out_ref, knew_ref, vnew_ref, kbuf, vbuf, vs_scr, attn, cat = rest
    t = pl.program_id(1)
    op_dtype = kbuf.dtype

    @pl.when(t == 0)
    def _():
        for s in range(nseq):
            for gi in range(N_KV_HEADS):
                hs = slice(gi * HEAD_DIM, (gi + 1) * HEAD_DIM)
                kbuf[s * N_KV_HEADS + gi, 0:WINDOW, :] = kc_ref[s, :, hs].astype(op_dtype)
                vbuf[s * N_KV_HEADS + gi, 0:WINDOW, :] = vc_ref[s, :, hs].astype(op_dtype)

    h = h_ref[...]
    n = _rms(h, g_ref[...])
    proj = _mm(n, win_ref[...], hp)
    o0 = 2 * D_SGU
    o1 = o0 + N_Q_HEADS * HEAD_DIM
    o2 = o1 + N_KV_HEADS * HEAD_DIM
    zp = proj[:, :o0]
    z = 0.5 * zp * (1.0 + lax.erf(zp * (0.5 ** 0.5)))
    u = z[:, :D_SGU]
    v = _layer_norm(z[:, D_SGU:], lng_ref[...], lnb_ref[...])
    if emit_sgu_v:
        sguv_ref[...] = v

    n_sgu = -(-tm // SGU_CHUNK)
    if tm % SGU_CHUNK:
        vs_scr[...] = jnp.zeros(vs_scr.shape, vs_scr.dtype)
    for s in range(nseq):
        v0 = s * n_sgu * SGU_CHUNK
        vs_scr[v0:v0 + tm, :] = v[s * tm:(s + 1) * tm].astype(vs_scr.dtype)
    ri = lax.broadcasted_iota(jnp.int32, (SGU_CHUNK, SGU_CHUNK), 0)
    ci = lax.broadcasted_iota(jnp.int32, (SGU_CHUNK, SGU_CHUNK), 1)
    for gi in range(N_SGU_HEADS):
        ls = slice(gi * SGU_HEAD, (gi + 1) * SGU_HEAD)
        wsg = jnp.where(ri >= ci, ws_ref[gi], 0.0).astype(vs_scr.dtype)
        for s in range(nseq):
            for c in range(n_sgu):
                rows = min(SGU_CHUNK, tm - c * SGU_CHUNK)
                v0 = (s * n_sgu + c) * SGU_CHUNK
                mixed = _mm(wsg, vs_scr[v0:v0 + SGU_CHUNK, ls], hp) + bs_ref[:, ls]
                r0 = s * tm + c * SGU_CHUNK
                cat[r0:r0 + rows, ls] = (u[r0:r0 + rows, ls] * mixed[0:rows]).astype(cat.dtype)

    k = proj[:, o1:o2]
    vv = proj[:, o2:]
    q = proj[:, o0:o1].astype(op_dtype)
    nk = WINDOW + cq
    qrow = lax.broadcasted_iota(jnp.int32, (Q_PER_KV * cq, 1), 0)
    for s in range(nseq):
        knew_ref[s] = k[(s + 1) * tm - kv_rows:(s + 1) * tm, :]
        vnew_ref[s] = vv[(s + 1) * tm - kv_rows:(s + 1) * tm, :]
        for gi in range(N_KV_HEADS):
            hs = slice(gi * HEAD_DIM, (gi + 1) * HEAD_DIM)
            kbuf[s * N_KV_HEADS + gi, WINDOW:WINDOW + tm, :] = k[s * tm:(s + 1) * tm, hs].astype(op_dtype)
            vbuf[s * N_KV_HEADS + gi, WINDOW:WINDOW + tm, :] = vv[s * tm:(s + 1) * tm, hs].astype(op_dtype)
    for gi in range(N_KV_HEADS):
        sk = jnp.zeros((Q_PER_KV * cq, 1), F32)
        for i in range(Q_PER_KV):
            sk = jnp.where((qrow >= i * cq) & (qrow < (i + 1) * cq), sink_ref[gi * Q_PER_KV + i], sk)
        for s in range(nseq):
            kv = s * N_KV_HEADS + gi
            for c in range(tm // cq):
                r0 = c * cq
                g0 = s * tm + r0
                qs = jnp.concatenate(
                    [q[g0:g0 + cq, (gi * Q_PER_KV + i) * HEAD_DIM:(gi * Q_PER_KV + i + 1) * HEAD_DIM]
                     for i in range(Q_PER_KV)], axis=0)
                sc = _mm_t(qs, kbuf[kv, r0:r0 + nk, :], hp) * (HEAD_DIM ** -0.5)
                if mask_first and r0 < WINDOW:
                    key_pos = t * tm + (r0 - WINDOW) + lax.broadcasted_iota(jnp.int32, (1, nk), 1)
                    sc = jnp.where(key_pos >= 0, sc, -jnp.inf)
                m = jnp.maximum(jnp.max(sc, axis=-1, keepdims=True), sk)
                e = jnp.exp(sc - m)
                p = e / (jnp.sum(e, axis=-1, keepdims=True) + jnp.exp(sk - m))
                o = _mm(p, vbuf[kv, r0:r0 + nk, :], hp)
                for i in range(Q_PER_KV):
                    hd = gi * Q_PER_KV + i
                    attn[g0:g0 + cq, hd * HEAD_DIM:(hd + 1) * HEAD_DIM] = o[i * cq:(i + 1) * cq, :]
    cat[:, D_SGU:] = attn[...].astype(cat.dtype)

    out_ref[...] = h + _mm(cat[...], wout_ref[...], hp)

    for kv in range(nseq * N_KV_HEADS):
        ktail = kbuf[kv, tm:tm + WINDOW, :]
        vtail = vbuf[kv, tm:tm + WINDOW, :]
        kbuf[kv, 0:WINDOW, :] = ktail
        vbuf[kv, 0:WINDOW, :] = vtail


def _odd_mixer(h, nb, tlen, sinks, g, w_in, lng, lnb, ws, bs, w_out, kc, vc, tm, cq, mask_first,
               emit_sgu_v, nseq, hp):
    nt = tlen // tm
    rows = nseq * tm
    kv_rows = min(WINDOW, tlen)
    odd_in = w_in.shape[1]
    row = lambda b, t: (b * nt + t, 0)
    const2 = lambda b, t: (0, 0)
    const3 = lambda b, t: (0, 0, 0)
    per_b = lambda b, t: (b, 0, 0)
    kern = functools.partial(_odd_mixer_kernel, nseq=nseq, tm=tm, cq=cq, kv_rows=kv_rows,
                             mask_first=mask_first, emit_sgu_v=emit_sgu_v, hp=hp)
    kvw = N_KV_HEADS * HEAD_DIM
    out_specs = [
        pl.BlockSpec((rows, D_MODEL), row),
        pl.BlockSpec((nseq, kv_rows, kvw), per_b),
        pl.BlockSpec((nseq, kv_rows, kvw), per_b),
    ]
    out_shape = [
        jax.ShapeDtypeStruct((nb * tlen, D_MODEL), F32),
        jax.ShapeDtypeStruct((nb, kv_rows, kvw), F32),
        jax.ShapeDtypeStruct((nb, kv_rows, kvw), F32),
    ]
    if emit_sgu_v:
        out_specs.append(pl.BlockSpec((rows, D_SGU), row))
        out_shape.append(jax.ShapeDtypeStruct((nb * tlen, D_SGU), F32))
    n_sgu = -(-tm // SGU_CHUNK)
    op_dtype = F32 if hp else BF16
    return pl.pallas_call(
        kern,
        grid=(nb // nseq, nt),
        in_specs=[
            pl.BlockSpec(memory_space=pltpu.SMEM),
            pl.BlockSpec((rows, D_MODEL), row),
            pl.BlockSpec((1, D_MODEL), const2),
            pl.BlockSpec((D_MODEL, odd_in), const2),
            pl.BlockSpec((1, D_SGU), const2),
            pl.BlockSpec((1, D_SGU), const2),
            pl.BlockSpec((N_SGU_HEADS, SGU_CHUNK, SGU_CHUNK), const3),
            pl.BlockSpec((SGU_CHUNK, D_SGU), const2),
            pl.BlockSpec((D_MODEL, D_MODEL), const2),
            pl.BlockSpec((nseq, WINDOW, kvw), per_b),
            pl.BlockSpec((nseq, WINDOW, kvw), per_b),
        ],
        out_specs=out_specs,
        out_shape=out_shape,
        scratch_shapes=[
            pltpu.VMEM((nseq * N_KV_HEADS, WINDOW + tm, HEAD_DIM), op_dtype),
            pltpu.VMEM((nseq * N_KV_HEADS, WINDOW + tm, HEAD_DIM), op_dtype),
            pltpu.VMEM((nseq * n_sgu * SGU_CHUNK, D_SGU), op_dtype),
            pltpu.VMEM((rows, N_Q_HEADS * HEAD_DIM), F32),
            pltpu.VMEM((rows, D_MODEL), op_dtype),
        ],
        compiler_params=pltpu.CompilerParams(
            dimension_semantics=("arbitrary", "arbitrary"), vmem_limit_bytes=VMEM_LIMIT),
        name="odd_mixer",
    )(sinks, h, g, w_in, lng, lnb, ws, bs, w_out, kc, vc)


def _ffn_kernel(*refs, routed, final_norm, n_chunks, hp_router, hp_experts):
    refs = list(refs)
    h_ref, g_ref = refs[0], refs[1]
    pos = 2
    if routed:
        router_ref = refs[pos]
        pos += 1
    wg_ref, wu_ref, wd_ref = refs[pos:pos + 3]
    pos += 3
    if final_norm:
        gf_ref = refs[pos]
        pos += 1
    out_ref = refs[pos]
    n_scr, acc = refs[pos + 1], refs[pos + 2]
    if routed:
        comb = refs[pos + 3]
    j = pl.program_id(1)

    @pl.when(j == 0)
    def _():
        n = _rms(h_ref[...], g_ref[...])
        n_scr[...] = n.astype(n_scr.dtype)
        if routed:
            logits = _mm(n, router_ref[...], hp_router)
            lane = lax.broadcasted_iota(jnp.int32, logits.shape, 1)
            logits = jnp.where(lane < N_EXPERTS, logits, -jnp.inf)
            m1 = jnp.max(logits, axis=-1, keepdims=True)
            i1 = jnp.min(jnp.where(logits == m1, lane, LANES), axis=-1, keepdims=True)
            rest = jnp.where(lane == i1, -jnp.inf, logits)
            m2 = jnp.max(rest, axis=-1, keepdims=True)
            i2 = jnp.min(jnp.where(rest == m2, lane, LANES), axis=-1, keepdims=True)
            e2 = jnp.exp(m2 - m1)
            den = 1.0 + e2
            comb[...] = jnp.where(lane == i1, 1.0 / den, 0.0) + jnp.where(lane == i2, e2 / den, 0.0)

    x = n_scr[...]
    gate = _mm(x, wg_ref[...], hp_experts)
    up = _mm(x, wu_ref[...], hp_experts)
    act = (gate * _sigmoid(gate)) * up
    y = _mm(act, wd_ref[...], hp_experts)
    if routed:
        cmb = comb[...]
        lane = lax.broadcasted_iota(jnp.int32, cmb.shape, 1)
        y = y * jnp.sum(jnp.where(lane == j, cmb, 0.0), axis=-1, keepdims=True)

    @pl.when(j == 0)
    def _():
        acc[...] = y

    @pl.when(j > 0)
    def _():
        acc[...] += y

    @pl.when(j == n_chunks - 1)
    def _():
        res = h_ref[...] + acc[...]
        if final_norm:
            res = _rms(res, gf_ref[...])
        out_ref[...] = res


def _ffn(h, g, wg, wu, wd, tm, router=None, g_final=None, hp_router=False, hp_experts=False):
    rows = h.shape[0]
    routed = router is not None
    final_norm = g_final is not None
    row = lambda i, j: (i, 0)
    const2 = lambda i, j: (0, 0)
    in_specs = [pl.BlockSpec((tm, D_MODEL), row), pl.BlockSpec((1, D_MODEL), const2)]
    args = [h, g]
    if routed:
        n_chunks = N_EXPERTS
        in_specs.append(pl.BlockSpec((D_MODEL, LANES), const2))
        args.append(router)
        in_specs += [
            pl.BlockSpec((None, D_MODEL, D_FF_EXPERT), lambda i, j: (j, 0, 0)),
            pl.BlockSpec((None, D_MODEL, D_FF_EXPERT), lambda i, j: (j, 0, 0)),
            pl.BlockSpec((None, D_FF_EXPERT, D_MODEL), lambda i, j: (j, 0, 0)),
        ]
    else:
        n_chunks = D_FF // D_FF_EXPERT
        in_specs += [
            pl.BlockSpec((D_MODEL, D_FF_EXPERT), lambda i, j: (0, j)),
            pl.BlockSpec((D_MODEL, D_FF_EXPERT), lambda i, j: (0, j)),
            pl.BlockSpec((D_FF_EXPERT, D_MODEL), lambda i, j: (j, 0)),
        ]
    args += [wg, wu, wd]
    if final_norm:
        in_specs.append(pl.BlockSpec((1, D_MODEL), const2))
        args.append(g_final)
    scratch = [pltpu.VMEM((tm, D_MODEL), F32 if hp_experts else BF16), pltpu.VMEM((tm, D_MODEL), F32)]
    if routed:
        scratch.append(pltpu.VMEM((tm, LANES), F32))
    kern = functools.partial(_ffn_kernel, routed=routed, final_norm=final_norm, n_chunks=n_chunks,
                             hp_router=hp_router, hp_experts=hp_experts)
    return pl.pallas_call(
        kern,
        grid=(rows // tm, n_chunks),
        in_specs=in_specs,
        out_specs=pl.BlockSpec((tm, D_MODEL), row),
        out_shape=jax.ShapeDtypeStruct((rows, D_MODEL), F32),
        scratch_shapes=scratch,
        compiler_params=pltpu.CompilerParams(
            dimension_semantics=("arbitrary", "arbitrary"), vmem_limit_bytes=VMEM_LIMIT),
        name="moe_ffn" if routed else "dense_ffn",
    )(*args)


N_GROUPS = N_EXPERTS * N_EXPERTS
PAIR_IDS = tuple(a * N_EXPERTS + b for a in range(N_EXPERTS) for b in range(a + 1, N_EXPERTS))
N_PAIRS = len(PAIR_IDS)
PAIR_TM = 256
ROW_TM = 512
TILE_LANES = 256


def _route_kernel(h_ref, g_ref, router_ref, grp_ref, rank_ref, cnt_ref, start_ref, tile_ref,
                  nv_ref, carry, *, tm):
    i = pl.program_id(0)

    @pl.when(i == 0)
    def _():
        carry[...] = jnp.zeros(carry.shape, F32)

    n = _rms(h_ref[...], g_ref[...])
    logits = _mm(n, router_ref[...])
    l8 = logits.T[0:N_EXPERTS, :]
    sub = lax.broadcasted_iota(jnp.int32, l8.shape, 0)
    m1 = jnp.max(l8, axis=0, keepdims=True)
    i1 = jnp.min(jnp.where(l8 == m1, sub, N_EXPERTS), axis=0, keepdims=True)
    rest = jnp.where(sub == i1, -jnp.inf, l8)
    m2 = jnp.max(rest, axis=0, keepdims=True)
    i2 = jnp.min(jnp.where(rest == m2, sub, N_EXPERTS), axis=0, keepdims=True)
    grp = jnp.minimum(i1, i2) * N_EXPERTS + jnp.maximum(i1, i2)
    grp_ref[...] = grp

    gid = lax.broadcasted_iota(jnp.int32, (N_GROUPS, tm), 0)
    onehot = jnp.where(gid == grp, 1.0, 0.0)
    rs = lax.broadcasted_iota(jnp.int32, (tm, tm), 0)
    cs = lax.broadcasted_iota(jnp.int32, (tm, tm), 1)
    upper = jnp.where(rs <= cs, 1.0, 0.0)
    incl = _mm(onehot, upper)
    before = carry[...]
    rank = jnp.sum(onehot * (incl - 1.0 + before), axis=0, keepdims=True)
    rank_ref[...] = rank.astype(jnp.int32)
    total = before + jnp.sum(onehot, axis=1, keepdims=True)
    carry[...] = total

    @pl.when(i == pl.num_programs(0) - 1)
    def _():
        padded = jnp.floor((total + (PAIR_TM - 1.0)) / PAIR_TM) * PAIR_TM
        gr = lax.broadcasted_iota(jnp.int32, (N_GROUPS, N_GROUPS), 0)
        gc = lax.broadcasted_iota(jnp.int32, (N_GROUPS, N_GROUPS), 1)
        lower = jnp.where(gc <= gr, 1.0, 0.0)
        ends = _mm(lower, jnp.broadcast_to(padded, (N_GROUPS, LANES)), hp=True)
        cnt_ref[...] = jnp.broadcast_to(total, cnt_ref.shape).astype(jnp.int32)
        start_ref[...] = (ends - padded).astype(jnp.int32)
        nv_ref[...] = (ends[N_GROUPS - 1:, :] / PAIR_TM).astype(jnp.int32)
        tile_start = lax.broadcasted_iota(jnp.int32, (N_GROUPS, TILE_LANES), 1).astype(F32) * PAIR_TM
        tile_grp = jnp.sum(jnp.where(ends[:, 0:1] <= tile_start, 1.0, 0.0), axis=0, keepdims=True)
        gidf = lax.broadcasted_iota(jnp.int32, (N_GROUPS, 1), 0).astype(F32)
        last = jnp.max(jnp.where(total > 0.0, gidf, 0.0), axis=0, keepdims=True)
        tile_ref[...] = jnp.minimum(tile_grp, last).astype(jnp.int32)


def _route(h, g, router, tm):
    rows = h.shape[0]
    const2 = lambda i: (0, 0)
    return pl.pallas_call(
        functools.partial(_route_kernel, tm=tm),
        grid=(rows // tm,),
        in_specs=[
            pl.BlockSpec((tm, D_MODEL), lambda i: (i, 0)),
            pl.BlockSpec((1, D_MODEL), const2),
            pl.BlockSpec((D_MODEL, LANES), const2),
        ],
        out_specs=[
            pl.BlockSpec((1, tm), lambda i: (0, i)),
            pl.BlockSpec((1, tm), lambda i: (0, i)),
            pl.BlockSpec((N_GROUPS, LANES), const2),
            pl.BlockSpec((N_GROUPS, LANES), const2),
            pl.BlockSpec((1, TILE_LANES), const2),
            pl.BlockSpec((1, LANES), const2),
        ],
        out_shape=[
            jax.ShapeDtypeStruct((1, rows), jnp.int32),
            jax.ShapeDtypeStruct((1, rows), jnp.int32),
            jax.ShapeDtypeStruct((N_GROUPS, LANES), jnp.int32),
            jax.ShapeDtypeStruct((N_GROUPS, LANES), jnp.int32),
            jax.ShapeDtypeStruct((1, TILE_LANES), jnp.int32),
            jax.ShapeDtypeStruct((1, LANES), jnp.int32),
        ],
        scratch_shapes=[pltpu.VMEM((N_GROUPS, 1), F32)],
        compiler_params=pltpu.CompilerParams(
            dimension_semantics=("arbitrary",), vmem_limit_bytes=VMEM_LIMIT),
        name="route",
    )(h, g, router)


def _dispatch_kernel(start_ref, cnt_ref, nv_ref, grp_ref, rank_ref, h_ref, xs_hbm,
                     stage, zbuf, sem, fsem, *, tm, n_tiles):
    i = pl.program_id(0)
    last = pl.num_programs(0) - 1
    slot = i % 3

    def wait_rows(s):
        pltpu.make_async_copy(stage.at[s], xs_hbm.at[pl.ds(0, tm)], sem.at[s]).wait()

    def fill_ops(op):
        def tail(j, c):
            op(pltpu.make_async_copy(zbuf, xs_hbm.at[pl.ds(j * PAIR_TM, PAIR_TM)], fsem))
            return c
        lax.fori_loop(nv_ref[0], n_tiles, tail, 0)
        for gid in PAIR_IDS:
            lo = start_ref[gid] + cnt_ref[gid]
            hi = start_ref[gid] + (cnt_ref[gid] + PAIR_TM - 1) // PAIR_TM * PAIR_TM

            def row(r, c):
                op(pltpu.make_async_copy(zbuf.at[pl.ds(0, 1)], xs_hbm.at[pl.ds(r, 1)], fsem))
                return c
            lax.fori_loop(lo, hi, row, 0)

    @pl.when(i == 0)
    def _():
        zbuf[...] = jnp.zeros(zbuf.shape, F32)
        fill_ops(lambda c: c.start())

    @pl.when(i >= 3)
    def _():
        wait_rows(slot)

    stage[slot] = h_ref[...]
    for r in range(tm):
        d = start_ref[grp_ref[0, 0, r]] + rank_ref[0, 0, r]
        pltpu.make_async_copy(stage.at[slot, pl.ds(r, 1)], xs_hbm.at[pl.ds(d, 1)], sem.at[slot]).start()

    @pl.when(i == last)
    def _():
        for s in range(3):
            wait_rows(s)
        fill_ops(lambda c: c.wait())


def _dispatch(h, start, cnt, nv, grp, rank, n_tiles):
    rows = h.shape[0]
    tm = ROW_TM
    n_steps = rows // tm
    assert n_steps >= 3
    blk = lambda i, s, c, n: (i, 0, 0)
    return pl.pallas_call(
        functools.partial(_dispatch_kernel, tm=tm, n_tiles=n_tiles),
        grid_spec=pltpu.PrefetchScalarGridSpec(
            num_scalar_prefetch=3,
            grid=(n_steps,),
            in_specs=[
                pl.BlockSpec((1, 1, tm), blk, memory_space=pltpu.SMEM),
                pl.BlockSpec((1, 1, tm), blk, memory_space=pltpu.SMEM),
                pl.BlockSpec((tm, D_MODEL), lambda i, s, c, n: (i, 0)),
            ],
            out_specs=pl.BlockSpec(memory_space=pl.ANY),
            scratch_shapes=[
                pltpu.VMEM((3, tm, D_MODEL), F32),
                pltpu.VMEM((PAIR_TM, D_MODEL), F32),
                pltpu.SemaphoreType.DMA((3,)),
                pltpu.SemaphoreType.DMA,
            ],
        ),
        out_shape=jax.ShapeDtypeStruct((n_tiles * PAIR_TM, D_MODEL), F32),
        compiler_params=pltpu.CompilerParams(
            dimension_semantics=("arbitrary",), vmem_limit_bytes=VMEM_LIMIT),
        name="dispatch_rows",
    )(start, cnt, nv, grp.reshape(n_steps, 1, tm), rank.reshape(n_steps, 1, tm), h)


def _collect_kernel(start_ref, grp_ref, rank_ref, ys_hbm, out_ref, stage, sem, *, tm):
    i = pl.program_id(0)
    n_steps = pl.num_programs(0) - 1

    @pl.when(i < n_steps)
    def _():
        slot = i % 2
        for r in range(tm):
            d = start_ref[grp_ref[0, 0, r]] + rank_ref[0, 0, r]
            pltpu.make_async_copy(ys_hbm.at[pl.ds(d, 1)], stage.at[slot, pl.ds(r, 1)],
                                  sem.at[slot]).start()

    @pl.when(i >= 1)
    def _():
        slot = (i - 1) % 2
        pltpu.make_async_copy(ys_hbm.at[pl.ds(0, tm)], stage.at[slot], sem.at[slot]).wait()
        out_ref[...] = stage[slot]


def _collect(ys, start, grp, rank):
    rows = grp.shape[-1]
    tm = ROW_TM
    n_steps = rows // tm
    blk = lambda i, s: (jnp.minimum(i, n_steps - 1), 0, 0)
    return pl.pallas_call(
        functools.partial(_collect_kernel, tm=tm),
        grid_spec=pltpu.PrefetchScalarGridSpec(
            num_scalar_prefetch=1,
            grid=(n_steps + 1,),
            in_specs=[
                pl.BlockSpec((1, 1, tm), blk, memory_space=pltpu.SMEM),
                pl.BlockSpec((1, 1, tm), blk, memory_space=pltpu.SMEM),
                pl.BlockSpec(memory_space=pl.ANY),
            ],
            out_specs=pl.BlockSpec((tm, D_MODEL), lambda i, s: (jnp.maximum(i - 1, 0), 0)),
            scratch_shapes=[
                pltpu.VMEM((2, tm, D_MODEL), F32),
                pltpu.SemaphoreType.DMA((2,)),
            ],
        ),
        out_shape=jax.ShapeDtypeStruct((rows, D_MODEL), F32),
        compiler_params=pltpu.CompilerParams(
            dimension_semantics=("arbitrary",), vmem_limit_bytes=VMEM_LIMIT),
        name="collect_rows",
    )(start, grp.reshape(n_steps, 1, tm), rank.reshape(n_steps, 1, tm), ys)


def _pair_ffn_kernel(tg_ref, nv_ref, x_ref, g_ref, router_ref,
                     wga_ref, wua_ref, wda_ref, wgb_ref, wub_ref, wdb_ref, *rest, final_norm):
    if final_norm:
        gf_ref, out_ref = rest
    else:
        (out_ref,) = rest
    i = pl.program_id(0)

    @pl.when(i < nv_ref[0])
    def _():
        x = x_ref[...]
        nb16 = _rms(x, g_ref[...]).astype(BF16)
        logits = _mm(nb16, router_ref[...])
        lane = lax.broadcasted_iota(jnp.int32, logits.shape, 1)
        ea_id = tg_ref[i] // N_EXPERTS
        eb_id = tg_ref[i] % N_EXPERTS
        la = jnp.sum(jnp.where(lane == ea_id, logits, 0.0), axis=-1, keepdims=True)
        lb = jnp.sum(jnp.where(lane == eb_id, logits, 0.0), axis=-1, keepdims=True)
        m = jnp.maximum(la, lb)
        ea = jnp.exp(la - m)
        eb = jnp.exp(lb - m)
        den = ea + eb

        def expert(wg_ref, wu_ref, wd_ref):
            gate = _mm(nb16, wg_ref[...])
            up = _mm(nb16, wu_ref[...])
            act = (gate * _sigmoid(gate)) * up
            return _mm(act, wd_ref[...])

        y = (ea / den) * expert(wga_ref, wua_ref, wda_ref)
        y = y + (eb / den) * expert(wgb_ref, wub_ref, wdb_ref)
        res = x + y
        if final_norm:
            res = _rms(res, gf_ref[...])
        out_ref[...] = res

    @pl.when(i >= nv_ref[0])
    def _():
        out_ref[...] = jnp.zeros(out_ref.shape, F32)


def _pair_ffn(xs, tile_grp, nvalid, g, router, wg, wu, wd, g_final=None):
    tm = PAIR_TM
    n_tiles = xs.shape[0] // tm
    final_norm = g_final is not None
    const2 = lambda i, tg, nv: (0, 0)
    wa = lambda i, tg, nv: (tg[i] // N_EXPERTS, 0, 0)
    wb = lambda i, tg, nv: (tg[i] % N_EXPERTS, 0, 0)
    in_specs = [
        pl.BlockSpec((tm, D_MODEL), lambda i, tg, nv: (jnp.minimum(i, nv[0] - 1), 0)),
        pl.BlockSpec((1, D_MODEL), const2),
        pl.BlockSpec((D_MODEL, LANES), const2),
        pl.BlockSpec((None, D_MODEL, D_FF_EXPERT), wa),
        pl.BlockSpec((None, D_MODEL, D_FF_EXPERT), wa),
        pl.BlockSpec((None, D_FF_EXPERT, D_MODEL), wa),
        pl.BlockSpec((None, D_MODEL, D_FF_EXPERT), wb),
        pl.BlockSpec((None, D_MODEL, D_FF_EXPERT), wb),
        pl.BlockSpec((None, D_FF_EXPERT, D_MODEL), wb),
    ]
    args = [xs, g, router, wg, wu, wd, wg, wu, wd]
    if final_norm:
        in_specs.append(pl.BlockSpec((1, D_MODEL), const2))
        args.append(g_final)
    return pl.pallas_call(
        functools.partial(_pair_ffn_kernel, final_norm=final_norm),
        grid_spec=pltpu.PrefetchScalarGridSpec(
            num_scalar_prefetch=2,
            grid=(n_tiles,),
            in_specs=in_specs,
            out_specs=pl.BlockSpec((tm, D_MODEL), lambda i, tg, nv: (i, 0)),
        ),
        out_shape=jax.ShapeDtypeStruct((n_tiles * tm, D_MODEL), F32),
        compiler_params=pltpu.CompilerParams(
            dimension_semantics=("arbitrary",), vmem_limit_bytes=VMEM_LIMIT),
        name="pair_ffn",
    )(tile_grp, nvalid, *args)


def _sparse_moe(h, g, router, wg, wu, wd, g_final=None):
    rows = h.shape[0]
    n_tiles = rows // PAIR_TM + N_PAIRS
    assert n_tiles <= TILE_LANES
    grp, rank, cnt, start, tile_grp, nv = _route(h, g, router, ROW_TM)
    start, cnt, nv = start[:, 0], cnt[:, 0], nv[0, :1]
    xs = _dispatch(h, start, cnt, nv, grp, rank, n_tiles)
    ys = _pair_ffn(xs, tile_grp[0, :n_tiles], nv, g, router, wg, wu, wd, g_final=g_final)
    return _collect(ys, start, grp, rank)


def kernel(x_prompt, x_sample, state_conv, state_pool, cache_swa_k, cache_swa_v, norm_mix, norm_ffn, norm_final, even_w_in, conv_w, conv_b, conv_ln_g, conv_ln_b, pool_w, pool_scale, even_w_out, ffn_w_gate, ffn_w_up, ffn_w_down, odd_w_in, sgu_ln_g, sgu_ln_b, sgu_w, sgu_b, attn_sinks, odd_w_out, router_w, exp_w_gate, exp_w_up, exp_w_down):
    bp, tp, _ = x_prompt.shape
    bs, ts, _ = x_sample.shape
    depth = norm_mix.shape[0]
    tm_p = 512
    kvw = N_KV_HEADS * HEAD_DIM

    h_p = x_prompt.reshape(bp * tp, D_MODEL)
    h_s = x_sample.reshape(bs * ts, D_MODEL)
    b16 = lambda w: w.astype(BF16)
    r2 = lambda v: v.reshape(1, -1)
    pad_router = lambda r: jnp.pad(r, ((0, 0), (0, LANES - N_EXPERTS)))

    conv_p, conv_s, pool_p, pool_s = [], [], [], []
    k_p, v_p, k_s, v_s, sgu_s = [], [], [], [], []
    for layer in range(depth):
        i = layer // 2
        last = layer == depth - 1
        gf = r2(norm_final) if last else None
        if layer % 2 == 0:
            small = (conv_w[i], r2(conv_b[i]), r2(conv_ln_g[i]), r2(conv_ln_b[i]))
            h_p, cst, pst = _even_mixer(
                h_p, bp, tp, 0, r2(norm_mix[layer]), b16(even_w_in[i]), *small, b16(pool_w[i]),
                r2(pool_scale[i]), b16(even_w_out[i]),
                jnp.zeros((bp, CONV_HIST, D_CONV), F32), jnp.zeros((bp, POOL_HIST, D_POOL), F32),
                tm_p, nseq=1, hp=False)
            conv_p.append(cst)
            pool_p.append(pst)
            h_s, cst, pst = _even_mixer(
                h_s, bs, ts, PAST_LEN, r2(norm_mix[layer]), even_w_in[i], *small, pool_w[i],
                r2(pool_scale[i]), even_w_out[i], state_conv[i], state_pool[i],
                ts, nseq=bs, hp=True)
            conv_s.append(cst)
            pool_s.append(pst)
            gn = r2(norm_ffn[layer])
            h_p = _ffn(h_p, gn, b16(ffn_w_gate[i]), b16(ffn_w_up[i]), b16(ffn_w_down[i]), tm_p,
                       g_final=gf)
            h_s = _ffn(h_s, gn, ffn_w_gate[i], ffn_w_up[i], ffn_w_down[i], bs * ts, g_final=gf,
                       hp_experts=True)
        else:
            bias = jnp.repeat(sgu_b[i].T, SGU_HEAD, axis=1)
            small = (r2(sgu_ln_g[i]), r2(sgu_ln_b[i]), sgu_w[i], bias)
            zkv = jnp.zeros((bp, WINDOW, kvw), F32)
            h_p, kn, vn = _odd_mixer(
                h_p, bp, tp, attn_sinks[i], r2(norm_mix[layer]), b16(odd_w_in[i]), *small,
                b16(odd_w_out[i]), zkv, zkv, tm_p, CHUNK, True, False, nseq=1, hp=False)
            k_p.append(kn.reshape(bp, -1, N_KV_HEADS, HEAD_DIM))
            v_p.append(vn.reshape(bp, -1, N_KV_HEADS, HEAD_DIM))
            h_s, kn, vn, sv = _odd_mixer(
                h_s, bs, ts, attn_sinks[i], r2(norm_mix[layer]), odd_w_in[i], *small, odd_w_out[i],
                cache_swa_k[i].reshape(bs, WINDOW, kvw), cache_swa_v[i].reshape(bs, WINDOW, kvw),
                ts, ts, False, True, nseq=bs, hp=True)
            k_s.append(kn.reshape(bs, ts, N_KV_HEADS, HEAD_DIM))
            v_s.append(vn.reshape(bs, ts, N_KV_HEADS, HEAD_DIM))
            sgu_s.append(sv.reshape(bs, ts, D_SGU))
            gn = r2(norm_ffn[layer])
            wg, wu, wd = b16(exp_w_gate[i]), b16(exp_w_up[i]), b16(exp_w_down[i])
            h_p = _sparse_moe(h_p, gn, pad_router(b16(router_w[i])), wg, wu, wd, g_final=gf)
            if last:
                h_s = _ffn(h_s, gn, wg, wu, wd, bs * ts, router=pad_router(router_w[i]),
                           g_final=gf, hp_router=True)
            else:
                h_s = _ffn(h_s, gn, exp_w_gate[i], exp_w_up[i], exp_w_down[i], bs * ts,
                           router=pad_router(router_w[i]), g_final=gf, hp_router=True,
                           hp_experts=True)

    return (h_p.reshape(bp, tp, D_MODEL), h_s.reshape(bs, ts, D_MODEL),
            jnp.stack(conv_p), jnp.stack(conv_s),
            jnp.stack(pool_p), jnp.stack(pool_s),
            jnp.stack(k_p), jnp.stack(v_p),
            jnp.stack(k_s), jnp.stack(v_s),
            jnp.stack(sgu_s))
```

```python
import functools

import jax
import jax.numpy as jnp
from jax import lax
from jax.experimental import pallas as pl
from jax.experimental.pallas import tpu as pltpu

F32 = jnp.float32
BF16 = jnp.bfloat16

D_MODEL = 1024
PAST_LEN = 1024
CHUNK = 64
D_CONV = 512
CONV_WIDTH = 31
CONV_HIST = CONV_WIDTH - 1
D_POOL = 512
POOL_WINDOWS = (2, 4, 8, 16)
POOL_GROUP = 128
POOL_HIST = 15
D_SGU = 512
SGU_CHUNK = 128
N_SGU_HEADS = 4
SGU_HEAD = 128
HEAD_DIM = 64
N_Q_HEADS = 8
N_KV_HEADS = 2
Q_PER_KV = 4
WINDOW = 128
D_FF = 2816
N_EXPERTS = 8
D_FF_EXPERT = 1408
EPS = 1e-6

LANES = 128
SUBLANES = 8
A_PAD = 32
P_PAD = 16
VMEM_LIMIT = 56 * 1024 * 1024


def _sigmoid(x):
    return 1.0 / (1.0 + jnp.exp(-x))


def _rms(x, g):
    ms = jnp.mean(x * x, axis=-1, keepdims=True)
    return x * lax.rsqrt(ms + EPS) * g


def _layer_norm(x, g, b):
    mu = jnp.mean(x, axis=-1, keepdims=True)
    xc = x - mu
    var = jnp.mean(xc * xc, axis=-1, keepdims=True)
    return xc * lax.rsqrt(var + EPS) * g + b


def _mm(x, w, hp=False):
    if hp:
        return jnp.dot(x, w, preferred_element_type=F32, precision=lax.Precision.HIGHEST)
    return jnp.dot(x.astype(BF16), w.astype(BF16), preferred_element_type=F32)


def _mm_t(x, y, hp=False):
    dims = (((1,), (1,)), ((), ()))
    if hp:
        return lax.dot_general(x, y, dims, preferred_element_type=F32,
                               precision=lax.Precision.HIGHEST)
    return lax.dot_general(x.astype(BF16), y.astype(BF16), dims, preferred_element_type=F32)


def _even_mixer_kernel(h_ref, g_ref, win_ref, cw_ref, cb_ref, lng_ref, lnb_ref, pw_ref, ps_ref,
                       wout_ref, chist_ref, phist_ref,
                       out_ref, cstate_ref, pstate_ref,
                       abuf, pbuf, ash, cat, *, nseq, tm, rc, start_pos, hp):
    t = pl.program_id(1)

    @pl.when(t == 0)
    def _():
        for s in range(nseq):
            abuf[s, 0:A_PAD - CONV_HIST, :] = jnp.zeros((A_PAD - CONV_HIST, D_CONV), F32)
            abuf[s, A_PAD - CONV_HIST:A_PAD, :] = chist_ref[s]
            pbuf[s, 0:P_PAD - POOL_HIST, :] = jnp.zeros((P_PAD - POOL_HIST, D_POOL), F32)
            pbuf[s, P_PAD - POOL_HIST:P_PAD, :] = phist_ref[s]

    h = h_ref[...]
    n = _rms(h, g_ref[...])
    proj = _mm(n, win_ref[...], hp)
    a = proj[:, :D_CONV] * _sigmoid(proj[:, D_CONV:2 * D_CONV])
    for s in range(nseq):
        abuf[s, A_PAD:A_PAD + tm, :] = a[s * tm:(s + 1) * tm]
        pbuf[s, P_PAD:P_PAD + tm, :] = proj[s * tm:(s + 1) * tm, 2 * D_CONV:]

    off = A_PAD - CONV_HIST
    n_sh = A_PAD + tm - SUBLANES
    for s in range(nseq):
        for j in range(1, SUBLANES):
            for c0 in range(0, n_sh, rc):
                nr = min(rc, n_sh - c0)
                ash[j - 1, c0:c0 + nr, :] = abuf[s, c0 + j:c0 + j + nr, :]

        def tap(k, r0, ls):
            q, j = divmod(off + k, SUBLANES)
            lo = r0 + q * SUBLANES
            if j == 0:
                return abuf[s, lo:lo + rc, ls]
            return ash[j - 1, lo:lo + rc, ls]

        for r0 in range(0, tm, rc):
            o0 = s * tm + r0
            cparts = []
            for lb in range(D_CONV // LANES):
                ls = slice(lb * LANES, (lb + 1) * LANES)
                acc = tap(0, r0, ls) * cw_ref[0:1, ls]
                for k in range(1, CONV_WIDTH):
                    acc = acc + tap(k, r0, ls) * cw_ref[k:k + 1, ls]
                cparts.append(acc)
            c = jnp.concatenate(cparts, axis=-1) + cb_ref[...]
            c = _layer_norm(c, lng_ref[...], lnb_ref[...])
            c = c * _sigmoid(c)
            cat[o0:o0 + rc, 0:D_CONV] = c.astype(cat.dtype)

            pos1 = (start_pos + 1 + t * tm + r0
                    + lax.broadcasted_iota(jnp.int32, (rc, 1), 0)).astype(F32)
            for gi, w in enumerate(POOL_WINDOWS):
                ls = slice(gi * POOL_GROUP, (gi + 1) * POOL_GROUP)
                cur = pbuf[s, P_PAD + r0:P_PAD + r0 + rc, ls]
                acc = cur
                for i in range(1, w):
                    acc = acc + pbuf[s, P_PAD + r0 - i:P_PAD + r0 - i + rc, ls]
                pooled = acc / jnp.minimum(float(w), pos1) - cur
                mixed = _mm(pooled, pw_ref[gi], hp)
                cat[o0:o0 + rc, D_CONV + gi * POOL_GROUP:D_CONV + (gi + 1) * POOL_GROUP] = (
                    mixed * ps_ref[0:1, ls]).astype(cat.dtype)

    out_ref[...] = h + _mm(cat[...], wout_ref[...], hp)

    for s in range(nseq):
        atail = abuf[s, tm:tm + A_PAD, :]
        ptail = pbuf[s, tm:tm + P_PAD, :]
        abuf[s, 0:A_PAD, :] = atail
        pbuf[s, 0:P_PAD, :] = ptail
        cstate_ref[s] = atail[A_PAD - CONV_HIST:, :]
        pstate_ref[s] = ptail[P_PAD - POOL_HIST:, :]


def _even_mixer(h, nb, tlen, start_pos, g, w_in, cw, cb, lng, lnb, pw, ps, w_out, chist, phist, tm,
                nseq, hp):
    rc = min(64, tm)
    nt = tlen // tm
    rows = nseq * tm
    row = lambda b, t: (b * nt + t, 0)
    const2 = lambda b, t: (0, 0)
    const3 = lambda b, t: (0, 0, 0)
    per_b = lambda b, t: (b, 0, 0)
    kern = functools.partial(_even_mixer_kernel, nseq=nseq, tm=tm, rc=rc, start_pos=start_pos, hp=hp)
    return pl.pallas_call(
        kern,
        grid=(nb // nseq, nt),
        in_specs=[
            pl.BlockSpec((rows, D_MODEL), row),
            pl.BlockSpec((1, D_MODEL), const2),
            pl.BlockSpec((D_MODEL, 3 * D_CONV), const2),
            pl.BlockSpec((CONV_WIDTH, D_CONV), const2),
            pl.BlockSpec((1, D_CONV), const2),
            pl.BlockSpec((1, D_CONV), const2),
            pl.BlockSpec((1, D_CONV), const2),
            pl.BlockSpec((len(POOL_WINDOWS), POOL_GROUP, POOL_GROUP), const3),
            pl.BlockSpec((1, D_POOL), const2),
            pl.BlockSpec((D_MODEL, D_MODEL), const2),
            pl.BlockSpec((nseq, CONV_HIST, D_CONV), per_b),
            pl.BlockSpec((nseq, POOL_HIST, D_POOL), per_b),
        ],
        out_specs=[
            pl.BlockSpec((rows, D_MODEL), row),
            pl.BlockSpec((nseq, CONV_HIST, D_CONV), per_b),
            pl.BlockSpec((nseq, POOL_HIST, D_POOL), per_b),
        ],
        out_shape=[
            jax.ShapeDtypeStruct((nb * tlen, D_MODEL), F32),
            jax.ShapeDtypeStruct((nb, CONV_HIST, D_CONV), F32),
            jax.ShapeDtypeStruct((nb, POOL_HIST, D_POOL), F32),
        ],
        scratch_shapes=[
            pltpu.VMEM((nseq, A_PAD + tm, D_CONV), F32),
            pltpu.VMEM((nseq, P_PAD + tm, D_POOL), F32),
            pltpu.VMEM((SUBLANES - 1, A_PAD + tm - SUBLANES, D_CONV), F32),
            pltpu.VMEM((rows, D_MODEL), F32 if hp else BF16),
        ],
        compiler_params=pltpu.CompilerParams(
            dimension_semantics=("arbitrary", "arbitrary"), vmem_limit_bytes=VMEM_LIMIT),
        name="even_mixer",
    )(h, g, w_in, cw, cb, lng, lnb, pw, ps, w_out, chist, phist)


def _odd_mixer_kernel(sink_ref, h_ref, g_ref, win_ref, lng_ref, lnb_ref, ws_ref, bs_ref, wout_ref,
                      kc_ref, vc_ref, *rest, nseq, tm, cq, kv_rows, mask_first, emit_sgu_v, hp):
    if emit_sgu_v:
        out_ref, knew_ref, vnew_ref, sguv_ref, kbuf, vbuf, vs_scr, attn, cat = rest
    else:
        out_ref, knew_ref, vnew_ref, kbuf, vbuf, vs_scr, attn, cat = rest
    t = pl.program_id(1)
    op_dtype = kbuf.dtype

    @pl.when(t == 0)
    def _():
        for s in range(nseq):
            for gi in range(N_KV_HEADS):
                hs = slice(gi * HEAD_DIM, (gi + 1) * HEAD_DIM)
                kbuf[s * N_KV_HEADS + gi, 0:WINDOW, :] = kc_ref[s, :, hs].astype(op_dtype)
                vbuf[s * N_KV_HEADS + gi, 0:WINDOW, :] = vc_ref[s, :, hs].astype(op_dtype)

    h = h_ref[...]
    n = _rms(h, g_ref[...])
    proj = _mm(n, win_ref[...], hp)
    o0 = 2 * D_SGU
    o1 = o0 + N_Q_HEADS * HEAD_DIM
    o2 = o1 + N_KV_HEADS * HEAD_DIM
    zp = proj[:, :o0]
    z = 0.5 * zp * (1.0 + lax.erf(zp * (0.5 ** 0.5)))
    u = z[:, :D_SGU]
    v = _layer_norm(z[:, D_SGU:], lng_ref[...], lnb_ref[...])
    if emit_sgu_v:
        sguv_ref[...] = v

    n_sgu = -(-tm // SGU_CHUNK)
    if tm % SGU_CHUNK:
        vs_scr[...] = jnp.zeros(vs_scr.shape, vs_scr.dtype)
    for s in range(nseq):
        v0 = s * n_sgu * SGU_CHUNK
        vs_scr[v0:v0 + tm, :] = v[s * tm:(s + 1) * tm].astype(vs_scr.dtype)
    ri = lax.broadcasted_iota(jnp.int32, (SGU_CHUNK, SGU_CHUNK), 0)
    ci = lax.broadcasted_iota(jnp.int32, (SGU_CHUNK, SGU_CHUNK), 1)
    for gi in range(N_SGU_HEADS):
        ls = slice(gi * SGU_HEAD, (gi + 1) * SGU_HEAD)
        wsg = jnp.where(ri >= ci, ws_ref[gi], 0.0).astype(vs_scr.dtype)
        for s in range(nseq):
            for c in range(n_sgu):
                rows = min(SGU_CHUNK, tm - c * SGU_CHUNK)
                v0 = (s * n_sgu + c) * SGU_CHUNK
                mixed = _mm(wsg, vs_scr[v0:v0 + SGU_CHUNK, ls], hp) + bs_ref[:, ls]
                r0 = s * tm + c * SGU_CHUNK
                cat[r0:r0 + rows, ls] = (u[r0:r0 + rows, ls] * mixed[0:rows]).astype(cat.dtype)

    k = proj[:, o1:o2]
    vv = proj[:, o2:]
    q = proj[:, o0:o1].astype(op_dtype)
    nk = WINDOW + cq
    qrow = lax.broadcasted_iota(jnp.int32, (Q_PER_KV * cq, 1), 0)
    for s in range(nseq):
        knew_ref[s] = k[(s + 1) * tm - kv_rows:(s + 1) * tm, :]
        vnew_ref[s] = vv[(s + 1) * tm - kv_rows:(s + 1) * tm, :]
        for gi in range(N_KV_HEADS):
            hs = slice(gi * HEAD_DIM, (gi + 1) * HEAD_DIM)
            kbuf[s * N_KV_HEADS + gi, WINDOW:WINDOW + tm, :] = k[s * tm:(s + 1) * tm, hs].astype(op_dtype)
            vbuf[s * N_KV_HEADS + gi, WINDOW:WINDOW + tm, :] = vv[s * tm:(s + 1) * tm, hs].astype(op_dtype)
    for gi in range(N_KV_HEADS):
        sk = jnp.zeros((Q_PER_KV * cq, 1), F32)
        for i in range(Q_PER_KV):
            sk = jnp.where((qrow >= i * cq) & (qrow < (i + 1) * cq), sink_ref[gi * Q_PER_KV + i], sk)
        for s in range(nseq):
            kv = s * N_KV_HEADS + gi
            for c in range(tm // cq):
                r0 = c * cq
                g0 = s * tm + r0
                qs = jnp.concatenate(
                    [q[g0:g0 + cq, (gi * Q_PER_KV + i) * HEAD_DIM:(gi * Q_PER_KV + i + 1) * HEAD_DIM]
                     for i in range(Q_PER_KV)], axis=0)
                sc = _mm_t(qs, kbuf[kv, r0:r0 + nk, :], hp) * (HEAD_DIM ** -0.5)
                if mask_first and r0 < WINDOW:
                    key_pos = t * tm + (r0 - WINDOW) + lax.broadcasted_iota(jnp.int32, (1, nk), 1)
                    sc = jnp.where(key_pos >= 0, sc, -jnp.inf)
                m = jnp.maximum(jnp.max(sc, axis=-1, keepdims=True), sk)
                e = jnp.exp(sc - m)
                p = e / (jnp.sum(e, axis=-1, keepdims=True) + jnp.exp(sk - m))
                o = _mm(p, vbuf[kv, r0:r0 + nk, :], hp)
                for i in range(Q_PER_KV):
                    hd = gi * Q_PER_KV + i
                    attn[g0:g0 + cq, hd * HEAD_DIM:(hd + 1) * HEAD_DIM] = o[i * cq:(i + 1) * cq, :]
    cat[:, D_SGU:] = attn[...].astype(cat.dtype)

    out_ref[...] = h + _mm(cat[...], wout_ref[...], hp)

    for kv in range(nseq * N_KV_HEADS):
        ktail = kbuf[kv, tm:tm + WINDOW, :]
        vtail = vbuf[kv, tm:tm + WINDOW, :]
        kbuf[kv, 0:WINDOW, :] = ktail
        vbuf[kv, 0:WINDOW, :] = vtail


def _odd_mixer(h, nb, tlen, sinks, g, w_in, lng, lnb, ws, bs, w_out, kc, vc, tm, cq, mask_first,
               emit_sgu_v, nseq, hp):
    nt = tlen // tm
    rows = nseq * tm
    kv_rows = min(WINDOW, tlen)
    odd_in = w_in.shape[1]
    row = lambda b, t: (b * nt + t, 0)
    const2 = lambda b, t: (0, 0)
    const3 = lambda b, t: (0, 0, 0)
    per_b = lambda b, t: (b, 0, 0)
    kern = functools.partial(_odd_mixer_kernel, nseq=nseq, tm=tm, cq=cq, kv_rows=kv_rows,
                             mask_first=mask_first, emit_sgu_v=emit_sgu_v, hp=hp)
    kvw = N_KV_HEADS * HEAD_DIM
    out_specs = [
        pl.BlockSpec((rows, D_MODEL), row),
        pl.BlockSpec((nseq, kv_rows, kvw), per_b),
        pl.BlockSpec((nseq, kv_rows, kvw), per_b),
    ]
    out_shape = [
        jax.ShapeDtypeStruct((nb * tlen, D_MODEL), F32),
        jax.ShapeDtypeStruct((nb, kv_rows, kvw), F32),
        jax.ShapeDtypeStruct((nb, kv_rows, kvw), F32),
    ]
    if emit_sgu_v:
        out_specs.append(pl.BlockSpec((rows, D_SGU), row))
        out_shape.append(jax.ShapeDtypeStruct((nb * tlen, D_SGU), F32))
    n_sgu = -(-tm // SGU_CHUNK)
    op_dtype = F32 if hp else BF16
    return pl.pallas_call(
        kern,
        grid=(nb // nseq, nt),
        in_specs=[
            pl.BlockSpec(memory_space=pltpu.SMEM),
            pl.BlockSpec((rows, D_MODEL), row),
            pl.BlockSpec((1, D_MODEL), const2),
            pl.BlockSpec((D_MODEL, odd_in), const2),
            pl.BlockSpec((1, D_SGU), const2),
            pl.BlockSpec((1, D_SGU), const2),
            pl.BlockSpec((N_SGU_HEADS, SGU_CHUNK, SGU_CHUNK), const3),
            pl.BlockSpec((SGU_CHUNK, D_SGU), const2),
            pl.BlockSpec((D_MODEL, D_MODEL), const2),
            pl.BlockSpec((nseq, WINDOW, kvw), per_b),
            pl.BlockSpec((nseq, WINDOW, kvw), per_b),
        ],
        out_specs=out_specs,
        out_shape=out_shape,
        scratch_shapes=[
            pltpu.VMEM((nseq * N_KV_HEADS, WINDOW + tm, HEAD_DIM), op_dtype),
            pltpu.VMEM((nseq * N_KV_HEADS, WINDOW + tm, HEAD_DIM), op_dtype),
            pltpu.VMEM((nseq * n_sgu * SGU_CHUNK, D_SGU), op_dtype),
            pltpu.VMEM((rows, N_Q_HEADS * HEAD_DIM), F32),
            pltpu.VMEM((rows, D_MODEL), op_dtype),
        ],
        compiler_params=pltpu.CompilerParams(
            dimension_semantics=("arbitrary", "arbitrary"), vmem_limit_bytes=VMEM_LIMIT),
        name="odd_mixer",
    )(sinks, h, g, w_in, lng, lnb, ws, bs, w_out, kc, vc)


def _ffn_kernel(*refs, routed, final_norm, n_chunks, hp_router, hp_experts):
    refs = list(refs)
    h_ref, g_ref = refs[0], refs[1]
    pos = 2
    if routed:
        router_ref = refs[pos]
        pos += 1
    wg_ref, wu_ref, wd_ref = refs[pos:pos + 3]
    pos += 3
    if final_norm:
        gf_ref = refs[pos]
        pos += 1
    out_ref = refs[pos]
    n_scr, acc = refs[pos + 1], refs[pos + 2]
    if routed:
        comb = refs[pos + 3]
    j = pl.program_id(1)

    @pl.when(j == 0)
    def _():
        n = _rms(h_ref[...], g_ref[...])
        n_scr[...] = n.astype(n_scr.dtype)
        if routed:
            logits = _mm(n, router_ref[...], hp_router)
            lane = lax.broadcasted_iota(jnp.int32, logits.shape, 1)
            logits = jnp.where(lane < N_EXPERTS, logits, -jnp.inf)
            m1 = jnp.max(logits, axis=-1, keepdims=True)
            i1 = jnp.min(jnp.where(logits == m1, lane, LANES), axis=-1, keepdims=True)
            rest = jnp.where(lane == i1, -jnp.inf, logits)
            m2 = jnp.max(rest, axis=-1, keepdims=True)
            i2 = jnp.min(jnp.where(rest == m2, lane, LANES), axis=-1, keepdims=True)
            e2 = jnp.exp(m2 - m1)
            den = 1.0 + e2
            comb[...] = jnp.where(lane == i1, 1.0 / den, 0.0) + jnp.where(lane == i2, e2 / den, 0.0)

    x = n_scr[...]
    gate = _mm(x, wg_ref[...], hp_experts)
    up = _mm(x, wu_ref[...], hp_experts)
    act = (gate * _sigmoid(gate)) * up
    y = _mm(act, wd_ref[...], hp_experts)
    if routed:
        cmb = comb[...]
        lane = lax.broadcasted_iota(jnp.int32, cmb.shape, 1)
        y = y * jnp.sum(jnp.where(lane == j, cmb, 0.0), axis=-1, keepdims=True)

    @pl.when(j == 0)
    def _():
        acc[...] = y

    @pl.when(j > 0)
    def _():
        acc[...] += y

    @pl.when(j == n_chunks - 1)
    def _():
        res = h_ref[...] + acc[...]
        if final_norm:
            res = _rms(res, gf_ref[...])
        out_ref[...] = res


def _ffn(h, g, wg, wu, wd, tm, router=None, g_final=None, hp_router=False, hp_experts=False):
    rows = h.shape[0]
    routed = router is not None
    final_norm = g_final is not None
    row = lambda i, j: (i, 0)
    const2 = lambda i, j: (0, 0)
    in_specs = [pl.BlockSpec((tm, D_MODEL), row), pl.BlockSpec((1, D_MODEL), const2)]
    args = [h, g]
    if routed:
        n_chunks = N_EXPERTS
        in_specs.append(pl.BlockSpec((D_MODEL, LANES), const2))
        args.append(router)
        in_specs += [
            pl.BlockSpec((None, D_MODEL, D_FF_EXPERT), lambda i, j: (j, 0, 0)),
            pl.BlockSpec((None, D_MODEL, D_FF_EXPERT), lambda i, j: (j, 0, 0)),
            pl.BlockSpec((None, D_FF_EXPERT, D_MODEL), lambda i, j: (j, 0, 0)),
        ]
    else:
        n_chunks = D_FF // D_FF_EXPERT
        in_specs += [
            pl.BlockSpec((D_MODEL, D_FF_EXPERT), lambda i, j: (0, j)),
            pl.BlockSpec((D_MODEL, D_FF_EXPERT), lambda i, j: (0, j)),
            pl.BlockSpec((D_FF_EXPERT, D_MODEL), lambda i, j: (j, 0)),
        ]
    args += [wg, wu, wd]
    if final_norm:
        in_specs.append(pl.BlockSpec((1, D_MODEL), const2))
        args.append(g_final)
    scratch = [pltpu.VMEM((tm, D_MODEL), F32 if hp_experts else BF16), pltpu.VMEM((tm, D_MODEL), F32)]
    if routed:
        scratch.append(pltpu.VMEM((tm, LANES), F32))
    kern = functools.partial(_ffn_kernel, routed=routed, final_norm=final_norm, n_chunks=n_chunks,
                             hp_router=hp_router, hp_experts=hp_experts)
    return pl.pallas_call(
        kern,
        grid=(rows // tm, n_chunks),
        in_specs=in_specs,
        out_specs=pl.BlockSpec((tm, D_MODEL), row),
        out_shape=jax.ShapeDtypeStruct((rows, D_MODEL), F32),
        scratch_shapes=scratch,
        compiler_params=pltpu.CompilerParams(
            dimension_semantics=("arbitrary", "arbitrary"), vmem_limit_bytes=VMEM_LIMIT),
        name="moe_ffn" if routed else "dense_ffn",
    )(*args)


N_GROUPS = N_EXPERTS * N_EXPERTS
PAIR_IDS = tuple(a * N_EXPERTS + b for a in range(N_EXPERTS) for b in range(a + 1, N_EXPERTS))
N_PAIRS = len(PAIR_IDS)
PAIR_TM = 256
ROW_TM = 512
TILE_LANES = 256


def _route_kernel(h_ref, g_ref, router_ref, dest_ref, cnt_ref, start_ref, tile_ref,
                  nv_ref, carry, grp_s, rank_s, *, tm):
    i = pl.program_id(0)
    n_steps = pl.num_programs(0)

    @pl.when(i == 0)
    def _():
        carry[...] = jnp.zeros(carry.shape, F32)

    n = _rms(h_ref[...], g_ref[...])
    logits = _mm(n, router_ref[...])
    l8 = logits.T[0:N_EXPERTS, :]
    sub = lax.broadcasted_iota(jnp.int32, l8.shape, 0)
    m1 = jnp.max(l8, axis=0, keepdims=True)
    i1 = jnp.min(jnp.where(l8 == m1, sub, N_EXPERTS), axis=0, keepdims=True)
    rest = jnp.where(sub == i1, -jnp.inf, l8)
    m2 = jnp.max(rest, axis=0, keepdims=True)
    i2 = jnp.min(jnp.where(rest == m2, sub, N_EXPERTS), axis=0, keepdims=True)
    grp = jnp.minimum(i1, i2) * N_EXPERTS + jnp.maximum(i1, i2)
    grp_s[i] = grp

    gid = lax.broadcasted_iota(jnp.int32, (N_GROUPS, tm), 0)
    onehot = jnp.where(gid == grp, 1.0, 0.0)
    rs = lax.broadcasted_iota(jnp.int32, (tm, tm), 0)
    cs = lax.broadcasted_iota(jnp.int32, (tm, tm), 1)
    upper = jnp.where(rs <= cs, 1.0, 0.0)
    incl = _mm(onehot, upper)
    before = carry[...]
    rank = jnp.sum(onehot * (incl - 1.0 + before), axis=0, keepdims=True)
    rank_s[i] = rank.astype(jnp.int32)
    total = before + jnp.sum(onehot, axis=1, keepdims=True)
    carry[...] = total

    @pl.when(i == n_steps - 1)
    def _():
        padded = jnp.floor((total + (PAIR_TM - 1.0)) / PAIR_TM) * PAIR_TM
        gr = lax.broadcasted_iota(jnp.int32, (N_GROUPS, N_GROUPS), 0)
        gc = lax.broadcasted_iota(jnp.int32, (N_GROUPS, N_GROUPS), 1)
        lower = jnp.where(gc <= gr, 1.0, 0.0)
        ends = _mm(lower, jnp.broadcast_to(padded, (N_GROUPS, LANES)), hp=True)
        cnt_ref[...] = jnp.broadcast_to(total, cnt_ref.shape).astype(jnp.int32)
        starts = (ends - padded).astype(jnp.int32)
        start_ref[...] = starts

        group_start = [jnp.broadcast_to(starts[gid:gid + 1, 0:1], (1, tm)) for gid in PAIR_IDS]

        def to_rows(s, c):
            gs = grp_s[s]
            d = rank_s[s]
            for gid, st in zip(PAIR_IDS, group_start):
                d = d + jnp.where(gs == gid, st, 0)
            dest_ref[s] = d
            return c
        lax.fori_loop(0, n_steps, to_rows, 0)

        nv_ref[...] = (ends[N_GROUPS - 1:, :] / PAIR_TM).astype(jnp.int32)
        tile_start = lax.broadcasted_iota(jnp.int32, (N_GROUPS, TILE_LANES), 1).astype(F32) * PAIR_TM
        tile_grp = jnp.sum(jnp.where(ends[:, 0:1] <= tile_start, 1.0, 0.0), axis=0, keepdims=True)
        gidf = lax.broadcasted_iota(jnp.int32, (N_GROUPS, 1), 0).astype(F32)
        last = jnp.max(jnp.where(total > 0.0, gidf, 0.0), axis=0, keepdims=True)
        tile_ref[...] = jnp.minimum(tile_grp, last).astype(jnp.int32)


def _route(h, g, router, tm):
    rows = h.shape[0]
    n_steps = rows // tm
    const2 = lambda i: (0, 0)
    return pl.pallas_call(
        functools.partial(_route_kernel, tm=tm),
        grid=(n_steps,),
        in_specs=[
            pl.BlockSpec((tm, D_MODEL), lambda i: (i, 0)),
            pl.BlockSpec((1, D_MODEL), const2),
            pl.BlockSpec((D_MODEL, LANES), const2),
        ],
        out_specs=[
            pl.BlockSpec((n_steps, 1, tm), lambda i: (0, 0, 0)),
            pl.BlockSpec((N_GROUPS, LANES), const2),
            pl.BlockSpec((N_GROUPS, LANES), const2),
            pl.BlockSpec((1, TILE_LANES), const2),
            pl.BlockSpec((1, LANES), const2),
        ],
        out_shape=[
            jax.ShapeDtypeStruct((n_steps, 1, tm), jnp.int32),
            jax.ShapeDtypeStruct((N_GROUPS, LANES), jnp.int32),
            jax.ShapeDtypeStruct((N_GROUPS, LANES), jnp.int32),
            jax.ShapeDtypeStruct((1, TILE_LANES), jnp.int32),
            jax.ShapeDtypeStruct((1, LANES), jnp.int32),
        ],
        scratch_shapes=[
            pltpu.VMEM((N_GROUPS, 1), F32),
            pltpu.VMEM((n_steps, 1, tm), jnp.int32),
            pltpu.VMEM((n_steps, 1, tm), jnp.int32),
        ],
        compiler_params=pltpu.CompilerParams(
            dimension_semantics=("arbitrary",), vmem_limit_bytes=VMEM_LIMIT),
        name="route",
    )(h, g, router)


def _dispatch_kernel(start_ref, cnt_ref, nv_ref, dest_ref, h_ref, xs_hbm,
                     stage, zbuf, sem, fsem, *, tm, n_tiles):
    i = pl.program_id(0)
    last = pl.num_programs(0) - 1
    slot = i % 3

    def wait_rows(s):
        pltpu.make_async_copy(stage.at[s], xs_hbm.at[pl.ds(0, tm)], sem.at[s]).wait()

    def fill_ops(op):
        def tail(j, c):
            op(pltpu.make_async_copy(zbuf, xs_hbm.at[pl.ds(j * PAIR_TM, PAIR_TM)], fsem))
            return c
        lax.fori_loop(nv_ref[0], n_tiles, tail, 0)
        for gid in PAIR_IDS:
            lo = start_ref[gid] + cnt_ref[gid]
            hi = start_ref[gid] + (cnt_ref[gid] + PAIR_TM - 1) // PAIR_TM * PAIR_TM

            def row(r, c):
                op(pltpu.make_async_copy(zbuf.at[pl.ds(0, 1)], xs_hbm.at[pl.ds(r, 1)], fsem))
                return c
            lax.fori_loop(lo, hi, row, 0)

    @pl.when(i == 0)
    def _():
        zbuf[...] = jnp.zeros(zbuf.shape, F32)
        fill_ops(lambda c: c.start())

    @pl.when(i >= 3)
    def _():
        wait_rows(slot)

    stage[slot] = h_ref[...]
    for r in range(tm):
        pltpu.make_async_copy(stage.at[slot, pl.ds(r, 1)], xs_hbm.at[pl.ds(dest_ref[0, 0, r], 1)],
                              sem.at[slot]).start()

    @pl.when(i == last)
    def _():
        for s in range(3):
            wait_rows(s)
        fill_ops(lambda c: c.wait())


def _dispatch(h, start, cnt, nv, dest, n_tiles):
    rows = h.shape[0]
    tm = ROW_TM
    n_steps = rows // tm
    assert n_steps >= 3
    return pl.pallas_call(
        functools.partial(_dispatch_kernel, tm=tm, n_tiles=n_tiles),
        grid_spec=pltpu.PrefetchScalarGridSpec(
            num_scalar_prefetch=3,
            grid=(n_steps,),
            in_specs=[
                pl.BlockSpec((1, 1, tm), lambda i, s, c, n: (i, 0, 0), memory_space=pltpu.SMEM),
                pl.BlockSpec((tm, D_MODEL), lambda i, s, c, n: (i, 0)),
            ],
            out_specs=pl.BlockSpec(memory_space=pl.ANY),
            scratch_shapes=[
                pltpu.VMEM((3, tm, D_MODEL), F32),
                pltpu.VMEM((PAIR_TM, D_MODEL), F32),
                pltpu.SemaphoreType.DMA((3,)),
                pltpu.SemaphoreType.DMA,
            ],
        ),
        out_shape=jax.ShapeDtypeStruct((n_tiles * PAIR_TM, D_MODEL), F32),
        compiler_params=pltpu.CompilerParams(
            dimension_semantics=("arbitrary",), vmem_limit_bytes=VMEM_LIMIT),
        name="dispatch_rows",
    )(start, cnt, nv, dest, h)


def _collect_kernel(dest_ref, ys_hbm, out_ref, stage, sem, *, tm):
    i = pl.program_id(0)
    n_steps = pl.num_programs(0) - 1

    @pl.when(i < n_steps)
    def _():
        slot = i % 2
        for r in range(tm):
            pltpu.make_async_copy(ys_hbm.at[pl.ds(dest_ref[0, 0, r], 1)],
                                  stage.at[slot, pl.ds(r, 1)], sem.at[slot]).start()

    @pl.when(i >= 1)
    def _():
        slot = (i - 1) % 2
        pltpu.make_async_copy(ys_hbm.at[pl.ds(0, tm)], stage.at[slot], sem.at[slot]).wait()
        out_ref[...] = stage[slot]


def _collect(ys, dest):
    n_steps, _, tm = dest.shape
    return pl.pallas_call(
        functools.partial(_collect_kernel, tm=tm),
        grid=(n_steps + 1,),
        in_specs=[
            pl.BlockSpec((1, 1, tm), lambda i: (jnp.minimum(i, n_steps - 1), 0, 0),
                         memory_space=pltpu.SMEM),
            pl.BlockSpec(memory_space=pl.ANY),
        ],
        out_specs=pl.BlockSpec((tm, D_MODEL), lambda i: (jnp.maximum(i - 1, 0), 0)),
        out_shape=jax.ShapeDtypeStruct((n_steps * tm, D_MODEL), F32),
        scratch_shapes=[
            pltpu.VMEM((2, tm, D_MODEL), F32),
            pltpu.SemaphoreType.DMA((2,)),
        ],
        compiler_params=pltpu.CompilerParams(
            dimension_semantics=("arbitrary",), vmem_limit_bytes=VMEM_LIMIT),
        name="collect_rows",
    )(dest, ys)


def _pair_ffn_kernel(tg_ref, nv_ref, x_ref, g_ref, router_ref,
                     wga_ref, wua_ref, wda_ref, wgb_ref, wub_ref, wdb_ref, *rest, final_norm):
    if final_norm:
        gf_ref, out_ref = rest
    else:
        (out_ref,) = rest
    i = pl.program_id(0)

    @pl.when(i < nv_ref[0])
    def _():
        x = x_ref[...]
        nb16 = _rms(x, g_ref[...]).astype(BF16)
        logits = _mm(nb16, router_ref[...])
        lane = lax.broadcasted_iota(jnp.int32, logits.shape, 1)
        ea_id = tg_ref[i] // N_EXPERTS
        eb_id = tg_ref[i] % N_EXPERTS
        la = jnp.sum(jnp.where(lane == ea_id, logits, 0.0), axis=-1, keepdims=True)
        lb = jnp.sum(jnp.where(lane == eb_id, logits, 0.0), axis=-1, keepdims=True)
        m = jnp.maximum(la, lb)
        ea = jnp.exp(la - m)
        eb = jnp.exp(lb - m)
        den = ea + eb

        def expert(wg_ref, wu_ref, wd_ref):
            gate = _mm(nb16, wg_ref[...])
            up = _mm(nb16, wu_ref[...])
            act = (gate * _sigmoid(gate)) * up
            return _mm(act, wd_ref[...])

        y = (ea / den) * expert(wga_ref, wua_ref, wda_ref)
        y = y + (eb / den) * expert(wgb_ref, wub_ref, wdb_ref)
        res = x + y
        if final_norm:
            res = _rms(res, gf_ref[...])
        out_ref[...] = res

    @pl.when(i >= nv_ref[0])
    def _():
        out_ref[...] = jnp.zeros(out_ref.shape, F32)


def _pair_ffn(xs, tile_grp, nvalid, g, router, wg, wu, wd, g_final=None):
    tm = PAIR_TM
    n_tiles = xs.shape[0] // tm
    final_norm = g_final is not None
    const2 = lambda i, tg, nv: (0, 0)
    wa = lambda i, tg, nv: (tg[i] // N_EXPERTS, 0, 0)
    wb = lambda i, tg, nv: (tg[i] % N_EXPERTS, 0, 0)
    in_specs = [
        pl.BlockSpec((tm, D_MODEL), lambda i, tg, nv: (jnp.minimum(i, nv[0] - 1), 0)),
        pl.BlockSpec((1, D_MODEL), const2),
        pl.BlockSpec((D_MODEL, LANES), const2),
        pl.BlockSpec((None, D_MODEL, D_FF_EXPERT), wa),
        pl.BlockSpec((None, D_MODEL, D_FF_EXPERT), wa),
        pl.BlockSpec((None, D_FF_EXPERT, D_MODEL), wa),
        pl.BlockSpec((None, D_MODEL, D_FF_EXPERT), wb),
        pl.BlockSpec((None, D_MODEL, D_FF_EXPERT), wb),
        pl.BlockSpec((None, D_FF_EXPERT, D_MODEL), wb),
    ]
    args = [xs, g, router, wg, wu, wd, wg, wu, wd]
    if final_norm:
        in_specs.append(pl.BlockSpec((1, D_MODEL), const2))
        args.append(g_final)
    return pl.pallas_call(
        functools.partial(_pair_ffn_kernel, final_norm=final_norm),
        grid_spec=pltpu.PrefetchScalarGridSpec(
            num_scalar_prefetch=2,
            grid=(n_tiles,),
            in_specs=in_specs,
            out_specs=pl.BlockSpec((tm, D_MODEL), lambda i, tg, nv: (i, 0)),
        ),
        out_shape=jax.ShapeDtypeStruct((n_tiles * tm, D_MODEL), F32),
        compiler_params=pltpu.CompilerParams(
            dimension_semantics=("arbitrary",), vmem_limit_bytes=VMEM_LIMIT),
        name="pair_ffn",
    )(tile_grp, nvalid, *args)


def _sparse_moe(h, g, router, wg, wu, wd, g_final=None):
    rows = h.shape[0]
    n_tiles = rows // PAIR_TM + N_PAIRS
    assert n_tiles <= TILE_LANES
    dest, cnt, start, tile_grp, nv = _route(h, g, router, ROW_TM)
    start, cnt, nv = start[:, 0], cnt[:, 0], nv[0, :1]
    xs = _dispatch(h, start, cnt, nv, dest, n_tiles)
    ys = _pair_ffn(xs, tile_grp[0, :n_tiles], nv, g, router, wg, wu, wd, g_final=g_final)
    return _collect(ys, dest)


def kernel(x_prompt, x_sample, state_conv, state_pool, cache_swa_k, cache_swa_v, norm_mix, norm_ffn, norm_final, even_w_in, conv_w, conv_b, conv_ln_g, conv_ln_b, pool_w, pool_scale, even_w_out, ffn_w_gate, ffn_w_up, ffn_w_down, odd_w_in, sgu_ln_g, sgu_ln_b, sgu_w, sgu_b, attn_sinks, odd_w_out, router_w, exp_w_gate, exp_w_up, exp_w_down):
    bp, tp, _ = x_prompt.shape
    bs, ts, _ = x_sample.shape
    depth = norm_mix.shape[0]
    tm_p = 512
    kvw = N_KV_HEADS * HEAD_DIM

    h_p = x_prompt.reshape(bp * tp, D_MODEL)
    h_s = x_sample.reshape(bs * ts, D_MODEL)
    b16 = lambda w: w.astype(BF16)
    r2 = lambda v: v.reshape(1, -1)
    pad_router = lambda r: jnp.pad(r, ((0, 0), (0, LANES - N_EXPERTS)))

    conv_p, conv_s, pool_p, pool_s = [], [], [], []
    k_p, v_p, k_s, v_s, sgu_s = [], [], [], [], []
    for layer in range(depth):
        i = layer // 2
        last = layer == depth - 1
        gf = r2(norm_final) if last else None
        if layer % 2 == 0:
            small = (conv_w[i], r2(conv_b[i]), r2(conv_ln_g[i]), r2(conv_ln_b[i]))
            h_p, cst, pst = _even_mixer(
                h_p, bp, tp, 0, r2(norm_mix[layer]), b16(even_w_in[i]), *small, b16(pool_w[i]),
                r2(pool_scale[i]), b16(even_w_out[i]),
                jnp.zeros((bp, CONV_HIST, D_CONV), F32), jnp.zeros((bp, POOL_HIST, D_POOL), F32),
                tm_p, nseq=1, hp=False)
            conv_p.append(cst)
            pool_p.append(pst)
            h_s, cst, pst = _even_mixer(
                h_s, bs, ts, PAST_LEN, r2(norm_mix[layer]), even_w_in[i], *small, pool_w[i],
                r2(pool_scale[i]), even_w_out[i], state_conv[i], state_pool[i],
                ts, nseq=bs, hp=True)
            conv_s.append(cst)
            pool_s.append(pst)
            gn = r2(norm_ffn[layer])
            h_p = _ffn(h_p, gn, b16(ffn_w_gate[i]), b16(ffn_w_up[i]), b16(ffn_w_down[i]), tm_p,
                       g_final=gf)
            h_s = _ffn(h_s, gn, ffn_w_gate[i], ffn_w_up[i], ffn_w_down[i], bs * ts, g_final=gf,
                       hp_experts=True)
        else:
            bias = jnp.repeat(sgu_b[i].T, SGU_HEAD, axis=1)
            small = (r2(sgu_ln_g[i]), r2(sgu_ln_b[i]), sgu_w[i], bias)
            zkv = jnp.zeros((bp, WINDOW, kvw), F32)
            h_p, kn, vn = _odd_mixer(
                h_p, bp, tp, attn_sinks[i], r2(norm_mix[layer]), b16(odd_w_in[i]), *small,
                b16(odd_w_out[i]), zkv, zkv, tm_p, CHUNK, True, False, nseq=1, hp=False)
            k_p.append(kn.reshape(bp, -1, N_KV_HEADS, HEAD_DIM))
            v_p.append(vn.reshape(bp, -1, N_KV_HEADS, HEAD_DIM))
            h_s, kn, vn, sv = _odd_mixer(
                h_s, bs, ts, attn_sinks[i], r2(norm_mix[layer]), odd_w_in[i], *small, odd_w_out[i],
                cache_swa_k[i].reshape(bs, WINDOW, kvw), cache_swa_v[i].reshape(bs, WINDOW, kvw),
                ts, ts, False, True, nseq=bs, hp=True)
            k_s.append(kn.reshape(bs, ts, N_KV_HEADS, HEAD_DIM))
            v_s.append(vn.reshape(bs, ts, N_KV_HEADS, HEAD_DIM))
            sgu_s.append(sv.reshape(bs, ts, D_SGU))
            gn = r2(norm_ffn[layer])
            wg, wu, wd = b16(exp_w_gate[i]), b16(exp_w_up[i]), b16(exp_w_down[i])
            h_p = _sparse_moe(h_p, gn, pad_router(b16(router_w[i])), wg, wu, wd, g_final=gf)
            if last:
                h_s = _ffn(h_s, gn, wg, wu, wd, bs * ts, router=pad_router(router_w[i]),
                           g_final=gf, hp_router=True)
            else:
                h_s = _ffn(h_s, gn, exp_w_gate[i], exp_w_up[i], exp_w_down[i], bs * ts,
                           router=pad_router(router_w[i]), g_final=gf, hp_router=True,
                           hp_experts=True)

    return (h_p.reshape(bp, tp, D_MODEL), h_s.reshape(bs, ts, D_MODEL),
            jnp.stack(conv_p), jnp.stack(conv_s),
            jnp.stack(pool_p), jnp.stack(pool_s),
            jnp.stack(k_p), jnp.stack(v_p),
            jnp.stack(k_s), jnp.stack(v_s),
            jnp.stack(sgu_s))
```

```python
import functools

import jax
import jax.numpy as jnp
from jax import lax
from jax.experimental import pallas as pl
from jax.experimental.pallas import tpu as pltpu

F32 = jnp.float32
BF16 = jnp.bfloat16

D_MODEL = 1024
PAST_LEN = 1024
CHUNK = 64
D_CONV = 512
CONV_WIDTH = 31
CONV_HIST = CONV_WIDTH - 1
D_POOL = 512
POOL_WINDOWS = (2, 4, 8, 16)
POOL_GROUP = 128
POOL_HIST = 15
D_SGU = 512
SGU_CHUNK = 128
N_SGU_HEADS = 4
SGU_HEAD = 128
HEAD_DIM = 64
N_Q_HEADS = 8
N_KV_HEADS = 2
Q_PER_KV = 4
WINDOW = 128
D_FF = 2816
N_EXPERTS = 8
D_FF_EXPERT = 1408
EPS = 1e-6

LANES = 128
SUBLANES = 8
A_PAD = 32
P_PAD = 16
VMEM_LIMIT = 56 * 1024 * 1024
EVEN_TM = 256


def _sigmoid(x):
    return 1.0 / (1.0 + jnp.exp(-x))


def _rms(x, g):
    ms = jnp.mean(x * x, axis=-1, keepdims=True)
    return x * lax.rsqrt(ms + EPS) * g


def _layer_norm(x, g, b):
    mu = jnp.mean(x, axis=-1, keepdims=True)
    xc = x - mu
    var = jnp.mean(xc * xc, axis=-1, keepdims=True)
    return xc * lax.rsqrt(var + EPS) * g + b


def _mm(x, w, hp=False):
    if hp:
        return jnp.dot(x, w, preferred_element_type=F32, precision=lax.Precision.HIGHEST)
    return jnp.dot(x.astype(BF16), w.astype(BF16), preferred_element_type=F32)


def _mm_t(x, y, hp=False):
    dims = (((1,), (1,)), ((), ()))
    if hp:
        return lax.dot_general(x, y, dims, preferred_element_type=F32,
                               precision=lax.Precision.HIGHEST)
    return lax.dot_general(x.astype(BF16), y.astype(BF16), dims, preferred_element_type=F32)


def _even_load_history(abuf, pbuf, chist_ref, phist_ref, nseq):
    for s in range(nseq):
        abuf[s, 0:A_PAD - CONV_HIST, :] = jnp.zeros((A_PAD - CONV_HIST, D_CONV), F32)
        abuf[s, A_PAD - CONV_HIST:A_PAD, :] = chist_ref[s]
        pbuf[s, 0:P_PAD - POOL_HIST, :] = jnp.zeros((P_PAD - POOL_HIST, D_POOL), F32)
        pbuf[s, P_PAD - POOL_HIST:P_PAD, :] = phist_ref[s]


def _even_carry_state(abuf, pbuf, cstate_ref, pstate_ref, nseq, tm):
    for s in range(nseq):
        atail = abuf[s, tm:tm + A_PAD, :]
        ptail = pbuf[s, tm:tm + P_PAD, :]
        abuf[s, 0:A_PAD, :] = atail
        pbuf[s, 0:P_PAD, :] = ptail
        cstate_ref[s] = atail[A_PAD - CONV_HIST:, :]
        pstate_ref[s] = ptail[P_PAD - POOL_HIST:, :]


def _even_mix(h, t, g_ref, win_ref, cw_ref, cb_ref, lng_ref, lnb_ref, pw_ref, ps_ref, wout_ref,
              abuf, pbuf, ash, cat, *, nseq, tm, rc, start_pos, hp, between=None):
    n = _rms(h, g_ref[...])
    proj = _mm(n, win_ref[...], hp)
    a = proj[:, :D_CONV] * _sigmoid(proj[:, D_CONV:2 * D_CONV])
    for s in range(nseq):
        abuf[s, A_PAD:A_PAD + tm, :] = a[s * tm:(s + 1) * tm]
        pbuf[s, P_PAD:P_PAD + tm, :] = proj[s * tm:(s + 1) * tm, 2 * D_CONV:]
    if between is not None:
        between()

    off = A_PAD - CONV_HIST
    n_sh = A_PAD + tm - SUBLANES
    for s in range(nseq):
        for j in range(1, SUBLANES):
            for c0 in range(0, n_sh, rc):
                nr = min(rc, n_sh - c0)
                ash[j - 1, c0:c0 + nr, :] = abuf[s, c0 + j:c0 + j + nr, :]

        def tap(k, r0, ls):
            q, j = divmod(off + k, SUBLANES)
            lo = r0 + q * SUBLANES
            if j == 0:
                return abuf[s, lo:lo + rc, ls]
            return ash[j - 1, lo:lo + rc, ls]

        for r0 in range(0, tm, rc):
            o0 = s * tm + r0
            cparts = []
            for lb in range(D_CONV // LANES):
                ls = slice(lb * LANES, (lb + 1) * LANES)
                acc = tap(0, r0, ls) * cw_ref[0:1, ls]
                for k in range(1, CONV_WIDTH):
                    acc = acc + tap(k, r0, ls) * cw_ref[k:k + 1, ls]
                cparts.append(acc)
            c = jnp.concatenate(cparts, axis=-1) + cb_ref[...]
            c = _layer_norm(c, lng_ref[...], lnb_ref[...])
            c = c * _sigmoid(c)
            cat[o0:o0 + rc, 0:D_CONV] = c.astype(cat.dtype)

            pos1 = (start_pos + 1 + t * tm + r0
                    + lax.broadcasted_iota(jnp.int32, (rc, 1), 0)).astype(F32)
            for gi, w in enumerate(POOL_WINDOWS):
                ls = slice(gi * POOL_GROUP, (gi + 1) * POOL_GROUP)
                cur = pbuf[s, P_PAD + r0:P_PAD + r0 + rc, ls]
                acc = cur
                for i in range(1, w):
                    acc = acc + pbuf[s, P_PAD + r0 - i:P_PAD + r0 - i + rc, ls]
                pooled = acc / jnp.minimum(float(w), pos1) - cur
                mixed = _mm(pooled, pw_ref[gi], hp)
                cat[o0:o0 + rc, D_CONV + gi * POOL_GROUP:D_CONV + (gi + 1) * POOL_GROUP] = (
                    mixed * ps_ref[0:1, ls]).astype(cat.dtype)

    return h + _mm(cat[...], wout_ref[...], hp)


def _even_mixer_kernel(h_ref, g_ref, win_ref, cw_ref, cb_ref, lng_ref, lnb_ref, pw_ref, ps_ref,
                       wout_ref, chist_ref, phist_ref,
                       out_ref, cstate_ref, pstate_ref,
                       abuf, pbuf, ash, cat, *, nseq, tm, rc, start_pos, hp):
    t = pl.program_id(1)

    @pl.when(t == 0)
    def _():
        _even_load_history(abuf, pbuf, chist_ref, phist_ref, nseq)

    out_ref[...] = _even_mix(h_ref[...], t, g_ref, win_ref, cw_ref, cb_ref, lng_ref, lnb_ref, pw_ref,
                             ps_ref, wout_ref, abuf, pbuf, ash, cat,
                             nseq=nseq, tm=tm, rc=rc, start_pos=start_pos, hp=hp)
    _even_carry_state(abuf, pbuf, cstate_ref, pstate_ref, nseq, tm)


def _even_layer_kernel(h_ref, g_ref, win_ref, cw_ref, cb_ref, lng_ref, lnb_ref, pw_ref, ps_ref,
                       wout_ref, chist_ref, phist_ref, gff_ref, wg_ref, wu_ref, wd_ref,
                       out_ref, cstate_ref, pstate_ref,
                       abuf, pbuf, ash, cat, hbuf, hres, n_scr, *, tm, rc, nt, n_total):
    s = pl.program_id(0)
    real = s < n_total
    t = jnp.minimum(s, n_total - 1) % nt

    @pl.when(s == 0)
    def _():
        hbuf[...] = jnp.zeros((tm, D_MODEL), F32)

    @pl.when(real & (t == 0))
    def _():
        _even_load_history(abuf, pbuf, chist_ref, phist_ref, 1)

    h_prev = hbuf[...]
    hres[...] = h_prev
    n_scr[...] = _rms(h_prev, gff_ref[...]).astype(BF16)

    def swiglu_prev():
        x = n_scr[...]
        y = None
        for c0 in range(0, D_FF, D_FF_EXPERT):
            cols = slice(c0, c0 + D_FF_EXPERT)
            gate = _mm(x, wg_ref[:, cols])
            up = _mm(x, wu_ref[:, cols])
            part = _mm((gate * _sigmoid(gate)) * up, wd_ref[cols, :])
            y = part if y is None else y + part
        out_ref[...] = hres[...] + y

    hbuf[...] = _even_mix(h_ref[...], t, g_ref, win_ref, cw_ref, cb_ref, lng_ref, lnb_ref, pw_ref,
                          ps_ref, wout_ref, abuf, pbuf, ash, cat,
                          nseq=1, tm=tm, rc=rc, start_pos=0, hp=False, between=swiglu_prev)

    @pl.when(real)
    def _():
        _even_carry_state(abuf, pbuf, cstate_ref, pstate_ref, 1, tm)


def _even_layer(h, nb, tlen, g, w_in, cw, cb, lng, lnb, pw, ps, w_out, gff, wg, wu, wd, tm):
    rc = min(64, tm)
    nt = tlen // tm
    n_total = nb * nt
    cl = lambda s: jnp.minimum(s, n_total - 1)
    once = pl.Buffered(1)
    const2 = lambda s: (0, 0)
    const3 = lambda s: (0, 0, 0)
    per_b = lambda s: (cl(s) // nt, 0, 0)
    full = lambda shape: pl.BlockSpec(shape, const2 if len(shape) == 2 else const3,
                                      pipeline_mode=once)
    chist = jnp.zeros((nb, CONV_HIST, D_CONV), F32)
    phist = jnp.zeros((nb, POOL_HIST, D_POOL), F32)
    kern = functools.partial(_even_layer_kernel, tm=tm, rc=rc, nt=nt, n_total=n_total)
    return pl.pallas_call(
        kern,
        grid=(n_total + 1,),
        in_specs=[
            pl.BlockSpec((tm, D_MODEL), lambda s: (cl(s), 0)),
            full((1, D_MODEL)),
            full((D_MODEL, 3 * D_CONV)),
            full((CONV_WIDTH, D_CONV)),
            full((1, D_CONV)),
            full((1, D_CONV)),
            full((1, D_CONV)),
            full((len(POOL_WINDOWS), POOL_GROUP, POOL_GROUP)),
            full((1, D_POOL)),
            full((D_MODEL, D_MODEL)),
            pl.BlockSpec((1, CONV_HIST, D_CONV), per_b),
            pl.BlockSpec((1, POOL_HIST, D_POOL), per_b),
            full((1, D_MODEL)),
            full((D_MODEL, D_FF)),
            full((D_MODEL, D_FF)),
            full((D_FF, D_MODEL)),
        ],
        out_specs=[
            pl.BlockSpec((tm, D_MODEL), lambda s: (jnp.maximum(s - 1, 0), 0)),
            pl.BlockSpec((1, CONV_HIST, D_CONV), per_b),
            pl.BlockSpec((1, POOL_HIST, D_POOL), per_b),
        ],
        out_shape=[
            jax.ShapeDtypeStruct((nb * tlen, D_MODEL), F32),
            jax.ShapeDtypeStruct((nb, CONV_HIST, D_CONV), F32),
            jax.ShapeDtypeStruct((nb, POOL_HIST, D_POOL), F32),
        ],
        scratch_shapes=[
            pltpu.VMEM((1, A_PAD + tm, D_CONV), F32),
            pltpu.VMEM((1, P_PAD + tm, D_POOL), F32),
            pltpu.VMEM((SUBLANES - 1, A_PAD + tm - SUBLANES, D_CONV), F32),
            pltpu.VMEM((tm, D_MODEL), BF16),
            pltpu.VMEM((tm, D_MODEL), F32),
            pltpu.VMEM((tm, D_MODEL), F32),
            pltpu.VMEM((tm, D_MODEL), BF16),
        ],
        compiler_params=pltpu.CompilerParams(
            dimension_semantics=("arbitrary",), vmem_limit_bytes=VMEM_LIMIT),
        name="even_layer",
    )(h, g, w_in, cw, cb, lng, lnb, pw, ps, w_out, chist, phist, gff, wg, wu, wd)


def _even_mixer(h, nb, tlen, start_pos, g, w_in, cw, cb, lng, lnb, pw, ps, w_out, chist, phist, tm,
                nseq, hp):
    rc = min(64, tm)
    nt = tlen // tm
    rows = nseq * tm
    row = lambda b, t: (b * nt + t, 0)
    const2 = lambda b, t: (0, 0)
    const3 = lambda b, t: (0, 0, 0)
    per_b = lambda b, t: (b, 0, 0)
    kern = functools.partial(_even_mixer_kernel, nseq=nseq, tm=tm, rc=rc, start_pos=start_pos, hp=hp)
    return pl.pallas_call(
        kern,
        grid=(nb // nseq, nt),
        in_specs=[
            pl.BlockSpec((rows, D_MODEL), row),
            pl.BlockSpec((1, D_MODEL), const2),
            pl.BlockSpec((D_MODEL, 3 * D_CONV), const2),
            pl.BlockSpec((CONV_WIDTH, D_CONV), const2),
            pl.BlockSpec((1, D_CONV), const2),
            pl.BlockSpec((1, D_CONV), const2),
            pl.BlockSpec((1, D_CONV), const2),
            pl.BlockSpec((len(POOL_WINDOWS), POOL_GROUP, POOL_GROUP), const3),
            pl.BlockSpec((1, D_POOL), const2),
            pl.BlockSpec((D_MODEL, D_MODEL), const2),
            pl.BlockSpec((nseq, CONV_HIST, D_CONV), per_b),
            pl.BlockSpec((nseq, POOL_HIST, D_POOL), per_b),
        ],
        out_specs=[
            pl.BlockSpec((rows, D_MODEL), row),
            pl.BlockSpec((nseq, CONV_HIST, D_CONV), per_b),
            pl.BlockSpec((nseq, POOL_HIST, D_POOL), per_b),
        ],
        out_shape=[
            jax.ShapeDtypeStruct((nb * tlen, D_MODEL), F32),
            jax.ShapeDtypeStruct((nb, CONV_HIST, D_CONV), F32),
            jax.ShapeDtypeStruct((nb, POOL_HIST, D_POOL), F32),
        ],
        scratch_shapes=[
            pltpu.VMEM((nseq, A_PAD + tm, D_CONV), F32),
            pltpu.VMEM((nseq, P_PAD + tm, D_POOL), F32),
            pltpu.VMEM((SUBLANES - 1, A_PAD + tm - SUBLANES, D_CONV), F32),
            pltpu.VMEM((rows, D_MODEL), F32 if hp else BF16),
        ],
        compiler_params=pltpu.CompilerParams(
            dimension_semantics=("arbitrary", "arbitrary"), vmem_limit_bytes=VMEM_LIMIT),
        name="even_mixer",
    )(h, g, w_in, cw, cb, lng, lnb, pw, ps, w_out, chist, phist)


def _odd_mixer_kernel(sink_ref, h_ref, g_ref, win_ref, lng_ref, lnb_ref, ws_ref, bs_ref, wout_ref,
                      kc_ref, vc_ref, *rest, nseq, tm, cq, kv_rows, mask_first, emit_sgu_v, hp):
    if emit_sgu_v:
        out_ref, knew_ref, vnew_ref, sguv_ref, kbuf, vbuf, vs_scr, attn, cat = rest
    else:
        out_ref, knew_ref, vnew_ref, kbuf, vbuf, vs_scr, attn, cat = rest
    t = pl.program_id(1)
    op_dtype = kbuf.dtype

    @pl.when(t == 0)
    def _():
        for s in range(nseq):
            for gi in range(N_KV_HEADS):
                hs = slice(gi * HEAD_DIM, (gi + 1) * HEAD_DIM)
                kbuf[s * N_KV_HEADS + gi, 0:WINDOW, :] = kc_ref[s, :, hs].astype(op_dtype)
                vbuf[s * N_KV_HEADS + gi, 0:WINDOW, :] = vc_ref[s, :, hs].astype(op_dtype)

    h = h_ref[...]
    n = _rms(h, g_ref[...])
    proj = _mm(n, win_ref[...], hp)
    o0 = 2 * D_SGU
    o1 = o0 + N_Q_HEADS * HEAD_DIM
    o2 = o1 + N_KV_HEADS * HEAD_DIM
    zp = proj[:, :o0]
    z = 0.5 * zp * (1.0 + lax.erf(zp * (0.5 ** 0.5)))
    u = z[:, :D_SGU]
    v = _layer_norm(z[:, D_SGU:], lng_ref[...], lnb_ref[...])
    if emit_sgu_v:
        sguv_ref[...] = v

    n_sgu = -(-tm // SGU_CHUNK)
    if tm % SGU_CHUNK:
        vs_scr[...] = jnp.zeros(vs_scr.shape, vs_scr.dtype)
    for s in range(nseq):
        v0 = s * n_sgu * SGU_CHUNK
        vs_scr[v0:v0 + tm, :] = v[s * tm:(s + 1) * tm].astype(vs_scr.dtype)
    ri = lax.broadcasted_iota(jnp.int32, (SGU_CHUNK, SGU_CHUNK), 0)
    ci = lax.broadcasted_iota(jnp.int32, (SGU_CHUNK, SGU_CHUNK), 1)
    for gi in range(N_SGU_HEADS):
        ls = slice(gi * SGU_HEAD, (gi + 1) * SGU_HEAD)
        wsg = jnp.where(ri >= ci, ws_ref[gi], 0.0).astype(vs_scr.dtype)
        for s in range(nseq):
            for c in range(n_sgu):
                rows = min(SGU_CHUNK, tm - c * SGU_CHUNK)
                v0 = (s * n_sgu + c) * SGU_CHUNK
                mixed = _mm(wsg, vs_scr[v0:v0 + SGU_CHUNK, ls], hp) + bs_ref[:, ls]
                r0 = s * tm + c * SGU_CHUNK
                cat[r0:r0 + rows, ls] = (u[r0:r0 + rows, ls] * mixed[0:rows]).astype(cat.dtype)

    k = proj[:, o1:o2]
    vv = proj[:, o2:]
    q = proj[:, o0:o1].astype(op_dtype)
    nk = WINDOW + cq
    qrow = lax.broadcasted_iota(jnp.int32, (Q_PER_KV * cq, 1), 0)
    for s in range(nseq):
        knew_ref[s] = k[(s + 1) * tm - kv_rows:(s + 1) * tm, :]
        vnew_ref[s] = vv[(s + 1) * tm - kv_rows:(s + 1) * tm, :]
        for gi in range(N_KV_HEADS):
            hs = slice(gi * HEAD_DIM, (gi + 1) * HEAD_DIM)
            kbuf[s * N_KV_HEADS + gi, WINDOW:WINDOW + tm, :] = k[s * tm:(s + 1) * tm, hs].astype(op_dtype)
            vbuf[s * N_KV_HEADS + gi, WINDOW:WINDOW + tm, :] = vv[s * tm:(s + 1) * tm, hs].astype(op_dtype)
    for gi in range(N_KV_HEADS):
        sk = jnp.zeros((Q_PER_KV * cq, 1), F32)
        for i in range(Q_PER_KV):
            sk = jnp.where((qrow >= i * cq) & (qrow < (i + 1) * cq), sink_ref[gi * Q_PER_KV + i], sk)
        for s in range(nseq):
            kv = s * N_KV_HEADS + gi
            for c in range(tm // cq):
                r0 = c * cq
                g0 = s * tm + r0
                qs = jnp.concatenate(
                    [q[g0:g0 + cq, (gi * Q_PER_KV + i) * HEAD_DIM:(gi * Q_PER_KV + i + 1) * HEAD_DIM]
                     for i in range(Q_PER_KV)], axis=0)
                sc = _mm_t(qs, kbuf[kv, r0:r0 + nk, :], hp) * (HEAD_DIM ** -0.5)
                if mask_first and r0 < WINDOW:
                    key_pos = t * tm + (r0 - WINDOW) + lax.broadcasted_iota(jnp.int32, (1, nk), 1)
                    sc = jnp.where(key_pos >= 0, sc, -jnp.inf)
                m = jnp.maximum(jnp.max(sc, axis=-1, keepdims=True), sk)
                e = jnp.exp(sc - m)
                p = e / (jnp.sum(e, axis=-1, keepdims=True) + jnp.exp(sk - m))
                o = _mm(p, vbuf[kv, r0:r0 + nk, :], hp)
                for i in range(Q_PER_KV):
                    hd = gi * Q_PER_KV + i
                    attn[g0:g0 + cq, hd * HEAD_DIM:(hd + 1) * HEAD_DIM] = o[i * cq:(i + 1) * cq, :]
    cat[:, D_SGU:] = attn[...].astype(cat.dtype)

    out_ref[...] = h + _mm(cat[...], wout_ref[...], hp)

    for kv in range(nseq * N_KV_HEADS):
        ktail = kbuf[kv, tm:tm + WINDOW, :]
        vtail = vbuf[kv, tm:tm + WINDOW, :]
        kbuf[kv, 0:WINDOW, :] = ktail
        vbuf[kv, 0:WINDOW, :] = vtail


def _odd_mixer(h, nb, tlen, sinks, g, w_in, lng, lnb, ws, bs, w_out, kc, vc, tm, cq, mask_first,
               emit_sgu_v, nseq, hp):
    nt = tlen // tm
    rows = nseq * tm
    kv_rows = min(WINDOW, tlen)
    odd_in = w_in.shape[1]
    row = lambda b, t: (b * nt + t, 0)
    const2 = lambda b, t: (0, 0)
    const3 = lambda b, t: (0, 0, 0)
    per_b = lambda b, t: (b, 0, 0)
    kern = functools.partial(_odd_mixer_kernel, nseq=nseq, tm=tm, cq=cq, kv_rows=kv_rows,
                             mask_first=mask_first, emit_sgu_v=emit_sgu_v, hp=hp)
    kvw = N_KV_HEADS * HEAD_DIM
    out_specs = [
        pl.BlockSpec((rows, D_MODEL), row),
        pl.BlockSpec((nseq, kv_rows, kvw), per_b),
        pl.BlockSpec((nseq, kv_rows, kvw), per_b),
    ]
    out_shape = [
        jax.ShapeDtypeStruct((nb * tlen, D_MODEL), F32),
        jax.ShapeDtypeStruct((nb, kv_rows, kvw), F32),
        jax.ShapeDtypeStruct((nb, kv_rows, kvw), F32),
    ]
    if emit_sgu_v:
        out_specs.append(pl.BlockSpec((rows, D_SGU), row))
        out_shape.append(jax.ShapeDtypeStruct((nb * tlen, D_SGU), F32))
    n_sgu = -(-tm // SGU_CHUNK)
    op_dtype = F32 if hp else BF16
    return pl.pallas_call(
        kern,
        grid=(nb // nseq, nt),
        in_specs=[
            pl.BlockSpec(memory_space=pltpu.SMEM),
            pl.BlockSpec((rows, D_MODEL), row),
            pl.BlockSpec((1, D_MODEL), const2),
            pl.BlockSpec((D_MODEL, odd_in), const2),
            pl.BlockSpec((1, D_SGU), const2),
            pl.BlockSpec((1, D_SGU), const2),
            pl.BlockSpec((N_SGU_HEADS, SGU_CHUNK, SGU_CHUNK), const3),
            pl.BlockSpec((SGU_CHUNK, D_SGU), const2),
            pl.BlockSpec((D_MODEL, D_MODEL), const2),
            pl.BlockSpec((nseq, WINDOW, kvw), per_b),
            pl.BlockSpec((nseq, WINDOW, kvw), per_b),
        ],
        out_specs=out_specs,
        out_shape=out_shape,
        scratch_shapes=[
            pltpu.VMEM((nseq * N_KV_HEADS, WINDOW + tm, HEAD_DIM), op_dtype),
            pltpu.VMEM((nseq * N_KV_HEADS, WINDOW + tm, HEAD_DIM), op_dtype),
            pltpu.VMEM((nseq * n_sgu * SGU_CHUNK, D_SGU), op_dtype),
            pltpu.VMEM((rows, N_Q_HEADS * HEAD_DIM), F32),
            pltpu.VMEM((rows, D_MODEL), op_dtype),
        ],
        compiler_params=pltpu.CompilerParams(
            dimension_semantics=("arbitrary", "arbitrary"), vmem_limit_bytes=VMEM_LIMIT),
        name="odd_mixer",
    )(sinks, h, g, w_in, lng, lnb, ws, bs, w_out, kc, vc)


def _ffn_kernel(*refs, routed, final_norm, n_chunks, hp_router, hp_experts):
    refs = list(refs)
    h_ref, g_ref = refs[0], refs[1]
    pos = 2
    if routed:
        router_ref = refs[pos]
        pos += 1
    wg_ref, wu_ref, wd_ref = refs[pos:pos + 3]
    pos += 3
    if final_norm:
        gf_ref = refs[pos]
        pos += 1
    out_ref = refs[pos]
    n_scr, acc = refs[pos + 1], refs[pos + 2]
    if routed:
        comb = refs[pos + 3]
    j = pl.program_id(1)

    @pl.when(j == 0)
    def _():
        n = _rms(h_ref[...], g_ref[...])
        n_scr[...] = n.astype(n_scr.dtype)
        if routed:
            logits = _mm(n, router_ref[...], hp_router)
            lane = lax.broadcasted_iota(jnp.int32, logits.shape, 1)
            logits = jnp.where(lane < N_EXPERTS, logits, -jnp.inf)
            m1 = jnp.max(logits, axis=-1, keepdims=True)
            i1 = jnp.min(jnp.where(logits == m1, lane, LANES), axis=-1, keepdims=True)
            rest = jnp.where(lane == i1, -jnp.inf, logits)
            m2 = jnp.max(rest, axis=-1, keepdims=True)
            i2 = jnp.min(jnp.where(rest == m2, lane, LANES), axis=-1, keepdims=True)
            e2 = jnp.exp(m2 - m1)
            den = 1.0 + e2
            comb[...] = jnp.where(lane == i1, 1.0 / den, 0.0) + jnp.where(lane == i2, e2 / den, 0.0)

    x = n_scr[...]
    gate = _mm(x, wg_ref[...], hp_experts)
    up = _mm(x, wu_ref[...], hp_experts)
    act = (gate * _sigmoid(gate)) * up
    y = _mm(act, wd_ref[...], hp_experts)
    if routed:
        cmb = comb[...]
        lane = lax.broadcasted_iota(jnp.int32, cmb.shape, 1)
        y = y * jnp.sum(jnp.where(lane == j, cmb, 0.0), axis=-1, keepdims=True)

    @pl.when(j == 0)
    def _():
        acc[...] = y

    @pl.when(j > 0)
    def _():
        acc[...] += y

    @pl.when(j == n_chunks - 1)
    def _():
        res = h_ref[...] + acc[...]
        if final_norm:
            res = _rms(res, gf_ref[...])
        out_ref[...] = res


def _ffn(h, g, wg, wu, wd, tm, router=None, g_final=None, hp_router=False, hp_experts=False):
    rows = h.shape[0]
    routed = router is not None
    final_norm = g_final is not None
    row = lambda i, j: (i, 0)
    const2 = lambda i, j: (0, 0)
    in_specs = [pl.BlockSpec((tm, D_MODEL), row), pl.BlockSpec((1, D_MODEL), const2)]
    args = [h, g]
    if routed:
        n_chunks = N_EXPERTS
        in_specs.append(pl.BlockSpec((D_MODEL, LANES), const2))
        args.append(router)
        in_specs += [
            pl.BlockSpec((None, D_MODEL, D_FF_EXPERT), lambda i, j: (j, 0, 0)),
            pl.BlockSpec((None, D_MODEL, D_FF_EXPERT), lambda i, j: (j, 0, 0)),
            pl.BlockSpec((None, D_FF_EXPERT, D_MODEL), lambda i, j: (j, 0, 0)),
        ]
    else:
        n_chunks = D_FF // D_FF_EXPERT
        in_specs += [
            pl.BlockSpec((D_MODEL, D_FF_EXPERT), lambda i, j: (0, j)),
            pl.BlockSpec((D_MODEL, D_FF_EXPERT), lambda i, j: (0, j)),
            pl.BlockSpec((D_FF_EXPERT, D_MODEL), lambda i, j: (j, 0)),
        ]
    args += [wg, wu, wd]
    if final_norm:
        in_specs.append(pl.BlockSpec((1, D_MODEL), const2))
        args.append(g_final)
    scratch = [pltpu.VMEM((tm, D_MODEL), F32 if hp_experts else BF16), pltpu.VMEM((tm, D_MODEL), F32)]
    if routed:
        scratch.append(pltpu.VMEM((tm, LANES), F32))
    kern = functools.partial(_ffn_kernel, routed=routed, final_norm=final_norm, n_chunks=n_chunks,
                             hp_router=hp_router, hp_experts=hp_experts)
    return pl.pallas_call(
        kern,
        grid=(rows // tm, n_chunks),
        in_specs=in_specs,
        out_specs=pl.BlockSpec((tm, D_MODEL), row),
        out_shape=jax.ShapeDtypeStruct((rows, D_MODEL), F32),
        scratch_shapes=scratch,
        compiler_params=pltpu.CompilerParams(
            dimension_semantics=("arbitrary", "arbitrary"), vmem_limit_bytes=VMEM_LIMIT),
        name="moe_ffn" if routed else "dense_ffn",
    )(*args)


N_GROUPS = N_EXPERTS * N_EXPERTS
PAIR_IDS = tuple(a * N_EXPERTS + b for a in range(N_EXPERTS) for b in range(a + 1, N_EXPERTS))
N_PAIRS = len(PAIR_IDS)
PAIR_TM = 256
ROW_TM = 512
TILE_LANES = 256


def _route_kernel(h_ref, g_ref, router_ref, dest_ref, cnt_ref, start_ref, tile_ref,
                  nv_ref, carry, grp_s, rank_s, *, tm):
    i = pl.program_id(0)
    n_steps = pl.num_programs(0)

    @pl.when(i == 0)
    def _():
        carry[...] = jnp.zeros(carry.shape, F32)

    n = _rms(h_ref[...], g_ref[...])
    logits = _mm(n, router_ref[...])
    l8 = logits.T[0:N_EXPERTS, :]
    sub = lax.broadcasted_iota(jnp.int32, l8.shape, 0)
    m1 = jnp.max(l8, axis=0, keepdims=True)
    i1 = jnp.min(jnp.where(l8 == m1, sub, N_EXPERTS), axis=0, keepdims=True)
    rest = jnp.where(sub == i1, -jnp.inf, l8)
    m2 = jnp.max(rest, axis=0, keepdims=True)
    i2 = jnp.min(jnp.where(rest == m2, sub, N_EXPERTS), axis=0, keepdims=True)
    grp = jnp.minimum(i1, i2) * N_EXPERTS + jnp.maximum(i1, i2)
    grp_s[i] = grp

    gid = lax.broadcasted_iota(jnp.int32, (N_GROUPS, tm), 0)
    onehot = jnp.where(gid == grp, 1.0, 0.0)
    rs = lax.broadcasted_iota(jnp.int32, (tm, tm), 0)
    cs = lax.broadcasted_iota(jnp.int32, (tm, tm), 1)
    upper = jnp.where(rs <= cs, 1.0, 0.0)
    incl = _mm(onehot, upper)
    before = carry[...]
    rank = jnp.sum(onehot * (incl - 1.0 + before), axis=0, keepdims=True)
    rank_s[i] = rank.astype(jnp.int32)
    total = before + jnp.sum(onehot, axis=1, keepdims=True)
    carry[...] = total

    @pl.when(i == n_steps - 1)
    def _():
        padded = jnp.floor((total + (PAIR_TM - 1.0)) / PAIR_TM) * PAIR_TM
        gr = lax.broadcasted_iota(jnp.int32, (N_GROUPS, N_GROUPS), 0)
        gc = lax.broadcasted_iota(jnp.int32, (N_GROUPS, N_GROUPS), 1)
        lower = jnp.where(gc <= gr, 1.0, 0.0)
        ends = _mm(lower, jnp.broadcast_to(padded, (N_GROUPS, LANES)), hp=True)
        cnt_ref[...] = jnp.broadcast_to(total, cnt_ref.shape).astype(jnp.int32)
        starts = (ends - padded).astype(jnp.int32)
        start_ref[...] = starts

        group_start = [jnp.broadcast_to(starts[gid:gid + 1, 0:1], (1, tm)) for gid in PAIR_IDS]

        def to_rows(s, c):
            gs = grp_s[s]
            d = rank_s[s]
            for gid, st in zip(PAIR_IDS, group_start):
                d = d + jnp.where(gs == gid, st, 0)
            dest_ref[s] = d
            return c
        lax.fori_loop(0, n_steps, to_rows, 0)

        nv_ref[...] = (ends[N_GROUPS - 1:, :] / PAIR_TM).astype(jnp.int32)
        tile_start = lax.broadcasted_iota(jnp.int32, (N_GROUPS, TILE_LANES), 1).astype(F32) * PAIR_TM
        tile_grp = jnp.sum(jnp.where(ends[:, 0:1] <= tile_start, 1.0, 0.0), axis=0, keepdims=True)
        gidf = lax.broadcasted_iota(jnp.int32, (N_GROUPS, 1), 0).astype(F32)
        last = jnp.max(jnp.where(total > 0.0, gidf, 0.0), axis=0, keepdims=True)
        tile_ref[...] = jnp.minimum(tile_grp, last).astype(jnp.int32)


def _route(h, g, router, tm):
    rows = h.shape[0]
    n_steps = rows // tm
    const2 = lambda i: (0, 0)
    return pl.pallas_call(
        functools.partial(_route_kernel, tm=tm),
        grid=(n_steps,),
        in_specs=[
            pl.BlockSpec((tm, D_MODEL), lambda i: (i, 0)),
            pl.BlockSpec((1, D_MODEL), const2),
            pl.BlockSpec((D_MODEL, LANES), const2),
        ],
        out_specs=[
            pl.BlockSpec((n_steps, 1, tm), lambda i: (0, 0, 0)),
            pl.BlockSpec((N_GROUPS, LANES), const2),
            pl.BlockSpec((N_GROUPS, LANES), const2),
            pl.BlockSpec((1, TILE_LANES), const2),
            pl.BlockSpec((1, LANES), const2),
        ],
        out_shape=[
            jax.ShapeDtypeStruct((n_steps, 1, tm), jnp.int32),
            jax.ShapeDtypeStruct((N_GROUPS, LANES), jnp.int32),
            jax.ShapeDtypeStruct((N_GROUPS, LANES), jnp.int32),
            jax.ShapeDtypeStruct((1, TILE_LANES), jnp.int32),
            jax.ShapeDtypeStruct((1, LANES), jnp.int32),
        ],
        scratch_shapes=[
            pltpu.VMEM((N_GROUPS, 1), F32),
            pltpu.VMEM((n_steps, 1, tm), jnp.int32),
            pltpu.VMEM((n_steps, 1, tm), jnp.int32),
        ],
        compiler_params=pltpu.CompilerParams(
            dimension_semantics=("arbitrary",), vmem_limit_bytes=VMEM_LIMIT),
        name="route",
    )(h, g, router)


def _dispatch_kernel(start_ref, cnt_ref, nv_ref, dest_ref, h_ref, xs_hbm,
                     stage, zbuf, sem, fsem, *, tm, n_tiles):
    i = pl.program_id(0)
    last = pl.num_programs(0) - 1
    slot = i % 3

    def wait_rows(s):
        pltpu.make_async_copy(stage.at[s], xs_hbm.at[pl.ds(0, tm)], sem.at[s]).wait()

    def fill_ops(op):
        def tail(j, c):
            op(pltpu.make_async_copy(zbuf, xs_hbm.at[pl.ds(j * PAIR_TM, PAIR_TM)], fsem))
            return c
        lax.fori_loop(nv_ref[0], n_tiles, tail, 0)
        for gid in PAIR_IDS:
            lo = start_ref[gid] + cnt_ref[gid]
            hi = start_ref[gid] + (cnt_ref[gid] + PAIR_TM - 1) // PAIR_TM * PAIR_TM

            def row(r, c):
                op(pltpu.make_async_copy(zbuf.at[pl.ds(0, 1)], xs_hbm.at[pl.ds(r, 1)], fsem))
                return c
            lax.fori_loop(lo, hi, row, 0)

    @pl.when(i == 0)
    def _():
        zbuf[...] = jnp.zeros(zbuf.shape, F32)
        fill_ops(lambda c: c.start())

    @pl.when(i >= 3)
    def _():
        wait_rows(slot)

    stage[slot] = h_ref[...]
    for r in range(tm):
        pltpu.make_async_copy(stage.at[slot, pl.ds(r, 1)], xs_hbm.at[pl.ds(dest_ref[0, 0, r], 1)],
                              sem.at[slot]).start()

    @pl.when(i == last)
    def _():
        for s in range(3):
            wait_rows(s)
        fill_ops(lambda c: c.wait())


def _dispatch(h, start, cnt, nv, dest, n_tiles):
    rows = h.shape[0]
    tm = ROW_TM
    n_steps = rows // tm
    assert n_steps >= 3
    return pl.pallas_call(
        functools.partial(_dispatch_kernel, tm=tm, n_tiles=n_tiles),
        grid_spec=pltpu.PrefetchScalarGridSpec(
            num_scalar_prefetch=3,
            grid=(n_steps,),
            in_specs=[
                pl.BlockSpec((1, 1, tm), lambda i, s, c, n: (i, 0, 0), memory_space=pltpu.SMEM),
                pl.BlockSpec((tm, D_MODEL), lambda i, s, c, n: (i, 0)),
            ],
            out_specs=pl.BlockSpec(memory_space=pl.ANY),
            scratch_shapes=[
                pltpu.VMEM((3, tm, D_MODEL), F32),
                pltpu.VMEM((PAIR_TM, D_MODEL), F32),
                pltpu.SemaphoreType.DMA((3,)),
                pltpu.SemaphoreType.DMA,
            ],
        ),
        out_shape=jax.ShapeDtypeStruct((n_tiles * PAIR_TM, D_MODEL), F32),
        compiler_params=pltpu.CompilerParams(
            dimension_semantics=("arbitrary",), vmem_limit_bytes=VMEM_LIMIT),
        name="dispatch_rows",
    )(start, cnt, nv, dest, h)


def _collect_kernel(dest_ref, ys_hbm, out_ref, stage, sem, *, tm):
    i = pl.program_id(0)
    n_steps = pl.num_programs(0) - 1

    @pl.when(i < n_steps)
    def _():
        slot = i % 2
        for r in range(tm):
            pltpu.make_async_copy(ys_hbm.at[pl.ds(dest_ref[0, 0, r], 1)],
                                  stage.at[slot, pl.ds(r, 1)], sem.at[slot]).start()

    @pl.when(i >= 1)
    def _():
        slot = (i - 1) % 2
        pltpu.make_async_copy(ys_hbm.at[pl.ds(0, tm)], stage.at[slot], sem.at[slot]).wait()
        out_ref[...] = stage[slot]


def _collect(ys, dest):
    n_steps, _, tm = dest.shape
    return pl.pallas_call(
        functools.partial(_collect_kernel, tm=tm),
        grid=(n_steps + 1,),
        in_specs=[
            pl.BlockSpec((1, 1, tm), lambda i: (jnp.minimum(i, n_steps - 1), 0, 0),
                         memory_space=pltpu.SMEM),
            pl.BlockSpec(memory_space=pl.ANY),
        ],
        out_specs=pl.BlockSpec((tm, D_MODEL), lambda i: (jnp.maximum(i - 1, 0), 0)),
        out_shape=jax.ShapeDtypeStruct((n_steps * tm, D_MODEL), F32),
        scratch_shapes=[
            pltpu.VMEM((2, tm, D_MODEL), F32),
            pltpu.SemaphoreType.DMA((2,)),
        ],
        compiler_params=pltpu.CompilerParams(
            dimension_semantics=("arbitrary",), vmem_limit_bytes=VMEM_LIMIT),
        name="collect_rows",
    )(dest, ys)


def _pair_ffn_kernel(tg_ref, nv_ref, x_ref, g_ref, router_ref,
                     wga_ref, wua_ref, wda_ref, wgb_ref, wub_ref, wdb_ref, *rest, final_norm):
    if final_norm:
        gf_ref, out_ref = rest
    else:
        (out_ref,) = rest
    i = pl.program_id(0)

    @pl.when(i < nv_ref[0])
    def _():
        x = x_ref[...]
        nb16 = _rms(x, g_ref[...]).astype(BF16)
        logits = _mm(nb16, router_ref[...])
        lane = lax.broadcasted_iota(jnp.int32, logits.shape, 1)
        ea_id = tg_ref[i] // N_EXPERTS
        eb_id = tg_ref[i] % N_EXPERTS
        la = jnp.sum(jnp.where(lane == ea_id, logits, 0.0), axis=-1, keepdims=True)
        lb = jnp.sum(jnp.where(lane == eb_id, logits, 0.0), axis=-1, keepdims=True)
        m = jnp.maximum(la, lb)
        ea = jnp.exp(la - m)
        eb = jnp.exp(lb - m)
        den = ea + eb

        def expert(wg_ref, wu_ref, wd_ref):
            gate = _mm(nb16, wg_ref[...])
            up = _mm(nb16, wu_ref[...])
            act = (gate * _sigmoid(gate)) * up
            return _mm(act, wd_ref[...])

        y = (ea / den) * expert(wga_ref, wua_ref, wda_ref)
        y = y + (eb / den) * expert(wgb_ref, wub_ref, wdb_ref)
        res = x + y
        if final_norm:
            res = _rms(res, gf_ref[...])
        out_ref[...] = res

    @pl.when(i >= nv_ref[0])
    def _():
        out_ref[...] = jnp.zeros(out_ref.shape, F32)


def _pair_ffn(xs, tile_grp, nvalid, g, router, wg, wu, wd, g_final=None):
    tm = PAIR_TM
    n_tiles = xs.shape[0] // tm
    final_norm = g_final is not None
    const2 = lambda i, tg, nv: (0, 0)
    wa = lambda i, tg, nv: (tg[i] // N_EXPERTS, 0, 0)
    wb = lambda i, tg, nv: (tg[i] % N_EXPERTS, 0, 0)
    in_specs = [
        pl.BlockSpec((tm, D_MODEL), lambda i, tg, nv: (jnp.minimum(i, nv[0] - 1), 0)),
        pl.BlockSpec((1, D_MODEL), const2),
        pl.BlockSpec((D_MODEL, LANES), const2),
        pl.BlockSpec((None, D_MODEL, D_FF_EXPERT), wa),
        pl.BlockSpec((None, D_MODEL, D_FF_EXPERT), wa),
        pl.BlockSpec((None, D_FF_EXPERT, D_MODEL), wa),
        pl.BlockSpec((None, D_MODEL, D_FF_EXPERT), wb),
        pl.BlockSpec((None, D_MODEL, D_FF_EXPERT), wb),
        pl.BlockSpec((None, D_FF_EXPERT, D_MODEL), wb),
    ]
    args = [xs, g, router, wg, wu, wd, wg, wu, wd]
    if final_norm:
        in_specs.append(pl.BlockSpec((1, D_MODEL), const2))
        args.append(g_final)
    return pl.pallas_call(
        functools.partial(_pair_ffn_kernel, final_norm=final_norm),
        grid_spec=pltpu.PrefetchScalarGridSpec(
            num_scalar_prefetch=2,
            grid=(n_tiles,),
            in_specs=in_specs,
            out_specs=pl.BlockSpec((tm, D_MODEL), lambda i, tg, nv: (i, 0)),
        ),
        out_shape=jax.ShapeDtypeStruct((n_tiles * tm, D_MODEL), F32),
        compiler_params=pltpu.CompilerParams(
            dimension_semantics=("arbitrary",), vmem_limit_bytes=VMEM_LIMIT),
        name="pair_ffn",
    )(tile_grp, nvalid, *args)


def _sparse_moe(h, g, router, wg, wu, wd, g_final=None):
    rows = h.shape[0]
    n_tiles = rows // PAIR_TM + N_PAIRS
    assert n_tiles <= TILE_LANES
    dest, cnt, start, tile_grp, nv = _route(h, g, router, ROW_TM)
    start, cnt, nv = start[:, 0], cnt[:, 0], nv[0, :1]
    xs = _dispatch(h, start, cnt, nv, dest, n_tiles)
    ys = _pair_ffn(xs, tile_grp[0, :n_tiles], nv, g, router, wg, wu, wd, g_final=g_final)
    return _collect(ys, dest)


def kernel(x_prompt, x_sample, state_conv, state_pool, cache_swa_k, cache_swa_v, norm_mix, norm_ffn, norm_final, even_w_in, conv_w, conv_b, conv_ln_g, conv_ln_b, pool_w, pool_scale, even_w_out, ffn_w_gate, ffn_w_up, ffn_w_down, odd_w_in, sgu_ln_g, sgu_ln_b, sgu_w, sgu_b, attn_sinks, odd_w_out, router_w, exp_w_gate, exp_w_up, exp_w_down):
    bp, tp, _ = x_prompt.shape
    bs, ts, _ = x_sample.shape
    depth = norm_mix.shape[0]
    tm_p = 512
    kvw = N_KV_HEADS * HEAD_DIM

    h_p = x_prompt.reshape(bp * tp, D_MODEL)
    h_s = x_sample.reshape(bs * ts, D_MODEL)
    b16 = lambda w: w.astype(BF16)
    r2 = lambda v: v.reshape(1, -1)
    pad_router = lambda r: jnp.pad(r, ((0, 0), (0, LANES - N_EXPERTS)))

    conv_p, conv_s, pool_p, pool_s = [], [], [], []
    k_p, v_p, k_s, v_s, sgu_s = [], [], [], [], []
    for layer in range(depth):
        i = layer // 2
        last = layer == depth - 1
        gf = r2(norm_final) if last else None
        if layer % 2 == 0:
            small = (conv_w[i], r2(conv_b[i]), r2(conv_ln_g[i]), r2(conv_ln_b[i]))
            gn = r2(norm_ffn[layer])
            fused = not last
            if fused:
                h_p, cst, pst = _even_layer(
                    h_p, bp, tp, r2(norm_mix[layer]), b16(even_w_in[i]), *small, b16(pool_w[i]),
                    r2(pool_scale[i]), b16(even_w_out[i]), gn, b16(ffn_w_gate[i]),
                    b16(ffn_w_up[i]), b16(ffn_w_down[i]), EVEN_TM)
            else:
                h_p, cst, pst = _even_mixer(
                    h_p, bp, tp, 0, r2(norm_mix[layer]), b16(even_w_in[i]), *small, b16(pool_w[i]),
                    r2(pool_scale[i]), b16(even_w_out[i]),
                    jnp.zeros((bp, CONV_HIST, D_CONV), F32), jnp.zeros((bp, POOL_HIST, D_POOL), F32),
                    tm_p, nseq=1, hp=False)
            conv_p.append(cst)
            pool_p.append(pst)
            h_s, cst, pst = _even_mixer(
                h_s, bs, ts, PAST_LEN, r2(norm_mix[layer]), even_w_in[i], *small, pool_w[i],
                r2(pool_scale[i]), even_w_out[i], state_conv[i], state_pool[i],
                ts, nseq=bs, hp=True)
            conv_s.append(cst)
            pool_s.append(pst)
            if not fused:
                h_p = _ffn(h_p, gn, b16(ffn_w_gate[i]), b16(ffn_w_up[i]), b16(ffn_w_down[i]), tm_p,
                           g_final=gf)
            h_s = _ffn(h_s, gn, ffn_w_gate[i], ffn_w_up[i], ffn_w_down[i], bs * ts, g_final=gf,
                       hp_experts=True)
        else:
            bias = jnp.repeat(sgu_b[i].T, SGU_HEAD, axis=1)
            small = (r2(sgu_ln_g[i]), r2(sgu_ln_b[i]), sgu_w[i], bias)
            zkv = jnp.zeros((bp, WINDOW, kvw), F32)
            h_p, kn, vn = _odd_mixer(
                h_p, bp, tp, attn_sinks[i], r2(norm_mix[layer]), b16(odd_w_in[i]), *small,
                b16(odd_w_out[i]), zkv, zkv, tm_p, CHUNK, True, False, nseq=1, hp=False)
            k_p.append(kn.reshape(bp, -1, N_KV_HEADS, HEAD_DIM))
            v_p.append(vn.reshape(bp, -1, N_KV_HEADS, HEAD_DIM))
            h_s, kn, vn, sv = _odd_mixer(
                h_s, bs, ts, attn_sinks[i], r2(norm_mix[layer]), odd_w_in[i], *small, odd_w_out[i],
                cache_swa_k[i].reshape(bs, WINDOW, kvw), cache_swa_v[i].reshape(bs, WINDOW, kvw),
                ts, ts, False, True, nseq=bs, hp=True)
            k_s.append(kn.reshape(bs, ts, N_KV_HEADS, HEAD_DIM))
            v_s.append(vn.reshape(bs, ts, N_KV_HEADS, HEAD_DIM))
            sgu_s.append(sv.reshape(bs, ts, D_SGU))
            gn = r2(norm_ffn[layer])
            wg, wu, wd = b16(exp_w_gate[i]), b16(exp_w_up[i]), b16(exp_w_down[i])
            h_p = _sparse_moe(h_p, gn, pad_router(b16(router_w[i])), wg, wu, wd, g_final=gf)
            if last:
                h_s = _ffn(h_s, gn, wg, wu, wd, bs * ts, router=pad_router(router_w[i]),
                           g_final=gf, hp_router=True)
            else:
                h_s = _ffn(h_s, gn, exp_w_gate[i], exp_w_up[i], exp_w_down[i], bs * ts,
                           router=pad_router(router_w[i]), g_final=gf, hp_router=True,
                           hp_experts=True)

    return (h_p.reshape(bp, tp, D_MODEL), h_s.reshape(bs, ts, D_MODEL),
            jnp.stack(conv_p), jnp.stack(conv_s),
            jnp.stack(pool_p), jnp.stack(pool_s),
            jnp.stack(k_p), jnp.stack(v_p),
            jnp.stack(k_s), jnp.stack(v_s),
            jnp.stack(sgu_s))
```

```python
import functools

import jax
import jax.numpy as jnp
from jax import lax
from jax.experimental import pallas as pl
from jax.experimental.pallas import tpu as pltpu

F32 = jnp.float32
BF16 = jnp.bfloat16

D_MODEL = 1024
PAST_LEN = 1024
CHUNK = 64
D_CONV = 512
CONV_WIDTH = 31
CONV_HIST = CONV_WIDTH - 1
D_POOL = 512
POOL_WINDOWS = (2, 4, 8, 16)
POOL_GROUP = 128
POOL_HIST = 15
D_SGU = 512
SGU_CHUNK = 128
N_SGU_HEADS = 4
SGU_HEAD = 128
HEAD_DIM = 64
N_Q_HEADS = 8
N_KV_HEADS = 2
Q_PER_KV = 4
WINDOW = 128
D_FF = 2816
N_EXPERTS = 8
D_FF_EXPERT = 1408
EPS = 1e-6

LANES = 128
SUBLANES = 8
A_PAD = 32
P_PAD = 16
VMEM_LIMIT = 56 * 1024 * 1024
EVEN_TM = 256


def _sigmoid(x):
    return 1.0 / (1.0 + jnp.exp(-x))


def _rms(x, g):
    ms = jnp.mean(x * x, axis=-1, keepdims=True)
    return x * lax.rsqrt(ms + EPS) * g


def _layer_norm(x, g, b):
    mu = jnp.mean(x, axis=-1, keepdims=True)
    xc = x - mu
    var = jnp.mean(xc * xc, axis=-1, keepdims=True)
    return xc * lax.rsqrt(var + EPS) * g + b


def _mm(x, w, hp=False):
    if hp:
        return jnp.dot(x, w, preferred_element_type=F32, precision=lax.Precision.HIGHEST)
    return jnp.dot(x.astype(BF16), w.astype(BF16), preferred_element_type=F32)


def _zero_after(v):
    u = pltpu.bitcast(v[0:SUBLANES, 0:LANES], jnp.uint32)
    z = lax.shift_right_logical(lax.shift_right_logical(u, jnp.uint32(16)), jnp.uint32(16))
    return pltpu.bitcast(z, F32)[0:1, :]


def _mm_t(x, y, hp=False):
    dims = (((1,), (1,)), ((), ()))
    if hp:
        return lax.dot_general(x, y, dims, preferred_element_type=F32,
                               precision=lax.Precision.HIGHEST)
    return lax.dot_general(x.astype(BF16), y.astype(BF16), dims, preferred_element_type=F32)


def _even_load_history(abuf, pbuf, chist_ref, phist_ref, nseq):
    for s in range(nseq):
        abuf[s, 0:A_PAD - CONV_HIST, :] = jnp.zeros((A_PAD - CONV_HIST, D_CONV), F32)
        abuf[s, A_PAD - CONV_HIST:A_PAD, :] = chist_ref[s]
        pbuf[s, 0:P_PAD - POOL_HIST, :] = jnp.zeros((P_PAD - POOL_HIST, D_POOL), F32)
        pbuf[s, P_PAD - POOL_HIST:P_PAD, :] = phist_ref[s]


def _even_carry_state(abuf, pbuf, cstate_ref, pstate_ref, nseq, tm):
    for s in range(nseq):
        atail = abuf[s, tm:tm + A_PAD, :]
        ptail = pbuf[s, tm:tm + P_PAD, :]
        abuf[s, 0:A_PAD, :] = atail
        pbuf[s, 0:P_PAD, :] = ptail
        cstate_ref[s] = atail[A_PAD - CONV_HIST:, :]
        pstate_ref[s] = ptail[P_PAD - POOL_HIST:, :]


def _even_mix(h, t, g_ref, win_ref, cw_ref, cb_ref, lng_ref, lnb_ref, pw_ref, ps_ref, wout_ref,
              abuf, pbuf, ash, cat, *, nseq, tm, rc, start_pos, hp, between=None):
    n = _rms(h, g_ref[...])
    proj = _mm(n, win_ref[...], hp)
    a = proj[:, :D_CONV] * _sigmoid(proj[:, D_CONV:2 * D_CONV])
    for s in range(nseq):
        abuf[s, A_PAD:A_PAD + tm, :] = a[s * tm:(s + 1) * tm]
        pbuf[s, P_PAD:P_PAD + tm, :] = proj[s * tm:(s + 1) * tm, 2 * D_CONV:]
    gates = between() if between is not None else None

    off = A_PAD - CONV_HIST
    n_sh = A_PAD + tm - SUBLANES
    for s in range(nseq):
        for j in range(1, SUBLANES):
            for c0 in range(0, n_sh, rc):
                nr = min(rc, n_sh - c0)
                ash[j - 1, c0:c0 + nr, :] = abuf[s, c0 + j:c0 + j + nr, :]

        def tap(k, r0, ls):
            q, j = divmod(off + k, SUBLANES)
            lo = r0 + q * SUBLANES
            if j == 0:
                return abuf[s, lo:lo + rc, ls]
            return ash[j - 1, lo:lo + rc, ls]

        for r0 in range(0, tm, rc):
            o0 = s * tm + r0
            cparts = []
            hold = None
            if gates:
                hold = _zero_after(gates[min(r0 // rc, len(gates) - 1)])
            for lb in range(D_CONV // LANES):
                ls = slice(lb * LANES, (lb + 1) * LANES)
                w0 = cw_ref[0:1, ls] if hold is None else cw_ref[0:1, ls] + hold
                acc = tap(0, r0, ls) * w0
                for k in range(1, CONV_WIDTH):
                    acc = acc + tap(k, r0, ls) * cw_ref[k:k + 1, ls]
                cparts.append(acc)
            c = jnp.concatenate(cparts, axis=-1) + cb_ref[...]
            c = _layer_norm(c, lng_ref[...], lnb_ref[...])
            c = c * _sigmoid(c)
            cat[o0:o0 + rc, 0:D_CONV] = c.astype(cat.dtype)

            pos1 = (start_pos + 1 + t * tm + r0
                    + lax.broadcasted_iota(jnp.int32, (rc, 1), 0)).astype(F32)
            for gi, w in enumerate(POOL_WINDOWS):
                ls = slice(gi * POOL_GROUP, (gi + 1) * POOL_GROUP)
                cur = pbuf[s, P_PAD + r0:P_PAD + r0 + rc, ls]
                acc = cur
                for i in range(1, w):
                    acc = acc + pbuf[s, P_PAD + r0 - i:P_PAD + r0 - i + rc, ls]
                pooled = acc / jnp.minimum(float(w), pos1) - cur
                mixed = _mm(pooled, pw_ref[gi], hp)
                cat[o0:o0 + rc, D_CONV + gi * POOL_GROUP:D_CONV + (gi + 1) * POOL_GROUP] = (
                    mixed * ps_ref[0:1, ls]).astype(cat.dtype)

    return h + _mm(cat[...], wout_ref[...], hp)


def _even_mixer_kernel(h_ref, g_ref, win_ref, cw_ref, cb_ref, lng_ref, lnb_ref, pw_ref, ps_ref,
                       wout_ref, chist_ref, phist_ref,
                       out_ref, cstate_ref, pstate_ref,
                       abuf, pbuf, ash, cat, *, nseq, tm, rc, start_pos, hp):
    t = pl.program_id(1)

    @pl.when(t == 0)
    def _():
        _even_load_history(abuf, pbuf, chist_ref, phist_ref, nseq)

    out_ref[...] = _even_mix(h_ref[...], t, g_ref, win_ref, cw_ref, cb_ref, lng_ref, lnb_ref, pw_ref,
                             ps_ref, wout_ref, abuf, pbuf, ash, cat,
                             nseq=nseq, tm=tm, rc=rc, start_pos=start_pos, hp=hp)
    _even_carry_state(abuf, pbuf, cstate_ref, pstate_ref, nseq, tm)


def _even_layer_kernel(h_ref, g_ref, win_ref, cw_ref, cb_ref, lng_ref, lnb_ref, pw_ref, ps_ref,
                       wout_ref, chist_ref, phist_ref, gff_ref, wg_ref, wu_ref, wd_ref,
                       out_ref, cstate_ref, pstate_ref,
                       abuf, pbuf, ash, cat, hbuf, hres, n_scr, *, tm, rc, nt, n_total):
    s = pl.program_id(0)
    real = s < n_total
    t = jnp.minimum(s, n_total - 1) % nt

    @pl.when(s == 0)
    def _():
        hbuf[...] = jnp.zeros((tm, D_MODEL), F32)

    @pl.when(real & (t == 0))
    def _():
        _even_load_history(abuf, pbuf, chist_ref, phist_ref, 1)

    h_prev = hbuf[...]
    hres[...] = h_prev
    n_scr[...] = _rms(h_prev, gff_ref[...]).astype(BF16)

    def swiglu_prev():
        x = n_scr[...]
        y = None
        done = []
        for c0 in range(0, D_FF, D_FF_EXPERT):
            cols = slice(c0, c0 + D_FF_EXPERT)
            gate = _mm(x, wg_ref[:, cols])
            up = _mm(x, wu_ref[:, cols])
            part = _mm((gate * _sigmoid(gate)) * up, wd_ref[cols, :])
            y = part if y is None else y + part
            done += [gate, up, part]
        out_ref[...] = hres[...] + y
        return done

    hbuf[...] = _even_mix(h_ref[...], t, g_ref, win_ref, cw_ref, cb_ref, lng_ref, lnb_ref, pw_ref,
                          ps_ref, wout_ref, abuf, pbuf, ash, cat,
                          nseq=1, tm=tm, rc=rc, start_pos=0, hp=False, between=swiglu_prev)

    @pl.when(real)
    def _():
        _even_carry_state(abuf, pbuf, cstate_ref, pstate_ref, 1, tm)


def _even_layer(h, nb, tlen, g, w_in, cw, cb, lng, lnb, pw, ps, w_out, gff, wg, wu, wd, tm):
    rc = min(64, tm)
    nt = tlen // tm
    n_total = nb * nt
    cl = lambda s: jnp.minimum(s, n_total - 1)
    once = pl.Buffered(1)
    const2 = lambda s: (0, 0)
    const3 = lambda s: (0, 0, 0)
    per_b = lambda s: (cl(s) // nt, 0, 0)
    full = lambda shape: pl.BlockSpec(shape, const2 if len(shape) == 2 else const3,
                                      pipeline_mode=once)
    chist = jnp.zeros((nb, CONV_HIST, D_CONV), F32)
    phist = jnp.zeros((nb, POOL_HIST, D_POOL), F32)
    kern = functools.partial(_even_layer_kernel, tm=tm, rc=rc, nt=nt, n_total=n_total)
    return pl.pallas_call(
        kern,
        grid=(n_total + 1,),
        in_specs=[
            pl.BlockSpec((tm, D_MODEL), lambda s: (cl(s), 0)),
            full((1, D_MODEL)),
            full((D_MODEL, 3 * D_CONV)),
            full((CONV_WIDTH, D_CONV)),
            full((1, D_CONV)),
            full((1, D_CONV)),
            full((1, D_CONV)),
            full((len(POOL_WINDOWS), POOL_GROUP, POOL_GROUP)),
            full((1, D_POOL)),
            full((D_MODEL, D_MODEL)),
            pl.BlockSpec((1, CONV_HIST, D_CONV), per_b),
            pl.BlockSpec((1, POOL_HIST, D_POOL), per_b),
            full((1, D_MODEL)),
            full((D_MODEL, D_FF)),
            full((D_MODEL, D_FF)),
            full((D_FF, D_MODEL)),
        ],
        out_specs=[
            pl.BlockSpec((tm, D_MODEL), lambda s: (jnp.maximum(s - 1, 0), 0)),
            pl.BlockSpec((1, CONV_HIST, D_CONV), per_b),
            pl.BlockSpec((1, POOL_HIST, D_POOL), per_b),
        ],
        out_shape=[
            jax.ShapeDtypeStruct((nb * tlen, D_MODEL), F32),
            jax.ShapeDtypeStruct((nb, CONV_HIST, D_CONV), F32),
            jax.ShapeDtypeStruct((nb, POOL_HIST, D_POOL), F32),
        ],
        scratch_shapes=[
            pltpu.VMEM((1, A_PAD + tm, D_CONV), F32),
            pltpu.VMEM((1, P_PAD + tm, D_POOL), F32),
            pltpu.VMEM((SUBLANES - 1, A_PAD + tm - SUBLANES, D_CONV), F32),
            pltpu.VMEM((tm, D_MODEL), BF16),
            pltpu.VMEM((tm, D_MODEL), F32),
            pltpu.VMEM((tm, D_MODEL), F32),
            pltpu.VMEM((tm, D_MODEL), BF16),
        ],
        compiler_params=pltpu.CompilerParams(
            dimension_semantics=("arbitrary",), vmem_limit_bytes=VMEM_LIMIT),
        name="even_layer",
    )(h, g, w_in, cw, cb, lng, lnb, pw, ps, w_out, chist, phist, gff, wg, wu, wd)


def _even_mixer(h, nb, tlen, start_pos, g, w_in, cw, cb, lng, lnb, pw, ps, w_out, chist, phist, tm,
                nseq, hp):
    rc = min(64, tm)
    nt = tlen // tm
    rows = nseq * tm
    row = lambda b, t: (b * nt + t, 0)
    const2 = lambda b, t: (0, 0)
    const3 = lambda b, t: (0, 0, 0)
    per_b = lambda b, t: (b, 0, 0)
    kern = functools.partial(_even_mixer_kernel, nseq=nseq, tm=tm, rc=rc, start_pos=start_pos, hp=hp)
    return pl.pallas_call(
        kern,
        grid=(nb // nseq, nt),
        in_specs=[
            pl.BlockSpec((rows, D_MODEL), row),
            pl.BlockSpec((1, D_MODEL), const2),
            pl.BlockSpec((D_MODEL, 3 * D_CONV), const2),
            pl.BlockSpec((CONV_WIDTH, D_CONV), const2),
            pl.BlockSpec((1, D_CONV), const2),
            pl.BlockSpec((1, D_CONV), const2),
            pl.BlockSpec((1, D_CONV), const2),
            pl.BlockSpec((len(POOL_WINDOWS), POOL_GROUP, POOL_GROUP), const3),
            pl.BlockSpec((1, D_POOL), const2),
            pl.BlockSpec((D_MODEL, D_MODEL), const2),
            pl.BlockSpec((nseq, CONV_HIST, D_CONV), per_b),
            pl.BlockSpec((nseq, POOL_HIST, D_POOL), per_b),
        ],
        out_specs=[
            pl.BlockSpec((rows, D_MODEL), row),
            pl.BlockSpec((nseq, CONV_HIST, D_CONV), per_b),
            pl.BlockSpec((nseq, POOL_HIST, D_POOL), per_b),
        ],
        out_shape=[
            jax.ShapeDtypeStruct((nb * tlen, D_MODEL), F32),
            jax.ShapeDtypeStruct((nb, CONV_HIST, D_CONV), F32),
            jax.ShapeDtypeStruct((nb, POOL_HIST, D_POOL), F32),
        ],
        scratch_shapes=[
            pltpu.VMEM((nseq, A_PAD + tm, D_CONV), F32),
            pltpu.VMEM((nseq, P_PAD + tm, D_POOL), F32),
            pltpu.VMEM((SUBLANES - 1, A_PAD + tm - SUBLANES, D_CONV), F32),
            pltpu.VMEM((rows, D_MODEL), F32 if hp else BF16),
        ],
        compiler_params=pltpu.CompilerParams(
            dimension_semantics=("arbitrary", "arbitrary"), vmem_limit_bytes=VMEM_LIMIT),
        name="even_mixer",
    )(h, g, w_in, cw, cb, lng, lnb, pw, ps, w_out, chist, phist)


def _odd_mixer_kernel(sink_ref, h_ref, g_ref, win_ref, lng_ref, lnb_ref, ws_ref, bs_ref, wout_ref,
                      kc_ref, vc_ref, *rest, nseq, tm, cq, kv_rows, mask_first, emit_sgu_v, hp):
    if emit_sgu_v:
        out_ref, knew_ref, vnew_ref, sguv_ref, kbuf, vbuf, vs_scr, attn, cat = rest
    else:
        out_ref, knew_ref, vnew_ref, kbuf, vbuf, vs_scr, attn, cat = rest
    t = pl.program_id(1)
    op_dtype = kbuf.dtype

    @pl.when(t == 0)
    def _():
        for s in range(nseq):
            for gi in range(N_KV_HEADS):
                hs = slice(gi * HEAD_DIM, (gi + 1) * HEAD_DIM)
                kbuf[s * N_KV_HEADS + gi, 0:WINDOW, :] = kc_ref[s, :, hs].astype(op_dtype)
                vbuf[s * N_KV_HEADS + gi, 0:WINDOW, :] = vc_ref[s, :, hs].astype(op_dtype)

    h = h_ref[...]
    n = _rms(h, g_ref[...])
    proj = _mm(n, win_ref[...], hp)
    o0 = 2 * D_SGU
    o1 = o0 + N_Q_HEADS * HEAD_DIM
    o2 = o1 + N_KV_HEADS * HEAD_DIM
    zp = proj[:, :o0]
    z = 0.5 * zp * (1.0 + lax.erf(zp * (0.5 ** 0.5)))
    u = z[:, :D_SGU]
    v = _layer_norm(z[:, D_SGU:], lng_ref[...], lnb_ref[...])
    if emit_sgu_v:
        sguv_ref[...] = v

    n_sgu = -(-tm // SGU_CHUNK)
    if tm % SGU_CHUNK:
        vs_scr[...] = jnp.zeros(vs_scr.shape, vs_scr.dtype)
    for s in range(nseq):
        v0 = s * n_sgu * SGU_CHUNK
        vs_scr[v0:v0 + tm, :] = v[s * tm:(s + 1) * tm].astype(vs_scr.dtype)
    ri = lax.broadcasted_iota(jnp.int32, (SGU_CHUNK, SGU_CHUNK), 0)
    ci = lax.broadcasted_iota(jnp.int32, (SGU_CHUNK, SGU_CHUNK), 1)
    for gi in range(N_SGU_HEADS):
        ls = slice(gi * SGU_HEAD, (gi + 1) * SGU_HEAD)
        wsg = jnp.where(ri >= ci, ws_ref[gi], 0.0).astype(vs_scr.dtype)
        for s in range(nseq):
            for c in range(n_sgu):
                rows = min(SGU_CHUNK, tm - c * SGU_CHUNK)
                v0 = (s * n_sgu + c) * SGU_CHUNK
                mixed = _mm(wsg, vs_scr[v0:v0 + SGU_CHUNK, ls], hp) + bs_ref[:, ls]
                r0 = s * tm + c * SGU_CHUNK
                cat[r0:r0 + rows, ls] = (u[r0:r0 + rows, ls] * mixed[0:rows]).astype(cat.dtype)

    k = proj[:, o1:o2]
    vv = proj[:, o2:]
    q = proj[:, o0:o1].astype(op_dtype)
    nk = WINDOW + cq
    qrow = lax.broadcasted_iota(jnp.int32, (Q_PER_KV * cq, 1), 0)
    for s in range(nseq):
        knew_ref[s] = k[(s + 1) * tm - kv_rows:(s + 1) * tm, :]
        vnew_ref[s] = vv[(s + 1) * tm - kv_rows:(s + 1) * tm, :]
        for gi in range(N_KV_HEADS):
            hs = slice(gi * HEAD_DIM, (gi + 1) * HEAD_DIM)
            kbuf[s * N_KV_HEADS + gi, WINDOW:WINDOW + tm, :] = k[s * tm:(s + 1) * tm, hs].astype(op_dtype)
            vbuf[s * N_KV_HEADS + gi, WINDOW:WINDOW + tm, :] = vv[s * tm:(s + 1) * tm, hs].astype(op_dtype)
    for gi in range(N_KV_HEADS):
        sk = jnp.zeros((Q_PER_KV * cq, 1), F32)
        for i in range(Q_PER_KV):
            sk = jnp.where((qrow >= i * cq) & (qrow < (i + 1) * cq), sink_ref[gi * Q_PER_KV + i], sk)
        for s in range(nseq):
            kv = s * N_KV_HEADS + gi
            for c in range(tm // cq):
                r0 = c * cq
                g0 = s * tm + r0
                qs = jnp.concatenate(
                    [q[g0:g0 + cq, (gi * Q_PER_KV + i) * HEAD_DIM:(gi * Q_PER_KV + i + 1) * HEAD_DIM]
                     for i in range(Q_PER_KV)], axis=0)
                sc = _mm_t(qs, kbuf[kv, r0:r0 + nk, :], hp) * (HEAD_DIM ** -0.5)
                if mask_first and r0 < WINDOW:
                    key_pos = t * tm + (r0 - WINDOW) + lax.broadcasted_iota(jnp.int32, (1, nk), 1)
                    sc = jnp.where(key_pos >= 0, sc, -jnp.inf)
                m = jnp.maximum(jnp.max(sc, axis=-1, keepdims=True), sk)
                e = jnp.exp(sc - m)
                p = e / (jnp.sum(e, axis=-1, keepdims=True) + jnp.exp(sk - m))
                o = _mm(p, vbuf[kv, r0:r0 + nk, :], hp)
                for i in range(Q_PER_KV):
                    hd = gi * Q_PER_KV + i
                    attn[g0:g0 + cq, hd * HEAD_DIM:(hd + 1) * HEAD_DIM] = o[i * cq:(i + 1) * cq, :]
    cat[:, D_SGU:] = attn[...].astype(cat.dtype)

    out_ref[...] = h + _mm(cat[...], wout_ref[...], hp)

    for kv in range(nseq * N_KV_HEADS):
        ktail = kbuf[kv, tm:tm + WINDOW, :]
        vtail = vbuf[kv, tm:tm + WINDOW, :]
        kbuf[kv, 0:WINDOW, :] = ktail
        vbuf[kv, 0:WINDOW, :] = vtail


def _odd_mixer(h, nb, tlen, sinks, g, w_in, lng, lnb, ws, bs, w_out, kc, vc, tm, cq, mask_first,
               emit_sgu_v, nseq, hp):
    nt = tlen // tm
    rows = nseq * tm
    kv_rows = min(WINDOW, tlen)
    odd_in = w_in.shape[1]
    row = lambda b, t: (b * nt + t, 0)
    const2 = lambda b, t: (0, 0)
    const3 = lambda b, t: (0, 0, 0)
    per_b = lambda b, t: (b, 0, 0)
    kern = functools.partial(_odd_mixer_kernel, nseq=nseq, tm=tm, cq=cq, kv_rows=kv_rows,
                             mask_first=mask_first, emit_sgu_v=emit_sgu_v, hp=hp)
    kvw = N_KV_HEADS * HEAD_DIM
    out_specs = [
        pl.BlockSpec((rows, D_MODEL), row),
        pl.BlockSpec((nseq, kv_rows, kvw), per_b),
        pl.BlockSpec((nseq, kv_rows, kvw), per_b),
    ]
    out_shape = [
        jax.ShapeDtypeStruct((nb * tlen, D_MODEL), F32),
        jax.ShapeDtypeStruct((nb, kv_rows, kvw), F32),
        jax.ShapeDtypeStruct((nb, kv_rows, kvw), F32),
    ]
    if emit_sgu_v:
        out_specs.append(pl.BlockSpec((rows, D_SGU), row))
        out_shape.append(jax.ShapeDtypeStruct((nb * tlen, D_SGU), F32))
    n_sgu = -(-tm // SGU_CHUNK)
    op_dtype = F32 if hp else BF16
    return pl.pallas_call(
        kern,
        grid=(nb // nseq, nt),
        in_specs=[
            pl.BlockSpec(memory_space=pltpu.SMEM),
            pl.BlockSpec((rows, D_MODEL), row),
            pl.BlockSpec((1, D_MODEL), const2),
            pl.BlockSpec((D_MODEL, odd_in), const2),
            pl.BlockSpec((1, D_SGU), const2),
            pl.BlockSpec((1, D_SGU), const2),
            pl.BlockSpec((N_SGU_HEADS, SGU_CHUNK, SGU_CHUNK), const3),
            pl.BlockSpec((SGU_CHUNK, D_SGU), const2),
            pl.BlockSpec((D_MODEL, D_MODEL), const2),
            pl.BlockSpec((nseq, WINDOW, kvw), per_b),
            pl.BlockSpec((nseq, WINDOW, kvw), per_b),
        ],
        out_specs=out_specs,
        out_shape=out_shape,
        scratch_shapes=[
            pltpu.VMEM((nseq * N_KV_HEADS, WINDOW + tm, HEAD_DIM), op_dtype),
            pltpu.VMEM((nseq * N_KV_HEADS, WINDOW + tm, HEAD_DIM), op_dtype),
            pltpu.VMEM((nseq * n_sgu * SGU_CHUNK, D_SGU), op_dtype),
            pltpu.VMEM((rows, N_Q_HEADS * HEAD_DIM), F32),
            pltpu.VMEM((rows, D_MODEL), op_dtype),
        ],
        compiler_params=pltpu.CompilerParams(
            dimension_semantics=("arbitrary", "arbitrary"), vmem_limit_bytes=VMEM_LIMIT),
        name="odd_mixer",
    )(sinks, h, g, w_in, lng, lnb, ws, bs, w_out, kc, vc)


def _ffn_kernel(*refs, routed, final_norm, n_chunks, hp_router, hp_experts):
    refs = list(refs)
    h_ref, g_ref = refs[0], refs[1]
    pos = 2
    if routed:
        router_ref = refs[pos]
        pos += 1
    wg_ref, wu_ref, wd_ref = refs[pos:pos + 3]
    pos += 3
    if final_norm:
        gf_ref = refs[pos]
        pos += 1
    out_ref = refs[pos]
    n_scr, acc = refs[pos + 1], refs[pos + 2]
    if routed:
        comb = refs[pos + 3]
    j = pl.program_id(1)

    @pl.when(j == 0)
    def _():
        n = _rms(h_ref[...], g_ref[...])
        n_scr[...] = n.astype(n_scr.dtype)
        if routed:
            logits = _mm(n, router_ref[...], hp_router)
            lane = lax.broadcasted_iota(jnp.int32, logits.shape, 1)
            logits = jnp.where(lane < N_EXPERTS, logits, -jnp.inf)
            m1 = jnp.max(logits, axis=-1, keepdims=True)
            i1 = jnp.min(jnp.where(logits == m1, lane, LANES), axis=-1, keepdims=True)
            rest = jnp.where(lane == i1, -jnp.inf, logits)
            m2 = jnp.max(rest, axis=-1, keepdims=True)
            i2 = jnp.min(jnp.where(rest == m2, lane, LANES), axis=-1, keepdims=True)
            e2 = jnp.exp(m2 - m1)
            den = 1.0 + e2
            comb[...] = jnp.where(lane == i1, 1.0 / den, 0.0) + jnp.where(lane == i2, e2 / den, 0.0)

    x = n_scr[...]
    gate = _mm(x, wg_ref[...], hp_experts)
    up = _mm(x, wu_ref[...], hp_experts)
    act = (gate * _sigmoid(gate)) * up
    y = _mm(act, wd_ref[...], hp_experts)
    if routed:
        cmb = comb[...]
        lane = lax.broadcasted_iota(jnp.int32, cmb.shape, 1)
        y = y * jnp.sum(jnp.where(lane == j, cmb, 0.0), axis=-1, keepdims=True)

    @pl.when(j == 0)
    def _():
        acc[...] = y

    @pl.when(j > 0)
    def _():
        acc[...] += y

    @pl.when(j == n_chunks - 1)
    def _():
        res = h_ref[...] + acc[...]
        if final_norm:
            res = _rms(res, gf_ref[...])
        out_ref[...] = res


def _ffn(h, g, wg, wu, wd, tm, router=None, g_final=None, hp_router=False, hp_experts=False):
    rows = h.shape[0]
    routed = router is not None
    final_norm = g_final is not None
    row = lambda i, j: (i, 0)
    const2 = lambda i, j: (0, 0)
    in_specs = [pl.BlockSpec((tm, D_MODEL), row), pl.BlockSpec((1, D_MODEL), const2)]
    args = [h, g]
    if routed:
        n_chunks = N_EXPERTS
        in_specs.append(pl.BlockSpec((D_MODEL, LANES), const2))
        args.append(router)
        in_specs += [
            pl.BlockSpec((None, D_MODEL, D_FF_EXPERT), lambda i, j: (j, 0, 0)),
            pl.BlockSpec((None, D_MODEL, D_FF_EXPERT), lambda i, j: (j, 0, 0)),
            pl.BlockSpec((None, D_FF_EXPERT, D_MODEL), lambda i, j: (j, 0, 0)),
        ]
    else:
        n_chunks = D_FF // D_FF_EXPERT
        in_specs += [
            pl.BlockSpec((D_MODEL, D_FF_EXPERT), lambda i, j: (0, j)),
            pl.BlockSpec((D_MODEL, D_FF_EXPERT), lambda i, j: (0, j)),
            pl.BlockSpec((D_FF_EXPERT, D_MODEL), lambda i, j: (j, 0)),
        ]
    args += [wg, wu, wd]
    if final_norm:
        in_specs.append(pl.BlockSpec((1, D_MODEL), const2))
        args.append(g_final)
    scratch = [pltpu.VMEM((tm, D_MODEL), F32 if hp_experts else BF16), pltpu.VMEM((tm, D_MODEL), F32)]
    if routed:
        scratch.append(pltpu.VMEM((tm, LANES), F32))
    kern = functools.partial(_ffn_kernel, routed=routed, final_norm=final_norm, n_chunks=n_chunks,
                             hp_router=hp_router, hp_experts=hp_experts)
    return pl.pallas_call(
        kern,
        grid=(rows // tm, n_chunks),
        in_specs=in_specs,
        out_specs=pl.BlockSpec((tm, D_MODEL), row),
        out_shape=jax.ShapeDtypeStruct((rows, D_MODEL), F32),
        scratch_shapes=scratch,
        compiler_params=pltpu.CompilerParams(
            dimension_semantics=("arbitrary", "arbitrary"), vmem_limit_bytes=VMEM_LIMIT),
        name="moe_ffn" if routed else "dense_ffn",
    )(*args)


N_GROUPS = N_EXPERTS * N_EXPERTS
PAIR_IDS = tuple(a * N_EXPERTS + b for a in range(N_EXPERTS) for b in range(a + 1, N_EXPERTS))
N_PAIRS = len(PAIR_IDS)
PAIR_TM = 256
ROW_TM = 512
TILE_LANES = 256


def _route_kernel(h_ref, g_ref, router_ref, dest_ref, cnt_ref, start_ref, tile_ref,
                  nv_ref, carry, grp_s, rank_s, *, tm):
    i = pl.program_id(0)
    n_steps = pl.num_programs(0)

    @pl.when(i == 0)
    def _():
        carry[...] = jnp.zeros(carry.shape, F32)

    n = _rms(h_ref[...], g_ref[...])
    logits = _mm(n, router_ref[...])
    l8 = logits.T[0:N_EXPERTS, :]
    sub = lax.broadcasted_iota(jnp.int32, l8.shape, 0)
    m1 = jnp.max(l8, axis=0, keepdims=True)
    i1 = jnp.min(jnp.where(l8 == m1, sub, N_EXPERTS), axis=0, keepdims=True)
    rest = jnp.where(sub == i1, -jnp.inf, l8)
    m2 = jnp.max(rest, axis=0, keepdims=True)
    i2 = jnp.min(jnp.where(rest == m2, sub, N_EXPERTS), axis=0, keepdims=True)
    grp = jnp.minimum(i1, i2) * N_EXPERTS + jnp.maximum(i1, i2)
    grp_s[i] = grp

    gid = lax.broadcasted_iota(jnp.int32, (N_GROUPS, tm), 0)
    onehot = jnp.where(gid == grp, 1.0, 0.0)
    rs = lax.broadcasted_iota(jnp.int32, (tm, tm), 0)
    cs = lax.broadcasted_iota(jnp.int32, (tm, tm), 1)
    upper = jnp.where(rs <= cs, 1.0, 0.0)
    incl = _mm(onehot, upper)
    before = carry[...]
    rank = jnp.sum(onehot * (incl - 1.0 + before), axis=0, keepdims=True)
    rank_s[i] = rank.astype(jnp.int32)
    total = before + jnp.sum(onehot, axis=1, keepdims=True)
    carry[...] = total

    @pl.when(i == n_steps - 1)
    def _():
        padded = jnp.floor((total + (PAIR_TM - 1.0)) / PAIR_TM) * PAIR_TM
        gr = lax.broadcasted_iota(jnp.int32, (N_GROUPS, N_GROUPS), 0)
        gc = lax.broadcasted_iota(jnp.int32, (N_GROUPS, N_GROUPS), 1)
        lower = jnp.where(gc <= gr, 1.0, 0.0)
        ends = _mm(lower, jnp.broadcast_to(padded, (N_GROUPS, LANES)), hp=True)
        cnt_ref[...] = jnp.broadcast_to(total, cnt_ref.shape).astype(jnp.int32)
        starts = (ends - padded).astype(jnp.int32)
        start_ref[...] = starts

        group_start = [jnp.broadcast_to(starts[gid:gid + 1, 0:1], (1, tm)) for gid in PAIR_IDS]

        def to_rows(s, c):
            gs = grp_s[s]
            d = rank_s[s]
            for gid, st in zip(PAIR_IDS, group_start):
                d = d + jnp.where(gs == gid, st, 0)
            dest_ref[s] = d
            return c
        lax.fori_loop(0, n_steps, to_rows, 0)

        nv_ref[...] = (ends[N_GROUPS - 1:, :] / PAIR_TM).astype(jnp.int32)
        tile_start = lax.broadcasted_iota(jnp.int32, (N_GROUPS, TILE_LANES), 1).astype(F32) * PAIR_TM
        tile_grp = jnp.sum(jnp.where(ends[:, 0:1] <= tile_start, 1.0, 0.0), axis=0, keepdims=True)
        gidf = lax.broadcasted_iota(jnp.int32, (N_GROUPS, 1), 0).astype(F32)
        last = jnp.max(jnp.where(total > 0.0, gidf, 0.0), axis=0, keepdims=True)
        tile_ref[...] = jnp.minimum(tile_grp, last).astype(jnp.int32)


def _route(h, g, router, tm):
    rows = h.shape[0]
    n_steps = rows // tm
    const2 = lambda i: (0, 0)
    return pl.pallas_call(
        functools.partial(_route_kernel, tm=tm),
        grid=(n_steps,),
        in_specs=[
            pl.BlockSpec((tm, D_MODEL), lambda i: (i, 0)),
            pl.BlockSpec((1, D_MODEL), const2),
            pl.BlockSpec((D_MODEL, LANES), const2),
        ],
        out_specs=[
            pl.BlockSpec((n_steps, 1, tm), lambda i: (0, 0, 0)),
            pl.BlockSpec((N_GROUPS, LANES), const2),
            pl.BlockSpec((N_GROUPS, LANES), const2),
            pl.BlockSpec((1, TILE_LANES), const2),
            pl.BlockSpec((1, LANES), const2),
        ],
        out_shape=[
            jax.ShapeDtypeStruct((n_steps, 1, tm), jnp.int32),
            jax.ShapeDtypeStruct((N_GROUPS, LANES), jnp.int32),
            jax.ShapeDtypeStruct((N_GROUPS, LANES), jnp.int32),
            jax.ShapeDtypeStruct((1, TILE_LANES), jnp.int32),
            jax.ShapeDtypeStruct((1, LANES), jnp.int32),
        ],
        scratch_shapes=[
            pltpu.VMEM((N_GROUPS, 1), F32),
            pltpu.VMEM((n_steps, 1, tm), jnp.int32),
            pltpu.VMEM((n_steps, 1, tm), jnp.int32),
        ],
        compiler_params=pltpu.CompilerParams(
            dimension_semantics=("arbitrary",), vmem_limit_bytes=VMEM_LIMIT),
        name="route",
    )(h, g, router)


def _dispatch_kernel(start_ref, cnt_ref, nv_ref, dest_ref, h_ref, xs_hbm,
                     stage, zbuf, sem, fsem, *, tm, n_tiles):
    i = pl.program_id(0)
    last = pl.num_programs(0) - 1
    slot = i % 3

    def wait_rows(s):
        pltpu.make_async_copy(stage.at[s], xs_hbm.at[pl.ds(0, tm)], sem.at[s]).wait()

    def fill_ops(op):
        def tail(j, c):
            op(pltpu.make_async_copy(zbuf, xs_hbm.at[pl.ds(j * PAIR_TM, PAIR_TM)], fsem))
            return c
        lax.fori_loop(nv_ref[0], n_tiles, tail, 0)
        for gid in PAIR_IDS:
            lo = start_ref[gid] + cnt_ref[gid]
            hi = start_ref[gid] + (cnt_ref[gid] + PAIR_TM - 1) // PAIR_TM * PAIR_TM

            def row(r, c):
                op(pltpu.make_async_copy(zbuf.at[pl.ds(0, 1)], xs_hbm.at[pl.ds(r, 1)], fsem))
                return c
            lax.fori_loop(lo, hi, row, 0)

    @pl.when(i == 0)
    def _():
        zbuf[...] = jnp.zeros(zbuf.shape, F32)
        fill_ops(lambda c: c.start())

    @pl.when(i >= 3)
    def _():
        wait_rows(slot)

    stage[slot] = h_ref[...]
    for r in range(tm):
        pltpu.make_async_copy(stage.at[slot, pl.ds(r, 1)], xs_hbm.at[pl.ds(dest_ref[0, 0, r], 1)],
                              sem.at[slot]).start()

    @pl.when(i == last)
    def _():
        for s in range(3):
            wait_rows(s)
        fill_ops(lambda c: c.wait())


def _dispatch(h, start, cnt, nv, dest, n_tiles):
    rows = h.shape[0]
    tm = ROW_TM
    n_steps = rows // tm
    assert n_steps >= 3
    return pl.pallas_call(
        functools.partial(_dispatch_kernel, tm=tm, n_tiles=n_tiles),
        grid_spec=pltpu.PrefetchScalarGridSpec(
            num_scalar_prefetch=3,
            grid=(n_steps,),
            in_specs=[
                pl.BlockSpec((1, 1, tm), lambda i, s, c, n: (i, 0, 0), memory_space=pltpu.SMEM),
                pl.BlockSpec((tm, D_MODEL), lambda i, s, c, n: (i, 0)),
            ],
            out_specs=pl.BlockSpec(memory_space=pl.ANY),
            scratch_shapes=[
                pltpu.VMEM((3, tm, D_MODEL), F32),
                pltpu.VMEM((PAIR_TM, D_MODEL), F32),
                pltpu.SemaphoreType.DMA((3,)),
                pltpu.SemaphoreType.DMA,
            ],
        ),
        out_shape=jax.ShapeDtypeStruct((n_tiles * PAIR_TM, D_MODEL), F32),
        compiler_params=pltpu.CompilerParams(
            dimension_semantics=("arbitrary",), vmem_limit_bytes=VMEM_LIMIT),
        name="dispatch_rows",
    )(start, cnt, nv, dest, h)


def _collect_kernel(dest_ref, ys_hbm, out_ref, stage, sem, *, tm):
    i = pl.program_id(0)
    n_steps = pl.num_programs(0) - 1

    @pl.when(i < n_steps)
    def _():
        slot = i % 2
        for r in range(tm):
            pltpu.make_async_copy(ys_hbm.at[pl.ds(dest_ref[0, 0, r], 1)],
                                  stage.at[slot, pl.ds(r, 1)], sem.at[slot]).start()

    @pl.when(i >= 1)
    def _():
        slot = (i - 1) % 2
        pltpu.make_async_copy(ys_hbm.at[pl.ds(0, tm)], stage.at[slot], sem.at[slot]).wait()
        out_ref[...] = stage[slot]


def _collect(ys, dest):
    n_steps, _, tm = dest.shape
    return pl.pallas_call(
        functools.partial(_collect_kernel, tm=tm),
        grid=(n_steps + 1,),
        in_specs=[
            pl.BlockSpec((1, 1, tm), lambda i: (jnp.minimum(i, n_steps - 1), 0, 0),
                         memory_space=pltpu.SMEM),
            pl.BlockSpec(memory_space=pl.ANY),
        ],
        out_specs=pl.BlockSpec((tm, D_MODEL), lambda i: (jnp.maximum(i - 1, 0), 0)),
        out_shape=jax.ShapeDtypeStruct((n_steps * tm, D_MODEL), F32),
        scratch_shapes=[
            pltpu.VMEM((2, tm, D_MODEL), F32),
            pltpu.SemaphoreType.DMA((2,)),
        ],
        compiler_params=pltpu.CompilerParams(
            dimension_semantics=("arbitrary",), vmem_limit_bytes=VMEM_LIMIT),
        name="collect_rows",
    )(dest, ys)


def _pair_ffn_kernel(tg_ref, nv_ref, x_ref, g_ref, router_ref,
                     wga_ref, wua_ref, wda_ref, wgb_ref, wub_ref, wdb_ref, *rest, final_norm):
    if final_norm:
        gf_ref, out_ref = rest
    else:
        (out_ref,) = rest
    i = pl.program_id(0)

    @pl.when(i < nv_ref[0])
    def _():
        x = x_ref[...]
        nb16 = _rms(x, g_ref[...]).astype(BF16)
        logits = _mm(nb16, router_ref[...])
        lane = lax.broadcasted_iota(jnp.int32, logits.shape, 1)
        ea_id = tg_ref[i] // N_EXPERTS
        eb_id = tg_ref[i] % N_EXPERTS
        la = jnp.sum(jnp.where(lane == ea_id, logits, 0.0), axis=-1, keepdims=True)
        lb = jnp.sum(jnp.where(lane == eb_id, logits, 0.0), axis=-1, keepdims=True)
        m = jnp.maximum(la, lb)
        ea = jnp.exp(la - m)
        eb = jnp.exp(lb - m)
        den = ea + eb

        def expert(wg_ref, wu_ref, wd_ref):
            gate = _mm(nb16, wg_ref[...])
            up = _mm(nb16, wu_ref[...])
            act = (gate * _sigmoid(gate)) * up
            return _mm(act, wd_ref[...])

        y = (ea / den) * expert(wga_ref, wua_ref, wda_ref)
        y = y + (eb / den) * expert(wgb_ref, wub_ref, wdb_ref)
        res = x + y
        if final_norm:
            res = _rms(res, gf_ref[...])
        out_ref[...] = res

    @pl.when(i >= nv_ref[0])
    def _():
        out_ref[...] = jnp.zeros(out_ref.shape, F32)


def _pair_ffn(xs, tile_grp, nvalid, g, router, wg, wu, wd, g_final=None):
    tm = PAIR_TM
    n_tiles = xs.shape[0] // tm
    final_norm = g_final is not None
    const2 = lambda i, tg, nv: (0, 0)
    wa = lambda i, tg, nv: (tg[i] // N_EXPERTS, 0, 0)
    wb = lambda i, tg, nv: (tg[i] % N_EXPERTS, 0, 0)
    in_specs = [
        pl.BlockSpec((tm, D_MODEL), lambda i, tg, nv: (jnp.minimum(i, nv[0] - 1), 0)),
        pl.BlockSpec((1, D_MODEL), const2),
        pl.BlockSpec((D_MODEL, LANES), const2),
        pl.BlockSpec((None, D_MODEL, D_FF_EXPERT), wa),
        pl.BlockSpec((None, D_MODEL, D_FF_EXPERT), wa),
        pl.BlockSpec((None, D_FF_EXPERT, D_MODEL), wa),
        pl.BlockSpec((None, D_MODEL, D_FF_EXPERT), wb),
        pl.BlockSpec((None, D_MODEL, D_FF_EXPERT), wb),
        pl.BlockSpec((None, D_FF_EXPERT, D_MODEL), wb),
    ]
    args = [xs, g, router, wg, wu, wd, wg, wu, wd]
    if final_norm:
        in_specs.append(pl.BlockSpec((1, D_MODEL), const2))
        args.append(g_final)
    return pl.pallas_call(
        functools.partial(_pair_ffn_kernel, final_norm=final_norm),
        grid_spec=pltpu.PrefetchScalarGridSpec(
            num_scalar_prefetch=2,
            grid=(n_tiles,),
            in_specs=in_specs,
            out_specs=pl.BlockSpec((tm, D_MODEL), lambda i, tg, nv: (i, 0)),
        ),
        out_shape=jax.ShapeDtypeStruct((n_tiles * tm, D_MODEL), F32),
        compiler_params=pltpu.CompilerParams(
            dimension_semantics=("arbitrary",), vmem_limit_bytes=VMEM_LIMIT),
        name="pair_ffn",
    )(tile_grp, nvalid, *args)


def _sparse_moe(h, g, router, wg, wu, wd, g_final=None):
    rows = h.shape[0]
    n_tiles = rows // PAIR_TM + N_PAIRS
    assert n_tiles <= TILE_LANES
    dest, cnt, start, tile_grp, nv = _route(h, g, router, ROW_TM)
    start, cnt, nv = start[:, 0], cnt[:, 0], nv[0, :1]
    xs = _dispatch(h, start, cnt, nv, dest, n_tiles)
    ys = _pair_ffn(xs, tile_grp[0, :n_tiles], nv, g, router, wg, wu, wd, g_final=g_final)
    return _collect(ys, dest)


def kernel(x_prompt, x_sample, state_conv, state_pool, cache_swa_k, cache_swa_v, norm_mix, norm_ffn, norm_final, even_w_in, conv_w, conv_b, conv_ln_g, conv_ln_b, pool_w, pool_scale, even_w_out, ffn_w_gate, ffn_w_up, ffn_w_down, odd_w_in, sgu_ln_g, sgu_ln_b, sgu_w, sgu_b, attn_sinks, odd_w_out, router_w, exp_w_gate, exp_w_up, exp_w_down):
    bp, tp, _ = x_prompt.shape
    bs, ts, _ = x_sample.shape
    depth = norm_mix.shape[0]
    tm_p = 512
    kvw = N_KV_HEADS * HEAD_DIM

    h_p = x_prompt.reshape(bp * tp, D_MODEL)
    h_s = x_sample.reshape(bs * ts, D_MODEL)
    b16 = lambda w: w.astype(BF16)
    r2 = lambda v: v.reshape(1, -1)
    pad_router = lambda r: jnp.pad(r, ((0, 0), (0, LANES - N_EXPERTS)))

    conv_p, conv_s, pool_p, pool_s = [], [], [], []
    k_p, v_p, k_s, v_s, sgu_s = [], [], [], [], []
    for layer in range(depth):
        i = layer // 2
        last = layer == depth - 1
        gf = r2(norm_final) if last else None
        if layer % 2 == 0:
            small = (conv_w[i], r2(conv_b[i]), r2(conv_ln_g[i]), r2(conv_ln_b[i]))
            gn = r2(norm_ffn[layer])
            fused = not last
            if fused:
                h_p, cst, pst = _even_layer(
                    h_p, bp, tp, r2(norm_mix[layer]), b16(even_w_in[i]), *small, b16(pool_w[i]),
                    r2(pool_scale[i]), b16(even_w_out[i]), gn, b16(ffn_w_gate[i]),
                    b16(ffn_w_up[i]), b16(ffn_w_down[i]), EVEN_TM)
            else:
                h_p, cst, pst = _even_mixer(
                    h_p, bp, tp, 0, r2(norm_mix[layer]), b16(even_w_in[i]), *small, b16(pool_w[i]),
                    r2(pool_scale[i]), b16(even_w_out[i]),
                    jnp.zeros((bp, CONV_HIST, D_CONV), F32), jnp.zeros((bp, POOL_HIST, D_POOL), F32),
                    tm_p, nseq=1, hp=False)
            conv_p.append(cst)
            pool_p.append(pst)
            h_s, cst, pst = _even_mixer(
                h_s, bs, ts, PAST_LEN, r2(norm_mix[layer]), even_w_in[i], *small, pool_w[i],
                r2(pool_scale[i]), even_w_out[i], state_conv[i], state_pool[i],
                ts, nseq=bs, hp=True)
            conv_s.append(cst)
            pool_s.append(pst)
            if not fused:
                h_p = _ffn(h_p, gn, b16(ffn_w_gate[i]), b16(ffn_w_up[i]), b16(ffn_w_down[i]), tm_p,
                           g_final=gf)
            h_s = _ffn(h_s, gn, ffn_w_gate[i], ffn_w_up[i], ffn_w_down[i], bs * ts, g_final=gf,
                       hp_experts=True)
        else:
            bias = jnp.repeat(sgu_b[i].T, SGU_HEAD, axis=1)
            small = (r2(sgu_ln_g[i]), r2(sgu_ln_b[i]), sgu_w[i], bias)
            zkv = jnp.zeros((bp, WINDOW, kvw), F32)
            h_p, kn, vn = _odd_mixer(
                h_p, bp, tp, attn_sinks[i], r2(norm_mix[layer]), b16(odd_w_in[i]), *small,
                b16(odd_w_out[i]), zkv, zkv, tm_p, CHUNK, True, False, nseq=1, hp=False)
            k_p.append(kn.reshape(bp, -1, N_KV_HEADS, HEAD_DIM))
            v_p.append(vn.reshape(bp, -1, N_KV_HEADS, HEAD_DIM))
            h_s, kn, vn, sv = _odd_mixer(
                h_s, bs, ts, attn_sinks[i], r2(norm_mix[layer]), odd_w_in[i], *small, odd_w_out[i],
                cache_swa_k[i].reshape(bs, WINDOW, kvw), cache_swa_v[i].reshape(bs, WINDOW, kvw),
                ts, ts, False, True, nseq=bs, hp=True)
            k_s.append(kn.reshape(bs, ts, N_KV_HEADS, HEAD_DIM))
            v_s.append(vn.reshape(bs, ts, N_KV_HEADS, HEAD_DIM))
            sgu_s.append(sv.reshape(bs, ts, D_SGU))
            gn = r2(norm_ffn[layer])
            wg, wu, wd = b16(exp_w_gate[i]), b16(exp_w_up[i]), b16(exp_w_down[i])
            h_p = _sparse_moe(h_p, gn, pad_router(b16(router_w[i])), wg, wu, wd, g_final=gf)
            if last:
                h_s = _ffn(h_s, gn, wg, wu, wd, bs * ts, router=pad_router(router_w[i]),
                           g_final=gf, hp_router=True)
            else:
                h_s = _ffn(h_s, gn, exp_w_gate[i], exp_w_up[i], exp_w_down[i], bs * ts,
                           router=pad_router(router_w[i]), g_final=gf, hp_router=True,
                           hp_experts=True)

    return (h_p.reshape(bp, tp, D_MODEL), h_s.reshape(bs, ts, D_MODEL),
            jnp.stack(conv_p), jnp.stack(conv_s),
            jnp.stack(pool_p), jnp.stack(pool_s),
            jnp.stack(k_p), jnp.stack(v_p),
            jnp.stack(k_s), jnp.stack(v_s),
            jnp.stack(sgu_s))
```

```python
import functools

import jax
import jax.numpy as jnp
from jax import lax
from jax.experimental import pallas as pl
from jax.experimental.pallas import tpu as pltpu

F32 = jnp.float32
BF16 = jnp.bfloat16

D_MODEL = 1024
PAST_LEN = 1024
CHUNK = 64
D_CONV = 512
CONV_WIDTH = 31
CONV_HIST = CONV_WIDTH - 1
D_POOL = 512
POOL_WINDOWS = (2, 4, 8, 16)
POOL_GROUP = 128
POOL_HIST = 15
D_SGU = 512
SGU_CHUNK = 128
N_SGU_HEADS = 4
SGU_HEAD = 128
HEAD_DIM = 64
N_Q_HEADS = 8
N_KV_HEADS = 2
Q_PER_KV = 4
WINDOW = 128
D_FF = 2816
N_EXPERTS = 8
D_FF_EXPERT = 1408
EPS = 1e-6

LANES = 128
SUBLANES = 8
A_PAD = 32
P_PAD = 16
VMEM_LIMIT = 56 * 1024 * 1024
EVEN_TM = 512


def _sigmoid(x):
    return 1.0 / (1.0 + jnp.exp(-x))


def _rms(x, g):
    ms = jnp.mean(x * x, axis=-1, keepdims=True)
    return x * lax.rsqrt(ms + EPS) * g


def _layer_norm(x, g, b):
    mu = jnp.mean(x, axis=-1, keepdims=True)
    xc = x - mu
    var = jnp.mean(xc * xc, axis=-1, keepdims=True)
    return xc * lax.rsqrt(var + EPS) * g + b


def _mm(x, w, hp=False):
    if hp:
        return jnp.dot(x, w, preferred_element_type=F32, precision=lax.Precision.HIGHEST)
    return jnp.dot(x.astype(BF16), w.astype(BF16), preferred_element_type=F32)


def _mm_t(x, y, hp=False):
    dims = (((1,), (1,)), ((), ()))
    if hp:
        return lax.dot_general(x, y, dims, preferred_element_type=F32,
                               precision=lax.Precision.HIGHEST)
    return lax.dot_general(x.astype(BF16), y.astype(BF16), dims, preferred_element_type=F32)


def _even_load_history(abuf, pbuf, chist_ref, phist_ref, nseq):
    for s in range(nseq):
        abuf[s, 0:A_PAD - CONV_HIST, :] = jnp.zeros((A_PAD - CONV_HIST, D_CONV), F32)
        abuf[s, A_PAD - CONV_HIST:A_PAD, :] = chist_ref[s]
        pbuf[s, 0:P_PAD - POOL_HIST, :] = jnp.zeros((P_PAD - POOL_HIST, D_POOL), F32)
        pbuf[s, P_PAD - POOL_HIST:P_PAD, :] = phist_ref[s]


def _even_carry_state(abuf, pbuf, cstate_ref, pstate_ref, nseq, tm):
    for s in range(nseq):
        atail = abuf[s, tm:tm + A_PAD, :]
        ptail = pbuf[s, tm:tm + P_PAD, :]
        abuf[s, 0:A_PAD, :] = atail
        pbuf[s, 0:P_PAD, :] = ptail
        cstate_ref[s] = atail[A_PAD - CONV_HIST:, :]
        pstate_ref[s] = ptail[P_PAD - POOL_HIST:, :]


def _even_mix(h, t, g_ref, win_ref, cw_ref, cb_ref, lng_ref, lnb_ref, pw_ref, ps_ref, wout_ref,
              abuf, pbuf, ash, cat, *, nseq, tm, rc, start_pos, hp, between=None):
    n = _rms(h, g_ref[...])
    proj = _mm(n, win_ref[...], hp)
    a = proj[:, :D_CONV] * _sigmoid(proj[:, D_CONV:2 * D_CONV])
    for s in range(nseq):
        abuf[s, A_PAD:A_PAD + tm, :] = a[s * tm:(s + 1) * tm]
        pbuf[s, P_PAD:P_PAD + tm, :] = proj[s * tm:(s + 1) * tm, 2 * D_CONV:]
    if between is not None:
        between()

    off = A_PAD - CONV_HIST
    n_sh = A_PAD + tm - SUBLANES
    for s in range(nseq):
        for j in range(1, SUBLANES):
            for c0 in range(0, n_sh, rc):
                nr = min(rc, n_sh - c0)
                ash[j - 1, c0:c0 + nr, :] = abuf[s, c0 + j:c0 + j + nr, :]

        def tap(k, r0, ls):
            q, j = divmod(off + k, SUBLANES)
            lo = r0 + q * SUBLANES
            if j == 0:
                return abuf[s, lo:lo + rc, ls]
            return ash[j - 1, lo:lo + rc, ls]

        for r0 in range(0, tm, rc):
            o0 = s * tm + r0
            cparts = []
            for lb in range(D_CONV // LANES):
                ls = slice(lb * LANES, (lb + 1) * LANES)
                acc = tap(0, r0, ls) * cw_ref[0:1, ls]
                for k in range(1, CONV_WIDTH):
                    acc = acc + tap(k, r0, ls) * cw_ref[k:k + 1, ls]
                cparts.append(acc)
            c = jnp.concatenate(cparts, axis=-1) + cb_ref[...]
            c = _layer_norm(c, lng_ref[...], lnb_ref[...])
            c = c * _sigmoid(c)
            cat[o0:o0 + rc, 0:D_CONV] = c.astype(cat.dtype)

            pos1 = (start_pos + 1 + t * tm + r0
                    + lax.broadcasted_iota(jnp.int32, (rc, 1), 0)).astype(F32)
            for gi, w in enumerate(POOL_WINDOWS):
                ls = slice(gi * POOL_GROUP, (gi + 1) * POOL_GROUP)
                cur = pbuf[s, P_PAD + r0:P_PAD + r0 + rc, ls]
                acc = cur
                for i in range(1, w):
                    acc = acc + pbuf[s, P_PAD + r0 - i:P_PAD + r0 - i + rc, ls]
                pooled = acc / jnp.minimum(float(w), pos1) - cur
                mixed = _mm(pooled, pw_ref[gi], hp)
                cat[o0:o0 + rc, D_CONV + gi * POOL_GROUP:D_CONV + (gi + 1) * POOL_GROUP] = (
                    mixed * ps_ref[0:1, ls]).astype(cat.dtype)

    return h + _mm(cat[...], wout_ref[...], hp)


def _even_mixer_kernel(h_ref, g_ref, win_ref, cw_ref, cb_ref, lng_ref, lnb_ref, pw_ref, ps_ref,
                       wout_ref, chist_ref, phist_ref,
                       out_ref, cstate_ref, pstate_ref,
                       abuf, pbuf, ash, cat, *, nseq, tm, rc, start_pos, hp):
    t = pl.program_id(1)

    @pl.when(t == 0)
    def _():
        _even_load_history(abuf, pbuf, chist_ref, phist_ref, nseq)

    out_ref[...] = _even_mix(h_ref[...], t, g_ref, win_ref, cw_ref, cb_ref, lng_ref, lnb_ref, pw_ref,
                             ps_ref, wout_ref, abuf, pbuf, ash, cat,
                             nseq=nseq, tm=tm, rc=rc, start_pos=start_pos, hp=hp)
    _even_carry_state(abuf, pbuf, cstate_ref, pstate_ref, nseq, tm)


def _even_layer_kernel(h_ref, g_ref, win_ref, cw_ref, cb_ref, lng_ref, lnb_ref, pw_ref, ps_ref,
                       wout_ref, chist_ref, phist_ref, gff_ref, wg_ref, wu_ref, wd_ref,
                       out_ref, cstate_ref, pstate_ref,
                       abuf, pbuf, ash, cat, hbuf, hres, n_scr, *, tm, rc, nt, n_total):
    s = pl.program_id(0)
    real = s < n_total
    t = jnp.minimum(s, n_total - 1) % nt

    @pl.when(s == 0)
    def _():
        hbuf[...] = jnp.zeros((tm, D_MODEL), F32)

    @pl.when(real & (t == 0))
    def _():
        _even_load_history(abuf, pbuf, chist_ref, phist_ref, 1)

    h_prev = hbuf[...]
    hres[...] = h_prev
    n_scr[...] = _rms(h_prev, gff_ref[...]).astype(BF16)

    def swiglu_prev():
        x = n_scr[...]
        y = None
        for c0 in range(0, D_FF, D_FF_EXPERT):
            cols = slice(c0, c0 + D_FF_EXPERT)
            gate = _mm(x, wg_ref[:, cols])
            up = _mm(x, wu_ref[:, cols])
            part = _mm((gate * _sigmoid(gate)) * up, wd_ref[cols, :])
            y = part if y is None else y + part
        out_ref[...] = hres[...] + y

    hbuf[...] = _even_mix(h_ref[...], t, g_ref, win_ref, cw_ref, cb_ref, lng_ref, lnb_ref, pw_ref,
                          ps_ref, wout_ref, abuf, pbuf, ash, cat,
                          nseq=1, tm=tm, rc=rc, start_pos=0, hp=False, between=swiglu_prev)

    @pl.when(real)
    def _():
        _even_carry_state(abuf, pbuf, cstate_ref, pstate_ref, 1, tm)


def _even_layer(h, nb, tlen, g, w_in, cw, cb, lng, lnb, pw, ps, w_out, gff, wg, wu, wd, tm):
    rc = min(64, tm)
    nt = tlen // tm
    n_total = nb * nt
    cl = lambda s: jnp.minimum(s, n_total - 1)
    once = pl.Buffered(1)
    const2 = lambda s: (0, 0)
    const3 = lambda s: (0, 0, 0)
    per_b = lambda s: (cl(s) // nt, 0, 0)
    full = lambda shape: pl.BlockSpec(shape, const2 if len(shape) == 2 else const3,
                                      pipeline_mode=once)
    chist = jnp.zeros((nb, CONV_HIST, D_CONV), F32)
    phist = jnp.zeros((nb, POOL_HIST, D_POOL), F32)
    kern = functools.partial(_even_layer_kernel, tm=tm, rc=rc, nt=nt, n_total=n_total)
    return pl.pallas_call(
        kern,
        grid=(n_total + 1,),
        in_specs=[
            pl.BlockSpec((tm, D_MODEL), lambda s: (cl(s), 0)),
            full((1, D_MODEL)),
            full((D_MODEL, 3 * D_CONV)),
            full((CONV_WIDTH, D_CONV)),
            full((1, D_CONV)),
            full((1, D_CONV)),
            full((1, D_CONV)),
            full((len(POOL_WINDOWS), POOL_GROUP, POOL_GROUP)),
            full((1, D_POOL)),
            full((D_MODEL, D_MODEL)),
            pl.BlockSpec((1, CONV_HIST, D_CONV), per_b),
            pl.BlockSpec((1, POOL_HIST, D_POOL), per_b),
            full((1, D_MODEL)),
            full((D_MODEL, D_FF)),
            full((D_MODEL, D_FF)),
            full((D_FF, D_MODEL)),
        ],
        out_specs=[
            pl.BlockSpec((tm, D_MODEL), lambda s: (jnp.maximum(s - 1, 0), 0)),
            pl.BlockSpec((1, CONV_HIST, D_CONV), per_b),
            pl.BlockSpec((1, POOL_HIST, D_POOL), per_b),
        ],
        out_shape=[
            jax.ShapeDtypeStruct((nb * tlen, D_MODEL), F32),
            jax.ShapeDtypeStruct((nb, CONV_HIST, D_CONV), F32),
            jax.ShapeDtypeStruct((nb, POOL_HIST, D_POOL), F32),
        ],
        scratch_shapes=[
            pltpu.VMEM((1, A_PAD + tm, D_CONV), F32),
            pltpu.VMEM((1, P_PAD + tm, D_POOL), F32),
            pltpu.VMEM((SUBLANES - 1, A_PAD + tm - SUBLANES, D_CONV), F32),
            pltpu.VMEM((tm, D_MODEL), BF16),
            pltpu.VMEM((tm, D_MODEL), F32),
            pltpu.VMEM((tm, D_MODEL), F32),
            pltpu.VMEM((tm, D_MODEL), BF16),
        ],
        compiler_params=pltpu.CompilerParams(
            dimension_semantics=("arbitrary",), vmem_limit_bytes=VMEM_LIMIT),
        name="even_layer",
    )(h, g, w_in, cw, cb, lng, lnb, pw, ps, w_out, chist, phist, gff, wg, wu, wd)


def _even_mixer(h, nb, tlen, start_pos, g, w_in, cw, cb, lng, lnb, pw, ps, w_out, chist, phist, tm,
                nseq, hp):
    rc = min(64, tm)
    nt = tlen // tm
    rows = nseq * tm
    row = lambda b, t: (b * nt + t, 0)
    const2 = lambda b, t: (0, 0)
    const3 = lambda b, t: (0, 0, 0)
    per_b = lambda b, t: (b, 0, 0)
    kern = functools.partial(_even_mixer_kernel, nseq=nseq, tm=tm, rc=rc, start_pos=start_pos, hp=hp)
    return pl.pallas_call(
        kern,
        grid=(nb // nseq, nt),
        in_specs=[
            pl.BlockSpec((rows, D_MODEL), row),
            pl.BlockSpec((1, D_MODEL), const2),
            pl.BlockSpec((D_MODEL, 3 * D_CONV), const2),
            pl.BlockSpec((CONV_WIDTH, D_CONV), const2),
            pl.BlockSpec((1, D_CONV), const2),
            pl.BlockSpec((1, D_CONV), const2),
            pl.BlockSpec((1, D_CONV), const2),
            pl.BlockSpec((len(POOL_WINDOWS), POOL_GROUP, POOL_GROUP), const3),
            pl.BlockSpec((1, D_POOL), const2),
            pl.BlockSpec((D_MODEL, D_MODEL), const2),
            pl.BlockSpec((nseq, CONV_HIST, D_CONV), per_b),
            pl.BlockSpec((nseq, POOL_HIST, D_POOL), per_b),
        ],
        out_specs=[
            pl.BlockSpec((rows, D_MODEL), row),
            pl.BlockSpec((nseq, CONV_HIST, D_CONV), per_b),
            pl.BlockSpec((nseq, POOL_HIST, D_POOL), per_b),
        ],
        out_shape=[
            jax.ShapeDtypeStruct((nb * tlen, D_MODEL), F32),
            jax.ShapeDtypeStruct((nb, CONV_HIST, D_CONV), F32),
            jax.ShapeDtypeStruct((nb, POOL_HIST, D_POOL), F32),
        ],
        scratch_shapes=[
            pltpu.VMEM((nseq, A_PAD + tm, D_CONV), F32),
            pltpu.VMEM((nseq, P_PAD + tm, D_POOL), F32),
            pltpu.VMEM((SUBLANES - 1, A_PAD + tm - SUBLANES, D_CONV), F32),
            pltpu.VMEM((rows, D_MODEL), F32 if hp else BF16),
        ],
        compiler_params=pltpu.CompilerParams(
            dimension_semantics=("arbitrary", "arbitrary"), vmem_limit_bytes=VMEM_LIMIT),
        name="even_mixer",
    )(h, g, w_in, cw, cb, lng, lnb, pw, ps, w_out, chist, phist)


def _odd_mixer_kernel(sink_ref, h_ref, g_ref, win_ref, lng_ref, lnb_ref, ws_ref, bs_ref, wout_ref,
                      kc_ref, vc_ref, *rest, nseq, tm, cq, kv_rows, mask_first, emit_sgu_v, hp):
    if emit_sgu_v:
        out_ref, knew_ref, vnew_ref, sguv_ref, kbuf, vbuf, vs_scr, attn, cat = rest
    else:
        out_ref, knew_ref, vnew_ref, kbuf, vbuf, vs_scr, attn, cat = rest
    t = pl.program_id(1)
    op_dtype = kbuf.dtype

    @pl.when(t == 0)
    def _():
        for s in range(nseq):
            for gi in range(N_KV_HEADS):
                hs = slice(gi * HEAD_DIM, (gi + 1) * HEAD_DIM)
                kbuf[s * N_KV_HEADS + gi, 0:WINDOW, :] = kc_ref[s, :, hs].astype(op_dtype)
                vbuf[s * N_KV_HEADS + gi, 0:WINDOW, :] = vc_ref[s, :, hs].astype(op_dtype)

    h = h_ref[...]
    n = _rms(h, g_ref[...])
    proj = _mm(n, win_ref[...], hp)
    o0 = 2 * D_SGU
    o1 = o0 + N_Q_HEADS * HEAD_DIM
    o2 = o1 + N_KV_HEADS * HEAD_DIM
    zp = proj[:, :o0]
    z = 0.5 * zp * (1.0 + lax.erf(zp * (0.5 ** 0.5)))
    u = z[:, :D_SGU]
    v = _layer_norm(z[:, D_SGU:], lng_ref[...], lnb_ref[...])
    if emit_sgu_v:
        sguv_ref[...] = v

    n_sgu = -(-tm // SGU_CHUNK)
    if tm % SGU_CHUNK:
        vs_scr[...] = jnp.zeros(vs_scr.shape, vs_scr.dtype)
    for s in range(nseq):
        v0 = s * n_sgu * SGU_CHUNK
        vs_scr[v0:v0 + tm, :] = v[s * tm:(s + 1) * tm].astype(vs_scr.dtype)
    ri = lax.broadcasted_iota(jnp.int32, (SGU_CHUNK, SGU_CHUNK), 0)
    ci = lax.broadcasted_iota(jnp.int32, (SGU_CHUNK, SGU_CHUNK), 1)
    for gi in range(N_SGU_HEADS):
        ls = slice(gi * SGU_HEAD, (gi + 1) * SGU_HEAD)
        wsg = jnp.where(ri >= ci, ws_ref[gi], 0.0).astype(vs_scr.dtype)
        for s in range(nseq):
            for c in range(n_sgu):
                rows = min(SGU_CHUNK, tm - c * SGU_CHUNK)
                v0 = (s * n_sgu + c) * SGU_CHUNK
                mixed = _mm(wsg, vs_scr[v0:v0 + SGU_CHUNK, ls], hp) + bs_ref[:, ls]
                r0 = s * tm + c * SGU_CHUNK
                cat[r0:r0 + rows, ls] = (u[r0:r0 + rows, ls] * mixed[0:rows]).astype(cat.dtype)

    k = proj[:, o1:o2]
    vv = proj[:, o2:]
    q = proj[:, o0:o1].astype(op_dtype)
    nk = WINDOW + cq
    qrow = lax.broadcasted_iota(jnp.int32, (Q_PER_KV * cq, 1), 0)
    for s in range(nseq):
        knew_ref[s] = k[(s + 1) * tm - kv_rows:(s + 1) * tm, :]
        vnew_ref[s] = vv[(s + 1) * tm - kv_rows:(s + 1) * tm, :]
        for gi in range(N_KV_HEADS):
            hs = slice(gi * HEAD_DIM, (gi + 1) * HEAD_DIM)
            kbuf[s * N_KV_HEADS + gi, WINDOW:WINDOW + tm, :] = k[s * tm:(s + 1) * tm, hs].astype(op_dtype)
            vbuf[s * N_KV_HEADS + gi, WINDOW:WINDOW + tm, :] = vv[s * tm:(s + 1) * tm, hs].astype(op_dtype)
    for gi in range(N_KV_HEADS):
        sk = jnp.zeros((Q_PER_KV * cq, 1), F32)
        for i in range(Q_PER_KV):
            sk = jnp.where((qrow >= i * cq) & (qrow < (i + 1) * cq), sink_ref[gi * Q_PER_KV + i], sk)
        for s in range(nseq):
            kv = s * N_KV_HEADS + gi
            for c in range(tm // cq):
                r0 = c * cq
                g0 = s * tm + r0
                qs = jnp.concatenate(
                    [q[g0:g0 + cq, (gi * Q_PER_KV + i) * HEAD_DIM:(gi * Q_PER_KV + i + 1) * HEAD_DIM]
                     for i in range(Q_PER_KV)], axis=0)
                sc = _mm_t(qs, kbuf[kv, r0:r0 + nk, :], hp) * (HEAD_DIM ** -0.5)
                if mask_first and r0 < WINDOW:
                    key_pos = t * tm + (r0 - WINDOW) + lax.broadcasted_iota(jnp.int32, (1, nk), 1)
                    sc = jnp.where(key_pos >= 0, sc, -jnp.inf)
                m = jnp.maximum(jnp.max(sc, axis=-1, keepdims=True), sk)
                e = jnp.exp(sc - m)
                p = e / (jnp.sum(e, axis=-1, keepdims=True) + jnp.exp(sk - m))
                o = _mm(p, vbuf[kv, r0:r0 + nk, :], hp)
                for i in range(Q_PER_KV):
                    hd = gi * Q_PER_KV + i
                    attn[g0:g0 + cq, hd * HEAD_DIM:(hd + 1) * HEAD_DIM] = o[i * cq:(i + 1) * cq, :]
    cat[:, D_SGU:] = attn[...].astype(cat.dtype)

    out_ref[...] = h + _mm(cat[...], wout_ref[...], hp)

    for kv in range(nseq * N_KV_HEADS):
        ktail = kbuf[kv, tm:tm + WINDOW, :]
        vtail = vbuf[kv, tm:tm + WINDOW, :]
        kbuf[kv, 0:WINDOW, :] = ktail
        vbuf[kv, 0:WINDOW, :] = vtail


def _odd_mixer(h, nb, tlen, sinks, g, w_in, lng, lnb, ws, bs, w_out, kc, vc, tm, cq, mask_first,
               emit_sgu_v, nseq, hp):
    nt = tlen // tm
    rows = nseq * tm
    kv_rows = min(WINDOW, tlen)
    odd_in = w_in.shape[1]
    row = lambda b, t: (b * nt + t, 0)
    const2 = lambda b, t: (0, 0)
    const3 = lambda b, t: (0, 0, 0)
    per_b = lambda b, t: (b, 0, 0)
    kern = functools.partial(_odd_mixer_kernel, nseq=nseq, tm=tm, cq=cq, kv_rows=kv_rows,
                             mask_first=mask_first, emit_sgu_v=emit_sgu_v, hp=hp)
    kvw = N_KV_HEADS * HEAD_DIM
    out_specs = [
        pl.BlockSpec((rows, D_MODEL), row),
        pl.BlockSpec((nseq, kv_rows, kvw), per_b),
        pl.BlockSpec((nseq, kv_rows, kvw), per_b),
    ]
    out_shape = [
        jax.ShapeDtypeStruct((nb * tlen, D_MODEL), F32),
        jax.ShapeDtypeStruct((nb, kv_rows, kvw), F32),
        jax.ShapeDtypeStruct((nb, kv_rows, kvw), F32),
    ]
    if emit_sgu_v:
        out_specs.append(pl.BlockSpec((rows, D_SGU), row))
        out_shape.append(jax.ShapeDtypeStruct((nb * tlen, D_SGU), F32))
    n_sgu = -(-tm // SGU_CHUNK)
    op_dtype = F32 if hp else BF16
    return pl.pallas_call(
        kern,
        grid=(nb // nseq, nt),
        in_specs=[
            pl.BlockSpec(memory_space=pltpu.SMEM),
            pl.BlockSpec((rows, D_MODEL), row),
            pl.BlockSpec((1, D_MODEL), const2),
            pl.BlockSpec((D_MODEL, odd_in), const2),
            pl.BlockSpec((1, D_SGU), const2),
            pl.BlockSpec((1, D_SGU), const2),
            pl.BlockSpec((N_SGU_HEADS, SGU_CHUNK, SGU_CHUNK), const3),
            pl.BlockSpec((SGU_CHUNK, D_SGU), const2),
            pl.BlockSpec((D_MODEL, D_MODEL), const2),
            pl.BlockSpec((nseq, WINDOW, kvw), per_b),
            pl.BlockSpec((nseq, WINDOW, kvw), per_b),
        ],
        out_specs=out_specs,
        out_shape=out_shape,
        scratch_shapes=[
            pltpu.VMEM((nseq * N_KV_HEADS, WINDOW + tm, HEAD_DIM), op_dtype),
            pltpu.VMEM((nseq * N_KV_HEADS, WINDOW + tm, HEAD_DIM), op_dtype),
            pltpu.VMEM((nseq * n_sgu * SGU_CHUNK, D_SGU), op_dtype),
            pltpu.VMEM((rows, N_Q_HEADS * HEAD_DIM), F32),
            pltpu.VMEM((rows, D_MODEL), op_dtype),
        ],
        compiler_params=pltpu.CompilerParams(
            dimension_semantics=("arbitrary", "arbitrary"), vmem_limit_bytes=VMEM_LIMIT),
        name="odd_mixer",
    )(sinks, h, g, w_in, lng, lnb, ws, bs, w_out, kc, vc)


def _ffn_kernel(*refs, routed, final_norm, n_chunks, hp_router, hp_experts):
    refs = list(refs)
    h_ref, g_ref = refs[0], refs[1]
    pos = 2
    if routed:
        router_ref = refs[pos]
        pos += 1
    wg_ref, wu_ref, wd_ref = refs[pos:pos + 3]
    pos += 3
    if final_norm:
        gf_ref = refs[pos]
        pos += 1
    out_ref = refs[pos]
    n_scr, acc = refs[pos + 1], refs[pos + 2]
    if routed:
        comb = refs[pos + 3]
    j = pl.program_id(1)

    @pl.when(j == 0)
    def _():
        n = _rms(h_ref[...], g_ref[...])
        n_scr[...] = n.astype(n_scr.dtype)
        if routed:
            logits = _mm(n, router_ref[...], hp_router)
            lane = lax.broadcasted_iota(jnp.int32, logits.shape, 1)
            logits = jnp.where(lane < N_EXPERTS, logits, -jnp.inf)
            m1 = jnp.max(logits, axis=-1, keepdims=True)
            i1 = jnp.min(jnp.where(logits == m1, lane, LANES), axis=-1, keepdims=True)
            rest = jnp.where(lane == i1, -jnp.inf, logits)
            m2 = jnp.max(rest, axis=-1, keepdims=True)
            i2 = jnp.min(jnp.where(rest == m2, lane, LANES), axis=-1, keepdims=True)
            e2 = jnp.exp(m2 - m1)
            den = 1.0 + e2
            comb[...] = jnp.where(lane == i1, 1.0 / den, 0.0) + jnp.where(lane == i2, e2 / den, 0.0)

    x = n_scr[...]
    gate = _mm(x, wg_ref[...], hp_experts)
    up = _mm(x, wu_ref[...], hp_experts)
    act = (gate * _sigmoid(gate)) * up
    y = _mm(act, wd_ref[...], hp_experts)
    if routed:
        cmb = comb[...]
        lane = lax.broadcasted_iota(jnp.int32, cmb.shape, 1)
        y = y * jnp.sum(jnp.where(lane == j, cmb, 0.0), axis=-1, keepdims=True)

    @pl.when(j == 0)
    def _():
        acc[...] = y

    @pl.when(j > 0)
    def _():
        acc[...] += y

    @pl.when(j == n_chunks - 1)
    def _():
        res = h_ref[...] + acc[...]
        if final_norm:
            res = _rms(res, gf_ref[...])
        out_ref[...] = res


def _ffn(h, g, wg, wu, wd, tm, router=None, g_final=None, hp_router=False, hp_experts=False):
    rows = h.shape[0]
    routed = router is not None
    final_norm = g_final is not None
    row = lambda i, j: (i, 0)
    const2 = lambda i, j: (0, 0)
    in_specs = [pl.BlockSpec((tm, D_MODEL), row), pl.BlockSpec((1, D_MODEL), const2)]
    args = [h, g]
    if routed:
        n_chunks = N_EXPERTS
        in_specs.append(pl.BlockSpec((D_MODEL, LANES), const2))
        args.append(router)
        in_specs += [
            pl.BlockSpec((None, D_MODEL, D_FF_EXPERT), lambda i, j: (j, 0, 0)),
            pl.BlockSpec((None, D_MODEL, D_FF_EXPERT), lambda i, j: (j, 0, 0)),
            pl.BlockSpec((None, D_FF_EXPERT, D_MODEL), lambda i, j: (j, 0, 0)),
        ]
    else:
        n_chunks = D_FF // D_FF_EXPERT
        in_specs += [
            pl.BlockSpec((D_MODEL, D_FF_EXPERT), lambda i, j: (0, j)),
            pl.BlockSpec((D_MODEL, D_FF_EXPERT), lambda i, j: (0, j)),
            pl.BlockSpec((D_FF_EXPERT, D_MODEL), lambda i, j: (j, 0)),
        ]
    args += [wg, wu, wd]
    if final_norm:
        in_specs.append(pl.BlockSpec((1, D_MODEL), const2))
        args.append(g_final)
    scratch = [pltpu.VMEM((tm, D_MODEL), F32 if hp_experts else BF16), pltpu.VMEM((tm, D_MODEL), F32)]
    if routed:
        scratch.append(pltpu.VMEM((tm, LANES), F32))
    kern = functools.partial(_ffn_kernel, routed=routed, final_norm=final_norm, n_chunks=n_chunks,
                             hp_router=hp_router, hp_experts=hp_experts)
    return pl.pallas_call(
        kern,
        grid=(rows // tm, n_chunks),
        in_specs=in_specs,
        out_specs=pl.BlockSpec((tm, D_MODEL), row),
        out_shape=jax.ShapeDtypeStruct((rows, D_MODEL), F32),
        scratch_shapes=scratch,
        compiler_params=pltpu.CompilerParams(
            dimension_semantics=("arbitrary", "arbitrary"), vmem_limit_bytes=VMEM_LIMIT),
        name="moe_ffn" if routed else "dense_ffn",
    )(*args)


N_GROUPS = N_EXPERTS * N_EXPERTS
PAIR_IDS = tuple(a * N_EXPERTS + b for a in range(N_EXPERTS) for b in range(a + 1, N_EXPERTS))
N_PAIRS = len(PAIR_IDS)
PAIR_TM = 256
ROW_TM = 512
TILE_LANES = 256


def _route_kernel(h_ref, g_ref, router_ref, dest_ref, cnt_ref, start_ref, tile_ref,
                  nv_ref, carry, grp_s, rank_s, *, tm):
    i = pl.program_id(0)
    n_steps = pl.num_programs(0)

    @pl.when(i == 0)
    def _():
        carry[...] = jnp.zeros(carry.shape, F32)

    n = _rms(h_ref[...], g_ref[...])
    logits = _mm(n, router_ref[...])
    l8 = logits.T[0:N_EXPERTS, :]
    sub = lax.broadcasted_iota(jnp.int32, l8.shape, 0)
    m1 = jnp.max(l8, axis=0, keepdims=True)
    i1 = jnp.min(jnp.where(l8 == m1, sub, N_EXPERTS), axis=0, keepdims=True)
    rest = jnp.where(sub == i1, -jnp.inf, l8)
    m2 = jnp.max(rest, axis=0, keepdims=True)
    i2 = jnp.min(jnp.where(rest == m2, sub, N_EXPERTS), axis=0, keepdims=True)
    grp = jnp.minimum(i1, i2) * N_EXPERTS + jnp.maximum(i1, i2)
    grp_s[i] = grp

    gid = lax.broadcasted_iota(jnp.int32, (N_GROUPS, tm), 0)
    onehot = jnp.where(gid == grp, 1.0, 0.0)
    rs = lax.broadcasted_iota(jnp.int32, (tm, tm), 0)
    cs = lax.broadcasted_iota(jnp.int32, (tm, tm), 1)
    upper = jnp.where(rs <= cs, 1.0, 0.0)
    incl = _mm(onehot, upper)
    before = carry[...]
    rank = jnp.sum(onehot * (incl - 1.0 + before), axis=0, keepdims=True)
    rank_s[i] = rank.astype(jnp.int32)
    total = before + jnp.sum(onehot, axis=1, keepdims=True)
    carry[...] = total

    @pl.when(i == n_steps - 1)
    def _():
        padded = jnp.floor((total + (PAIR_TM - 1.0)) / PAIR_TM) * PAIR_TM
        gr = lax.broadcasted_iota(jnp.int32, (N_GROUPS, N_GROUPS), 0)
        gc = lax.broadcasted_iota(jnp.int32, (N_GROUPS, N_GROUPS), 1)
        lower = jnp.where(gc <= gr, 1.0, 0.0)
        ends = _mm(lower, jnp.broadcast_to(padded, (N_GROUPS, LANES)), hp=True)
        cnt_ref[...] = jnp.broadcast_to(total, cnt_ref.shape).astype(jnp.int32)
        starts = (ends - padded).astype(jnp.int32)
        start_ref[...] = starts

        group_start = [jnp.broadcast_to(starts[gid:gid + 1, 0:1], (1, tm)) for gid in PAIR_IDS]

        def to_rows(s, c):
            gs = grp_s[s]
            d = rank_s[s]
            for gid, st in zip(PAIR_IDS, group_start):
                d = d + jnp.where(gs == gid, st, 0)
            dest_ref[s] = d
            return c
        lax.fori_loop(0, n_steps, to_rows, 0)

        nv_ref[...] = (ends[N_GROUPS - 1:, :] / PAIR_TM).astype(jnp.int32)
        tile_start = lax.broadcasted_iota(jnp.int32, (N_GROUPS, TILE_LANES), 1).astype(F32) * PAIR_TM
        tile_grp = jnp.sum(jnp.where(ends[:, 0:1] <= tile_start, 1.0, 0.0), axis=0, keepdims=True)
        gidf = lax.broadcasted_iota(jnp.int32, (N_GROUPS, 1), 0).astype(F32)
        last = jnp.max(jnp.where(total > 0.0, gidf, 0.0), axis=0, keepdims=True)
        tile_ref[...] = jnp.minimum(tile_grp, last).astype(jnp.int32)


def _route(h, g, router, tm):
    rows = h.shape[0]
    n_steps = rows // tm
    const2 = lambda i: (0, 0)
    return pl.pallas_call(
        functools.partial(_route_kernel, tm=tm),
        grid=(n_steps,),
        in_specs=[
            pl.BlockSpec((tm, D_MODEL), lambda i: (i, 0)),
            pl.BlockSpec((1, D_MODEL), const2),
            pl.BlockSpec((D_MODEL, LANES), const2),
        ],
        out_specs=[
            pl.BlockSpec((n_steps, 1, tm), lambda i: (0, 0, 0)),
            pl.BlockSpec((N_GROUPS, LANES), const2),
            pl.BlockSpec((N_GROUPS, LANES), const2),
            pl.BlockSpec((1, TILE_LANES), const2),
            pl.BlockSpec((1, LANES), const2),
        ],
        out_shape=[
            jax.ShapeDtypeStruct((n_steps, 1, tm), jnp.int32),
            jax.ShapeDtypeStruct((N_GROUPS, LANES), jnp.int32),
            jax.ShapeDtypeStruct((N_GROUPS, LANES), jnp.int32),
            jax.ShapeDtypeStruct((1, TILE_LANES), jnp.int32),
            jax.ShapeDtypeStruct((1, LANES), jnp.int32),
        ],
        scratch_shapes=[
            pltpu.VMEM((N_GROUPS, 1), F32),
            pltpu.VMEM((n_steps, 1, tm), jnp.int32),
            pltpu.VMEM((n_steps, 1, tm), jnp.int32),
        ],
        compiler_params=pltpu.CompilerParams(
            dimension_semantics=("arbitrary",), vmem_limit_bytes=VMEM_LIMIT),
        name="route",
    )(h, g, router)


def _dispatch_kernel(start_ref, cnt_ref, nv_ref, dest_ref, h_ref, xs_hbm,
                     stage, zbuf, sem, fsem, *, tm, n_tiles):
    i = pl.program_id(0)
    last = pl.num_programs(0) - 1
    slot = i % 3

    def wait_rows(s):
        pltpu.make_async_copy(stage.at[s], xs_hbm.at[pl.ds(0, tm)], sem.at[s]).wait()

    def fill_ops(op):
        def tail(j, c):
            op(pltpu.make_async_copy(zbuf, xs_hbm.at[pl.ds(j * PAIR_TM, PAIR_TM)], fsem))
            return c
        lax.fori_loop(nv_ref[0], n_tiles, tail, 0)
        for gid in PAIR_IDS:
            lo = start_ref[gid] + cnt_ref[gid]
            hi = start_ref[gid] + (cnt_ref[gid] + PAIR_TM - 1) // PAIR_TM * PAIR_TM

            def row(r, c):
                op(pltpu.make_async_copy(zbuf.at[pl.ds(0, 1)], xs_hbm.at[pl.ds(r, 1)], fsem))
                return c
            lax.fori_loop(lo, hi, row, 0)

    @pl.when(i == 0)
    def _():
        zbuf[...] = jnp.zeros(zbuf.shape, F32)
        fill_ops(lambda c: c.start())

    @pl.when(i >= 3)
    def _():
        wait_rows(slot)

    stage[slot] = h_ref[...]
    for r in range(tm):
        pltpu.make_async_copy(stage.at[slot, pl.ds(r, 1)], xs_hbm.at[pl.ds(dest_ref[0, 0, r], 1)],
                              sem.at[slot]).start()

    @pl.when(i == last)
    def _():
        for s in range(3):
            wait_rows(s)
        fill_ops(lambda c: c.wait())


def _dispatch(h, start, cnt, nv, dest, n_tiles):
    rows = h.shape[0]
    tm = ROW_TM
    n_steps = rows // tm
    assert n_steps >= 3
    return pl.pallas_call(
        functools.partial(_dispatch_kernel, tm=tm, n_tiles=n_tiles),
        grid_spec=pltpu.PrefetchScalarGridSpec(
            num_scalar_prefetch=3,
            grid=(n_steps,),
            in_specs=[
                pl.BlockSpec((1, 1, tm), lambda i, s, c, n: (i, 0, 0), memory_space=pltpu.SMEM),
                pl.BlockSpec((tm, D_MODEL), lambda i, s, c, n: (i, 0)),
            ],
            out_specs=pl.BlockSpec(memory_space=pl.ANY),
            scratch_shapes=[
                pltpu.VMEM((3, tm, D_MODEL), F32),
                pltpu.VMEM((PAIR_TM, D_MODEL), F32),
                pltpu.SemaphoreType.DMA((3,)),
                pltpu.SemaphoreType.DMA,
            ],
        ),
        out_shape=jax.ShapeDtypeStruct((n_tiles * PAIR_TM, D_MODEL), F32),
        compiler_params=pltpu.CompilerParams(
            dimension_semantics=("arbitrary",), vmem_limit_bytes=VMEM_LIMIT),
        name="dispatch_rows",
    )(start, cnt, nv, dest, h)


def _collect_kernel(dest_ref, ys_hbm, out_ref, stage, sem, *, tm):
    i = pl.program_id(0)
    n_steps = pl.num_programs(0) - 1

    @pl.when(i < n_steps)
    def _():
        slot = i % 2
        for r in range(tm):
            pltpu.make_async_copy(ys_hbm.at[pl.ds(dest_ref[0, 0, r], 1)],
                                  stage.at[slot, pl.ds(r, 1)], sem.at[slot]).start()

    @pl.when(i >= 1)
    def _():
        slot = (i - 1) % 2
        pltpu.make_async_copy(ys_hbm.at[pl.ds(0, tm)], stage.at[slot], sem.at[slot]).wait()
        out_ref[...] = stage[slot]


def _collect(ys, dest):
    n_steps, _, tm = dest.shape
    return pl.pallas_call(
        functools.partial(_collect_kernel, tm=tm),
        grid=(n_steps + 1,),
        in_specs=[
            pl.BlockSpec((1, 1, tm), lambda i: (jnp.minimum(i, n_steps - 1), 0, 0),
                         memory_space=pltpu.SMEM),
            pl.BlockSpec(memory_space=pl.ANY),
        ],
        out_specs=pl.BlockSpec((tm, D_MODEL), lambda i: (jnp.maximum(i - 1, 0), 0)),
        out_shape=jax.ShapeDtypeStruct((n_steps * tm, D_MODEL), F32),
        scratch_shapes=[
            pltpu.VMEM((2, tm, D_MODEL), F32),
            pltpu.SemaphoreType.DMA((2,)),
        ],
        compiler_params=pltpu.CompilerParams(
            dimension_semantics=("arbitrary",), vmem_limit_bytes=VMEM_LIMIT),
        name="collect_rows",
    )(dest, ys)


def _pair_ffn_kernel(tg_ref, nv_ref, x_ref, g_ref, router_ref,
                     wga_ref, wua_ref, wda_ref, wgb_ref, wub_ref, wdb_ref, *rest, final_norm):
    if final_norm:
        gf_ref, out_ref = rest
    else:
        (out_ref,) = rest
    i = pl.program_id(0)

    @pl.when(i < nv_ref[0])
    def _():
        x = x_ref[...]
        nb16 = _rms(x, g_ref[...]).astype(BF16)
        logits = _mm(nb16, router_ref[...])
        lane = lax.broadcasted_iota(jnp.int32, logits.shape, 1)
        ea_id = tg_ref[i] // N_EXPERTS
        eb_id = tg_ref[i] % N_EXPERTS
        la = jnp.sum(jnp.where(lane == ea_id, logits, 0.0), axis=-1, keepdims=True)
        lb = jnp.sum(jnp.where(lane == eb_id, logits, 0.0), axis=-1, keepdims=True)
        m = jnp.maximum(la, lb)
        ea = jnp.exp(la - m)
        eb = jnp.exp(lb - m)
        den = ea + eb

        def expert(wg_ref, wu_ref, wd_ref):
            gate = _mm(nb16, wg_ref[...])
            up = _mm(nb16, wu_ref[...])
            act = (gate * _sigmoid(gate)) * up
            return _mm(act, wd_ref[...])

        y = (ea / den) * expert(wga_ref, wua_ref, wda_ref)
        y = y + (eb / den) * expert(wgb_ref, wub_ref, wdb_ref)
        res = x + y
        if final_norm:
            res = _rms(res, gf_ref[...])
        out_ref[...] = res

    @pl.when(i >= nv_ref[0])
    def _():
        out_ref[...] = jnp.zeros(out_ref.shape, F32)


def _pair_ffn(xs, tile_grp, nvalid, g, router, wg, wu, wd, g_final=None):
    tm = PAIR_TM
    n_tiles = xs.shape[0] // tm
    final_norm = g_final is not None
    const2 = lambda i, tg, nv: (0, 0)
    wa = lambda i, tg, nv: (tg[i] // N_EXPERTS, 0, 0)
    wb = lambda i, tg, nv: (tg[i] % N_EXPERTS, 0, 0)
    in_specs = [
        pl.BlockSpec((tm, D_MODEL), lambda i, tg, nv: (jnp.minimum(i, nv[0] - 1), 0)),
        pl.BlockSpec((1, D_MODEL), const2),
        pl.BlockSpec((D_MODEL, LANES), const2),
        pl.BlockSpec((None, D_MODEL, D_FF_EXPERT), wa),
        pl.BlockSpec((None, D_MODEL, D_FF_EXPERT), wa),
        pl.BlockSpec((None, D_FF_EXPERT, D_MODEL), wa),
        pl.BlockSpec((None, D_MODEL, D_FF_EXPERT), wb),
        pl.BlockSpec((None, D_MODEL, D_FF_EXPERT), wb),
        pl.BlockSpec((None, D_FF_EXPERT, D_MODEL), wb),
    ]
    args = [xs, g, router, wg, wu, wd, wg, wu, wd]
    if final_norm:
        in_specs.append(pl.BlockSpec((1, D_MODEL), const2))
        args.append(g_final)
    return pl.pallas_call(
        functools.partial(_pair_ffn_kernel, final_norm=final_norm),
        grid_spec=pltpu.PrefetchScalarGridSpec(
            num_scalar_prefetch=2,
            grid=(n_tiles,),
            in_specs=in_specs,
            out_specs=pl.BlockSpec((tm, D_MODEL), lambda i, tg, nv: (i, 0)),
        ),
        out_shape=jax.ShapeDtypeStruct((n_tiles * tm, D_MODEL), F32),
        compiler_params=pltpu.CompilerParams(
            dimension_semantics=("arbitrary",), vmem_limit_bytes=VMEM_LIMIT),
        name="pair_ffn",
    )(tile_grp, nvalid, *args)


def _sparse_moe(h, g, router, wg, wu, wd, g_final=None):
    rows = h.shape[0]
    n_tiles = rows // PAIR_TM + N_PAIRS
    assert n_tiles <= TILE_LANES
    dest, cnt, start, tile_grp, nv = _route(h, g, router, ROW_TM)
    start, cnt, nv = start[:, 0], cnt[:, 0], nv[0, :1]
    xs = _dispatch(h, start, cnt, nv, dest, n_tiles)
    ys = _pair_ffn(xs, tile_grp[0, :n_tiles], nv, g, router, wg, wu, wd, g_final=g_final)
    return _collect(ys, dest)


def kernel(x_prompt, x_sample, state_conv, state_pool, cache_swa_k, cache_swa_v, norm_mix, norm_ffn, norm_final, even_w_in, conv_w, conv_b, conv_ln_g, conv_ln_b, pool_w, pool_scale, even_w_out, ffn_w_gate, ffn_w_up, ffn_w_down, odd_w_in, sgu_ln_g, sgu_ln_b, sgu_w, sgu_b, attn_sinks, odd_w_out, router_w, exp_w_gate, exp_w_up, exp_w_down):
    bp, tp, _ = x_prompt.shape
    bs, ts, _ = x_sample.shape
    depth = norm_mix.shape[0]
    tm_p = 512
    kvw = N_KV_HEADS * HEAD_DIM

    h_p = x_prompt.reshape(bp * tp, D_MODEL)
    h_s = x_sample.reshape(bs * ts, D_MODEL)
    b16 = lambda w: w.astype(BF16)
    r2 = lambda v: v.reshape(1, -1)
    pad_router = lambda r: jnp.pad(r, ((0, 0), (0, LANES - N_EXPERTS)))

    conv_p, conv_s, pool_p, pool_s = [], [], [], []
    k_p, v_p, k_s, v_s, sgu_s = [], [], [], [], []
    for layer in range(depth):
        i = layer // 2
        last = layer == depth - 1
        gf = r2(norm_final) if last else None
        if layer % 2 == 0:
            small = (conv_w[i], r2(conv_b[i]), r2(conv_ln_g[i]), r2(conv_ln_b[i]))
            gn = r2(norm_ffn[layer])
            fused = not last
            if fused:
                h_p, cst, pst = _even_layer(
                    h_p, bp, tp, r2(norm_mix[layer]), b16(even_w_in[i]), *small, b16(pool_w[i]),
                    r2(pool_scale[i]), b16(even_w_out[i]), gn, b16(ffn_w_gate[i]),
                    b16(ffn_w_up[i]), b16(ffn_w_down[i]), EVEN_TM)
            else:
                h_p, cst, pst = _even_mixer(
                    h_p, bp, tp, 0, r2(norm_mix[layer]), b16(even_w_in[i]), *small, b16(pool_w[i]),
                    r2(pool_scale[i]), b16(even_w_out[i]),
                    jnp.zeros((bp, CONV_HIST, D_CONV), F32), jnp.zeros((bp, POOL_HIST, D_POOL), F32),
                    tm_p, nseq=1, hp=False)
            conv_p.append(cst)
            pool_p.append(pst)
            h_s, cst, pst = _even_mixer(
                h_s, bs, ts, PAST_LEN, r2(norm_mix[layer]), even_w_in[i], *small, pool_w[i],
                r2(pool_scale[i]), even_w_out[i], state_conv[i], state_pool[i],
                ts, nseq=bs, hp=True)
            conv_s.append(cst)
            pool_s.append(pst)
            if not fused:
                h_p = _ffn(h_p, gn, b16(ffn_w_gate[i]), b16(ffn_w_up[i]), b16(ffn_w_down[i]), tm_p,
                           g_final=gf)
            h_s = _ffn(h_s, gn, ffn_w_gate[i], ffn_w_up[i], ffn_w_down[i], bs * ts, g_final=gf,
                       hp_experts=True)
        else:
            bias = jnp.repeat(sgu_b[i].T, SGU_HEAD, axis=1)
            small = (r2(sgu_ln_g[i]), r2(sgu_ln_b[i]), sgu_w[i], bias)
            zkv = jnp.zeros((bp, WINDOW, kvw), F32)
            h_p, kn, vn = _odd_mixer(
                h_p, bp, tp, attn_sinks[i], r2(norm_mix[layer]), b16(odd_w_in[i]), *small,
                b16(odd_w_out[i]), zkv, zkv, tm_p, CHUNK, True, False, nseq=1, hp=False)
            k_p.append(kn.reshape(bp, -1, N_KV_HEADS, HEAD_DIM))
            v_p.append(vn.reshape(bp, -1, N_KV_HEADS, HEAD_DIM))
            h_s, kn, vn, sv = _odd_mixer(
                h_s, bs, ts, attn_sinks[i], r2(norm_mix[layer]), odd_w_in[i], *small, odd_w_out[i],
                cache_swa_k[i].reshape(bs, WINDOW, kvw), cache_swa_v[i].reshape(bs, WINDOW, kvw),
                ts, ts, False, True, nseq=bs, hp=True)
            k_s.append(kn.reshape(bs, ts, N_KV_HEADS, HEAD_DIM))
            v_s.append(vn.reshape(bs, ts, N_KV_HEADS, HEAD_DIM))
            sgu_s.append(sv.reshape(bs, ts, D_SGU))
            gn = r2(norm_ffn[layer])
            wg, wu, wd = b16(exp_w_gate[i]), b16(exp_w_up[i]), b16(exp_w_down[i])
            h_p = _sparse_moe(h_p, gn, pad_router(b16(router_w[i])), wg, wu, wd, g_final=gf)
            if last:
                h_s = _ffn(h_s, gn, wg, wu, wd, bs * ts, router=pad_router(router_w[i]),
                           g_final=gf, hp_router=True)
            else:
                h_s = _ffn(h_s, gn, exp_w_gate[i], exp_w_up[i], exp_w_down[i], bs * ts,
                           router=pad_router(router_w[i]), g_final=gf, hp_router=True,
                           hp_experts=True)

    return (h_p.reshape(bp, tp, D_MODEL), h_s.reshape(bs, ts, D_MODEL),
            jnp.stack(conv_p), jnp.stack(conv_s),
            jnp.stack(pool_p), jnp.stack(pool_s),
            jnp.stack(k_p), jnp.stack(v_p),
            jnp.stack(k_s), jnp.stack(v_s),
            jnp.stack(sgu_s))
```

```python
import functools

import jax
import jax.numpy as jnp
from jax import lax
from jax.experimental import pallas as pl
from jax.experimental.pallas import tpu as pltpu

F32 = jnp.float32
BF16 = jnp.bfloat16

D_MODEL = 1024
PAST_LEN = 1024
CHUNK = 64
D_CONV = 512
CONV_WIDTH = 31
CONV_HIST = CONV_WIDTH - 1
D_POOL = 512
POOL_WINDOWS = (2, 4, 8, 16)
POOL_GROUP = 128
POOL_HIST = 15
D_SGU = 512
SGU_CHUNK = 128
N_SGU_HEADS = 4
SGU_HEAD = 128
HEAD_DIM = 64
N_Q_HEADS = 8
N_KV_HEADS = 2
Q_PER_KV = 4
WINDOW = 128
D_FF = 2816
N_EXPERTS = 8
D_FF_EXPERT = 1408
EPS = 1e-6

LANES = 128
SUBLANES = 8
A_PAD = 32
P_PAD = 16
VMEM_LIMIT = 56 * 1024 * 1024
EVEN_TM = 512


def _sigmoid(x):
    return 1.0 / (1.0 + jnp.exp(-x))


def _rms(x, g):
    ms = jnp.mean(x * x, axis=-1, keepdims=True)
    return x * lax.rsqrt(ms + EPS) * g


def _layer_norm(x, g, b):
    mu = jnp.mean(x, axis=-1, keepdims=True)
    xc = x - mu
    var = jnp.mean(xc * xc, axis=-1, keepdims=True)
    return xc * lax.rsqrt(var + EPS) * g + b


def _mm(x, w, hp=False):
    if hp:
        return jnp.dot(x, w, preferred_element_type=F32, precision=lax.Precision.HIGHEST)
    return jnp.dot(x.astype(BF16), w.astype(BF16), preferred_element_type=F32)


def _mm_t(x, y, hp=False):
    dims = (((1,), (1,)), ((), ()))
    if hp:
        return lax.dot_general(x, y, dims, preferred_element_type=F32,
                               precision=lax.Precision.HIGHEST)
    return lax.dot_general(x.astype(BF16), y.astype(BF16), dims, preferred_element_type=F32)


def _even_load_history(abuf, pbuf, chist_ref, phist_ref, nseq):
    for s in range(nseq):
        abuf[s, 0:A_PAD - CONV_HIST, :] = jnp.zeros((A_PAD - CONV_HIST, D_CONV), F32)
        abuf[s, A_PAD - CONV_HIST:A_PAD, :] = chist_ref[s]
        pbuf[s, 0:P_PAD - POOL_HIST, :] = jnp.zeros((P_PAD - POOL_HIST, D_POOL), F32)
        pbuf[s, P_PAD - POOL_HIST:P_PAD, :] = phist_ref[s]


def _even_carry_state(abuf, pbuf, cstate_ref, pstate_ref, nseq, tm):
    for s in range(nseq):
        atail = abuf[s, tm:tm + A_PAD, :]
        ptail = pbuf[s, tm:tm + P_PAD, :]
        abuf[s, 0:A_PAD, :] = atail
        pbuf[s, 0:P_PAD, :] = ptail
        cstate_ref[s] = atail[A_PAD - CONV_HIST:, :]
        pstate_ref[s] = ptail[P_PAD - POOL_HIST:, :]


def _even_mix(h, t, g_ref, win_ref, cw_ref, cb_ref, lng_ref, lnb_ref, pw_ref, ps_ref, wout_ref,
              abuf, pbuf, ash, cat, *, nseq, tm, rc, start_pos, hp, between=None):
    n = _rms(h, g_ref[...])
    proj = _mm(n, win_ref[...], hp)
    a = proj[:, :D_CONV] * _sigmoid(proj[:, D_CONV:2 * D_CONV])
    for s in range(nseq):
        abuf[s, A_PAD:A_PAD + tm, :] = a[s * tm:(s + 1) * tm]
        pbuf[s, P_PAD:P_PAD + tm, :] = proj[s * tm:(s + 1) * tm, 2 * D_CONV:]
    if between is not None:
        between()

    off = A_PAD - CONV_HIST
    n_sh = A_PAD + tm - SUBLANES
    for s in range(nseq):
        for j in range(1, SUBLANES):
            for c0 in range(0, n_sh, rc):
                nr = min(rc, n_sh - c0)
                ash[j - 1, c0:c0 + nr, :] = abuf[s, c0 + j:c0 + j + nr, :]

        def tap(k, r0, ls):
            q, j = divmod(off + k, SUBLANES)
            lo = r0 + q * SUBLANES
            if j == 0:
                return abuf[s, lo:lo + rc, ls]
            return ash[j - 1, lo:lo + rc, ls]

        for r0 in range(0, tm, rc):
            o0 = s * tm + r0
            cparts = []
            for lb in range(D_CONV // LANES):
                ls = slice(lb * LANES, (lb + 1) * LANES)
                acc = tap(0, r0, ls) * cw_ref[0:1, ls]
                for k in range(1, CONV_WIDTH):
                    acc = acc + tap(k, r0, ls) * cw_ref[k:k + 1, ls]
                cparts.append(acc)
            c = jnp.concatenate(cparts, axis=-1) + cb_ref[...]
            c = _layer_norm(c, lng_ref[...], lnb_ref[...])
            c = c * _sigmoid(c)
            cat[o0:o0 + rc, 0:D_CONV] = c.astype(cat.dtype)

            pos1 = (start_pos + 1 + t * tm + r0
                    + lax.broadcasted_iota(jnp.int32, (rc, 1), 0)).astype(F32)
            for gi, w in enumerate(POOL_WINDOWS):
                ls = slice(gi * POOL_GROUP, (gi + 1) * POOL_GROUP)
                cur = pbuf[s, P_PAD + r0:P_PAD + r0 + rc, ls]
                acc = cur
                for i in range(1, w):
                    acc = acc + pbuf[s, P_PAD + r0 - i:P_PAD + r0 - i + rc, ls]
                pooled = acc / jnp.minimum(float(w), pos1) - cur
                mixed = _mm(pooled, pw_ref[gi], hp)
                cat[o0:o0 + rc, D_CONV + gi * POOL_GROUP:D_CONV + (gi + 1) * POOL_GROUP] = (
                    mixed * ps_ref[0:1, ls]).astype(cat.dtype)

    return h + _mm(cat[...], wout_ref[...], hp)


def _even_mixer_kernel(h_ref, g_ref, win_ref, cw_ref, cb_ref, lng_ref, lnb_ref, pw_ref, ps_ref,
                       wout_ref, chist_ref, phist_ref,
                       out_ref, cstate_ref, pstate_ref,
                       abuf, pbuf, ash, cat, *, nseq, tm, rc, start_pos, hp):
    t = pl.program_id(1)

    @pl.when(t == 0)
    def _():
        _even_load_history(abuf, pbuf, chist_ref, phist_ref, nseq)

    out_ref[...] = _even_mix(h_ref[...], t, g_ref, win_ref, cw_ref, cb_ref, lng_ref, lnb_ref, pw_ref,
                             ps_ref, wout_ref, abuf, pbuf, ash, cat,
                             nseq=nseq, tm=tm, rc=rc, start_pos=start_pos, hp=hp)
    _even_carry_state(abuf, pbuf, cstate_ref, pstate_ref, nseq, tm)


def _even_layer_kernel(h_ref, g_ref, win_ref, cw_ref, cb_ref, lng_ref, lnb_ref, pw_ref, ps_ref,
                       wout_ref, chist_ref, phist_ref, gff_ref, wg_ref, wu_ref, wd_ref,
                       out_ref, cstate_ref, pstate_ref,
                       abuf, pbuf, ash, cat, hbuf, hres, n_scr, *, tm, rc, nt, n_total):
    s = pl.program_id(0)
    real = s < n_total
    t = jnp.minimum(s, n_total - 1) % nt

    @pl.when(s == 0)
    def _():
        hbuf[...] = jnp.zeros((tm, D_MODEL), F32)

    @pl.when(real & (t == 0))
    def _():
        _even_load_history(abuf, pbuf, chist_ref, phist_ref, 1)

    h_prev = hbuf[...]
    hres[...] = h_prev
    n_scr[...] = _rms(h_prev, gff_ref[...]).astype(BF16)

    def swiglu_prev():
        x = n_scr[...]
        y = None
        for c0 in range(0, D_FF, D_FF_EXPERT):
            cols = slice(c0, c0 + D_FF_EXPERT)
            gate = _mm(x, wg_ref[:, cols])
            up = _mm(x, wu_ref[:, cols])
            part = _mm((gate * _sigmoid(gate)) * up, wd_ref[cols, :])
            y = part if y is None else y + part
        out_ref[...] = hres[...] + y

    hbuf[...] = _even_mix(h_ref[...], t, g_ref, win_ref, cw_ref, cb_ref, lng_ref, lnb_ref, pw_ref,
                          ps_ref, wout_ref, abuf, pbuf, ash, cat,
                          nseq=1, tm=tm, rc=rc, start_pos=0, hp=False, between=swiglu_prev)

    @pl.when(real)
    def _():
        _even_carry_state(abuf, pbuf, cstate_ref, pstate_ref, 1, tm)


def _even_layer(h, nb, tlen, g, w_in, cw, cb, lng, lnb, pw, ps, w_out, gff, wg, wu, wd, tm):
    rc = min(64, tm)
    nt = tlen // tm
    n_total = nb * nt
    cl = lambda s: jnp.minimum(s, n_total - 1)
    once = pl.Buffered(1)
    const2 = lambda s: (0, 0)
    const3 = lambda s: (0, 0, 0)
    per_b = lambda s: (cl(s) // nt, 0, 0)
    full = lambda shape: pl.BlockSpec(shape, const2 if len(shape) == 2 else const3,
                                      pipeline_mode=once)
    chist = jnp.zeros((nb, CONV_HIST, D_CONV), F32)
    phist = jnp.zeros((nb, POOL_HIST, D_POOL), F32)
    kern = functools.partial(_even_layer_kernel, tm=tm, rc=rc, nt=nt, n_total=n_total)
    return pl.pallas_call(
        kern,
        grid=(n_total + 1,),
        in_specs=[
            pl.BlockSpec((tm, D_MODEL), lambda s: (cl(s), 0)),
            full((1, D_MODEL)),
            full((D_MODEL, 3 * D_CONV)),
            full((CONV_WIDTH, D_CONV)),
            full((1, D_CONV)),
            full((1, D_CONV)),
            full((1, D_CONV)),
            full((len(POOL_WINDOWS), POOL_GROUP, POOL_GROUP)),
            full((1, D_POOL)),
            full((D_MODEL, D_MODEL)),
            pl.BlockSpec((1, CONV_HIST, D_CONV), per_b),
            pl.BlockSpec((1, POOL_HIST, D_POOL), per_b),
            full((1, D_MODEL)),
            full((D_MODEL, D_FF)),
            full((D_MODEL, D_FF)),
            full((D_FF, D_MODEL)),
        ],
        out_specs=[
            pl.BlockSpec((tm, D_MODEL), lambda s: (jnp.maximum(s - 1, 0), 0)),
            pl.BlockSpec((1, CONV_HIST, D_CONV), per_b),
            pl.BlockSpec((1, POOL_HIST, D_POOL), per_b),
        ],
        out_shape=[
            jax.ShapeDtypeStruct((nb * tlen, D_MODEL), F32),
            jax.ShapeDtypeStruct((nb, CONV_HIST, D_CONV), F32),
            jax.ShapeDtypeStruct((nb, POOL_HIST, D_POOL), F32),
        ],
        scratch_shapes=[
            pltpu.VMEM((1, A_PAD + tm, D_CONV), F32),
            pltpu.VMEM((1, P_PAD + tm, D_POOL), F32),
            pltpu.VMEM((SUBLANES - 1, A_PAD + tm - SUBLANES, D_CONV), F32),
            pltpu.VMEM((tm, D_MODEL), BF16),
            pltpu.VMEM((tm, D_MODEL), F32),
            pltpu.VMEM((tm, D_MODEL), F32),
            pltpu.VMEM((tm, D_MODEL), BF16),
        ],
        compiler_params=pltpu.CompilerParams(
            dimension_semantics=("arbitrary",), vmem_limit_bytes=VMEM_LIMIT),
        name="even_layer",
    )(h, g, w_in, cw, cb, lng, lnb, pw, ps, w_out, chist, phist, gff, wg, wu, wd)


def _even_mixer(h, nb, tlen, start_pos, g, w_in, cw, cb, lng, lnb, pw, ps, w_out, chist, phist, tm,
                nseq, hp):
    rc = min(64, tm)
    nt = tlen // tm
    rows = nseq * tm
    row = lambda b, t: (b * nt + t, 0)
    const2 = lambda b, t: (0, 0)
    const3 = lambda b, t: (0, 0, 0)
    per_b = lambda b, t: (b, 0, 0)
    kern = functools.partial(_even_mixer_kernel, nseq=nseq, tm=tm, rc=rc, start_pos=start_pos, hp=hp)
    return pl.pallas_call(
        kern,
        grid=(nb // nseq, nt),
        in_specs=[
            pl.BlockSpec((rows, D_MODEL), row),
            pl.BlockSpec((1, D_MODEL), const2),
            pl.BlockSpec((D_MODEL, 3 * D_CONV), const2),
            pl.BlockSpec((CONV_WIDTH, D_CONV), const2),
            pl.BlockSpec((1, D_CONV), const2),
            pl.BlockSpec((1, D_CONV), const2),
            pl.BlockSpec((1, D_CONV), const2),
            pl.BlockSpec((len(POOL_WINDOWS), POOL_GROUP, POOL_GROUP), const3),
            pl.BlockSpec((1, D_POOL), const2),
            pl.BlockSpec((D_MODEL, D_MODEL), const2),
            pl.BlockSpec((nseq, CONV_HIST, D_CONV), per_b),
            pl.BlockSpec((nseq, POOL_HIST, D_POOL), per_b),
        ],
        out_specs=[
            pl.BlockSpec((rows, D_MODEL), row),
            pl.BlockSpec((nseq, CONV_HIST, D_CONV), per_b),
            pl.BlockSpec((nseq, POOL_HIST, D_POOL), per_b),
        ],
        out_shape=[
            jax.ShapeDtypeStruct((nb * tlen, D_MODEL), F32),
            jax.ShapeDtypeStruct((nb, CONV_HIST, D_CONV), F32),
            jax.ShapeDtypeStruct((nb, POOL_HIST, D_POOL), F32),
        ],
        scratch_shapes=[
            pltpu.VMEM((nseq, A_PAD + tm, D_CONV), F32),
            pltpu.VMEM((nseq, P_PAD + tm, D_POOL), F32),
            pltpu.VMEM((SUBLANES - 1, A_PAD + tm - SUBLANES, D_CONV), F32),
            pltpu.VMEM((rows, D_MODEL), F32 if hp else BF16),
        ],
        compiler_params=pltpu.CompilerParams(
            dimension_semantics=("arbitrary", "arbitrary"), vmem_limit_bytes=VMEM_LIMIT),
        name="even_mixer",
    )(h, g, w_in, cw, cb, lng, lnb, pw, ps, w_out, chist, phist)


def _odd_mixer_kernel(sink_ref, h_ref, g_ref, win_ref, lng_ref, lnb_ref, ws_ref, bs_ref, wout_ref,
                      kc_ref, vc_ref, *rest, nseq, tm, cq, kv_rows, mask_first, emit_sgu_v, hp):
    if emit_sgu_v:
        out_ref, knew_ref, vnew_ref, sguv_ref, kbuf, vbuf, vs_scr, attn, cat = rest
    else:
        out_ref, knew_ref, vnew_ref, kbuf, vbuf, vs_scr, attn, cat = rest
    t = pl.program_id(1)
    op_dtype = kbuf.dtype

    @pl.when(t == 0)
    def _():
        for s in range(nseq):
            for gi in range(N_KV_HEADS):
                hs = slice(gi * HEAD_DIM, (gi + 1) * HEAD_DIM)
                kbuf[s * N_KV_HEADS + gi, 0:WINDOW, :] = kc_ref[s, :, hs].astype(op_dtype)
                vbuf[s * N_KV_HEADS + gi, 0:WINDOW, :] = vc_ref[s, :, hs].astype(op_dtype)

    h = h_ref[...]
    n = _rms(h, g_ref[...])
    proj = _mm(n, win_ref[...], hp)
    o0 = 2 * D_SGU
    o1 = o0 + N_Q_HEADS * HEAD_DIM
    o2 = o1 + N_KV_HEADS * HEAD_DIM
    zp = proj[:, :o0]
    z = 0.5 * zp * (1.0 + lax.erf(zp * (0.5 ** 0.5)))
    u = z[:, :D_SGU]
    v = _layer_norm(z[:, D_SGU:], lng_ref[...], lnb_ref[...])
    if emit_sgu_v:
        sguv_ref[...] = v

    n_sgu = -(-tm // SGU_CHUNK)
    if tm % SGU_CHUNK:
        vs_scr[...] = jnp.zeros(vs_scr.shape, vs_scr.dtype)
    for s in range(nseq):
        v0 = s * n_sgu * SGU_CHUNK
        vs_scr[v0:v0 + tm, :] = v[s * tm:(s + 1) * tm].astype(vs_scr.dtype)
    ri = lax.broadcasted_iota(jnp.int32, (SGU_CHUNK, SGU_CHUNK), 0)
    ci = lax.broadcasted_iota(jnp.int32, (SGU_CHUNK, SGU_CHUNK), 1)
    for gi in range(N_SGU_HEADS):
        ls = slice(gi * SGU_HEAD, (gi + 1) * SGU_HEAD)
        wsg = jnp.where(ri >= ci, ws_ref[gi], 0.0).astype(vs_scr.dtype)
        for s in range(nseq):
            for c in range(n_sgu):
                rows = min(SGU_CHUNK, tm - c * SGU_CHUNK)
                v0 = (s * n_sgu + c) * SGU_CHUNK
                mixed = _mm(wsg, vs_scr[v0:v0 + SGU_CHUNK, ls], hp) + bs_ref[:, ls]
                r0 = s * tm + c * SGU_CHUNK
                cat[r0:r0 + rows, ls] = (u[r0:r0 + rows, ls] * mixed[0:rows]).astype(cat.dtype)

    k = proj[:, o1:o2]
    vv = proj[:, o2:]
    q = proj[:, o0:o1].astype(op_dtype)
    nk = WINDOW + cq
    qrow = lax.broadcasted_iota(jnp.int32, (Q_PER_KV * cq, 1), 0)
    for s in range(nseq):
        knew_ref[s] = k[(s + 1) * tm - kv_rows:(s + 1) * tm, :]
        vnew_ref[s] = vv[(s + 1) * tm - kv_rows:(s + 1) * tm, :]
        for gi in range(N_KV_HEADS):
            hs = slice(gi * HEAD_DIM, (gi + 1) * HEAD_DIM)
            kbuf[s * N_KV_HEADS + gi, WINDOW:WINDOW + tm, :] = k[s * tm:(s + 1) * tm, hs].astype(op_dtype)
            vbuf[s * N_KV_HEADS + gi, WINDOW:WINDOW + tm, :] = vv[s * tm:(s + 1) * tm, hs].astype(op_dtype)
    for gi in range(N_KV_HEADS):
        sk = jnp.zeros((Q_PER_KV * cq, 1), F32)
        for i in range(Q_PER_KV):
            sk = jnp.where((qrow >= i * cq) & (qrow < (i + 1) * cq), sink_ref[gi * Q_PER_KV + i], sk)
        for s in range(nseq):
            kv = s * N_KV_HEADS + gi
            for c in range(tm // cq):
                r0 = c * cq
                g0 = s * tm + r0
                qs = jnp.concatenate(
                    [q[g0:g0 + cq, (gi * Q_PER_KV + i) * HEAD_DIM:(gi * Q_PER_KV + i + 1) * HEAD_DIM]
                     for i in range(Q_PER_KV)], axis=0)
                sc = _mm_t(qs, kbuf[kv, r0:r0 + nk, :], hp) * (HEAD_DIM ** -0.5)
                if mask_first and r0 < WINDOW:
                    key_pos = t * tm + (r0 - WINDOW) + lax.broadcasted_iota(jnp.int32, (1, nk), 1)
                    sc = jnp.where(key_pos >= 0, sc, -jnp.inf)
                m = jnp.maximum(jnp.max(sc, axis=-1, keepdims=True), sk)
                e = jnp.exp(sc - m)
                p = e / (jnp.sum(e, axis=-1, keepdims=True) + jnp.exp(sk - m))
                o = _mm(p, vbuf[kv, r0:r0 + nk, :], hp)
                for i in range(Q_PER_KV):
                    hd = gi * Q_PER_KV + i
                    attn[g0:g0 + cq, hd * HEAD_DIM:(hd + 1) * HEAD_DIM] = o[i * cq:(i + 1) * cq, :]
    cat[:, D_SGU:] = attn[...].astype(cat.dtype)

    out_ref[...] = h + _mm(cat[...], wout_ref[...], hp)

    for kv in range(nseq * N_KV_HEADS):
        ktail = kbuf[kv, tm:tm + WINDOW, :]
        vtail = vbuf[kv, tm:tm + WINDOW, :]
        kbuf[kv, 0:WINDOW, :] = ktail
        vbuf[kv, 0:WINDOW, :] = vtail


def _odd_mixer(h, nb, tlen, sinks, g, w_in, lng, lnb, ws, bs, w_out, kc, vc, tm, cq, mask_first,
               emit_sgu_v, nseq, hp):
    nt = tlen // tm
    rows = nseq * tm
    kv_rows = min(WINDOW, tlen)
    odd_in = w_in.shape[1]
    row = lambda b, t: (b * nt + t, 0)
    const2 = lambda b, t: (0, 0)
    const3 = lambda b, t: (0, 0, 0)
    per_b = lambda b, t: (b, 0, 0)
    kern = functools.partial(_odd_mixer_kernel, nseq=nseq, tm=tm, cq=cq, kv_rows=kv_rows,
                             mask_first=mask_first, emit_sgu_v=emit_sgu_v, hp=hp)
    kvw = N_KV_HEADS * HEAD_DIM
    out_specs = [
        pl.BlockSpec((rows, D_MODEL), row),
        pl.BlockSpec((nseq, kv_rows, kvw), per_b),
        pl.BlockSpec((nseq, kv_rows, kvw), per_b),
    ]
    out_shape = [
        jax.ShapeDtypeStruct((nb * tlen, D_MODEL), F32),
        jax.ShapeDtypeStruct((nb, kv_rows, kvw), F32),
        jax.ShapeDtypeStruct((nb, kv_rows, kvw), F32),
    ]
    if emit_sgu_v:
        out_specs.append(pl.BlockSpec((rows, D_SGU), row))
        out_shape.append(jax.ShapeDtypeStruct((nb * tlen, D_SGU), F32))
    n_sgu = -(-tm // SGU_CHUNK)
    op_dtype = F32 if hp else BF16
    return pl.pallas_call(
        kern,
        grid=(nb // nseq, nt),
        in_specs=[
            pl.BlockSpec(memory_space=pltpu.SMEM),
            pl.BlockSpec((rows, D_MODEL), row),
            pl.BlockSpec((1, D_MODEL), const2),
            pl.BlockSpec((D_MODEL, odd_in), const2),
            pl.BlockSpec((1, D_SGU), const2),
            pl.BlockSpec((1, D_SGU), const2),
            pl.BlockSpec((N_SGU_HEADS, SGU_CHUNK, SGU_CHUNK), const3),
            pl.BlockSpec((SGU_CHUNK, D_SGU), const2),
            pl.BlockSpec((D_MODEL, D_MODEL), const2),
            pl.BlockSpec((nseq, WINDOW, kvw), per_b),
            pl.BlockSpec((nseq, WINDOW, kvw), per_b),
        ],
        out_specs=out_specs,
        out_shape=out_shape,
        scratch_shapes=[
            pltpu.VMEM((nseq * N_KV_HEADS, WINDOW + tm, HEAD_DIM), op_dtype),
            pltpu.VMEM((nseq * N_KV_HEADS, WINDOW + tm, HEAD_DIM), op_dtype),
            pltpu.VMEM((nseq * n_sgu * SGU_CHUNK, D_SGU), op_dtype),
            pltpu.VMEM((rows, N_Q_HEADS * HEAD_DIM), F32),
            pltpu.VMEM((rows, D_MODEL), op_dtype),
        ],
        compiler_params=pltpu.CompilerParams(
            dimension_semantics=("arbitrary", "arbitrary"), vmem_limit_bytes=VMEM_LIMIT),
        name="odd_mixer",
    )(sinks, h, g, w_in, lng, lnb, ws, bs, w_out, kc, vc)


def _ffn_kernel(*refs, routed, final_norm, n_chunks, hp_router, hp_experts):
    refs = list(refs)
    h_ref, g_ref = refs[0], refs[1]
    pos = 2
    if routed:
        router_ref = refs[pos]
        pos += 1
    wg_ref, wu_ref, wd_ref = refs[pos:pos + 3]
    pos += 3
    if final_norm:
        gf_ref = refs[pos]
        pos += 1
    out_ref = refs[pos]
    n_scr, acc = refs[pos + 1], refs[pos + 2]
    if routed:
        comb = refs[pos + 3]
    j = pl.program_id(1)

    @pl.when(j == 0)
    def _():
        n = _rms(h_ref[...], g_ref[...])
        n_scr[...] = n.astype(n_scr.dtype)
        if routed:
            logits = _mm(n, router_ref[...], hp_router)
            lane = lax.broadcasted_iota(jnp.int32, logits.shape, 1)
            logits = jnp.where(lane < N_EXPERTS, logits, -jnp.inf)
            m1 = jnp.max(logits, axis=-1, keepdims=True)
            i1 = jnp.min(jnp.where(logits == m1, lane, LANES), axis=-1, keepdims=True)
            rest = jnp.where(lane == i1, -jnp.inf, logits)
            m2 = jnp.max(rest, axis=-1, keepdims=True)
            i2 = jnp.min(jnp.where(rest == m2, lane, LANES), axis=-1, keepdims=True)
            e2 = jnp.exp(m2 - m1)
            den = 1.0 + e2
            comb[...] = jnp.where(lane == i1, 1.0 / den, 0.0) + jnp.where(lane == i2, e2 / den, 0.0)

    x = n_scr[...]
    gate = _mm(x, wg_ref[...], hp_experts)
    up = _mm(x, wu_ref[...], hp_experts)
    act = (gate * _sigmoid(gate)) * up
    y = _mm(act, wd_ref[...], hp_experts)
    if routed:
        cmb = comb[...]
        lane = lax.broadcasted_iota(jnp.int32, cmb.shape, 1)
        y = y * jnp.sum(jnp.where(lane == j, cmb, 0.0), axis=-1, keepdims=True)

    @pl.when(j == 0)
    def _():
        acc[...] = y

    @pl.when(j > 0)
    def _():
        acc[...] += y

    @pl.when(j == n_chunks - 1)
    def _():
        res = h_ref[...] + acc[...]
        if final_norm:
            res = _rms(res, gf_ref[...])
        out_ref[...] = res


def _ffn(h, g, wg, wu, wd, tm, router=None, g_final=None, hp_router=False, hp_experts=False):
    rows = h.shape[0]
    routed = router is not None
    final_norm = g_final is not None
    row = lambda i, j: (i, 0)
    const2 = lambda i, j: (0, 0)
    in_specs = [pl.BlockSpec((tm, D_MODEL), row), pl.BlockSpec((1, D_MODEL), const2)]
    args = [h, g]
    if routed:
        n_chunks = N_EXPERTS
        in_specs.append(pl.BlockSpec((D_MODEL, LANES), const2))
        args.append(router)
        in_specs += [
            pl.BlockSpec((None, D_MODEL, D_FF_EXPERT), lambda i, j: (j, 0, 0)),
            pl.BlockSpec((None, D_MODEL, D_FF_EXPERT), lambda i, j: (j, 0, 0)),
            pl.BlockSpec((None, D_FF_EXPERT, D_MODEL), lambda i, j: (j, 0, 0)),
        ]
    else:
        n_chunks = D_FF // D_FF_EXPERT
        in_specs += [
            pl.BlockSpec((D_MODEL, D_FF_EXPERT), lambda i, j: (0, j)),
            pl.BlockSpec((D_MODEL, D_FF_EXPERT), lambda i, j: (0, j)),
            pl.BlockSpec((D_FF_EXPERT, D_MODEL), lambda i, j: (j, 0)),
        ]
    args += [wg, wu, wd]
    if final_norm:
        in_specs.append(pl.BlockSpec((1, D_MODEL), const2))
        args.append(g_final)
    scratch = [pltpu.VMEM((tm, D_MODEL), F32 if hp_experts else BF16), pltpu.VMEM((tm, D_MODEL), F32)]
    if routed:
        scratch.append(pltpu.VMEM((tm, LANES), F32))
    kern = functools.partial(_ffn_kernel, routed=routed, final_norm=final_norm, n_chunks=n_chunks,
                             hp_router=hp_router, hp_experts=hp_experts)
    return pl.pallas_call(
        kern,
        grid=(rows // tm, n_chunks),
        in_specs=in_specs,
        out_specs=pl.BlockSpec((tm, D_MODEL), row),
        out_shape=jax.ShapeDtypeStruct((rows, D_MODEL), F32),
        scratch_shapes=scratch,
        compiler_params=pltpu.CompilerParams(
            dimension_semantics=("arbitrary", "arbitrary"), vmem_limit_bytes=VMEM_LIMIT),
        name="moe_ffn" if routed else "dense_ffn",
    )(*args)


N_GROUPS = N_EXPERTS * N_EXPERTS
PAIR_IDS = tuple(a * N_EXPERTS + b for a in range(N_EXPERTS) for b in range(a + 1, N_EXPERTS))
N_PAIRS = len(PAIR_IDS)
PAIR_TM = 256
ROW_TM = 512
TILE_LANES = 256


def _route_kernel(h_ref, g_ref, router_ref, dest_ref, cnt_ref, start_ref, tile_ref,
                  nv_ref, carry, grp_s, rank_s, *, tm):
    i = pl.program_id(0)
    n_steps = pl.num_programs(0)

    @pl.when(i == 0)
    def _():
        carry[...] = jnp.zeros(carry.shape, F32)

    n = _rms(h_ref[...], g_ref[...])
    logits = _mm(n, router_ref[...])
    l8 = logits.T[0:N_EXPERTS, :]
    sub = lax.broadcasted_iota(jnp.int32, l8.shape, 0)
    m1 = jnp.max(l8, axis=0, keepdims=True)
    i1 = jnp.min(jnp.where(l8 == m1, sub, N_EXPERTS), axis=0, keepdims=True)
    rest = jnp.where(sub == i1, -jnp.inf, l8)
    m2 = jnp.max(rest, axis=0, keepdims=True)
    i2 = jnp.min(jnp.where(rest == m2, sub, N_EXPERTS), axis=0, keepdims=True)
    grp = jnp.minimum(i1, i2) * N_EXPERTS + jnp.maximum(i1, i2)
    grp_s[i] = grp

    gid = lax.broadcasted_iota(jnp.int32, (N_GROUPS, tm), 0)
    onehot = jnp.where(gid == grp, 1.0, 0.0)
    rs = lax.broadcasted_iota(jnp.int32, (tm, tm), 0)
    cs = lax.broadcasted_iota(jnp.int32, (tm, tm), 1)
    upper = jnp.where(rs <= cs, 1.0, 0.0)
    incl = _mm(onehot, upper)
    before = carry[...]
    rank = jnp.sum(onehot * (incl - 1.0 + before), axis=0, keepdims=True)
    rank_s[i] = rank.astype(jnp.int32)
    total = before + jnp.sum(onehot, axis=1, keepdims=True)
    carry[...] = total

    @pl.when(i == n_steps - 1)
    def _():
        padded = jnp.floor((total + (PAIR_TM - 1.0)) / PAIR_TM) * PAIR_TM
        gr = lax.broadcasted_iota(jnp.int32, (N_GROUPS, N_GROUPS), 0)
        gc = lax.broadcasted_iota(jnp.int32, (N_GROUPS, N_GROUPS), 1)
        lower = jnp.where(gc <= gr, 1.0, 0.0)
        ends = _mm(lower, jnp.broadcast_to(padded, (N_GROUPS, LANES)), hp=True)
        cnt_ref[...] = jnp.broadcast_to(total, cnt_ref.shape).astype(jnp.int32)
        starts = (ends - padded).astype(jnp.int32)
        start_ref[...] = starts

        group_start = [jnp.broadcast_to(starts[gid:gid + 1, 0:1], (1, tm)) for gid in PAIR_IDS]

        def to_rows(s, c):
            gs = grp_s[s]
            d = rank_s[s]
            for gid, st in zip(PAIR_IDS, group_start):
                d = d + jnp.where(gs == gid, st, 0)
            dest_ref[s] = d
            return c
        lax.fori_loop(0, n_steps, to_rows, 0)

        nv_ref[...] = (ends[N_GROUPS - 1:, :] / PAIR_TM).astype(jnp.int32)
        tile_start = lax.broadcasted_iota(jnp.int32, (N_GROUPS, TILE_LANES), 1).astype(F32) * PAIR_TM
        tile_grp = jnp.sum(jnp.where(ends[:, 0:1] <= tile_start, 1.0, 0.0), axis=0, keepdims=True)
        gidf = lax.broadcasted_iota(jnp.int32, (N_GROUPS, 1), 0).astype(F32)
        last = jnp.max(jnp.where(total > 0.0, gidf, 0.0), axis=0, keepdims=True)
        tile_ref[...] = jnp.minimum(tile_grp, last).astype(jnp.int32)


def _route(h, g, router, tm):
    rows = h.shape[0]
    n_steps = rows // tm
    const2 = lambda i: (0, 0)
    return pl.pallas_call(
        functools.partial(_route_kernel, tm=tm),
        grid=(n_steps,),
        in_specs=[
            pl.BlockSpec((tm, D_MODEL), lambda i: (i, 0)),
            pl.BlockSpec((1, D_MODEL), const2),
            pl.BlockSpec((D_MODEL, LANES), const2),
        ],
        out_specs=[
            pl.BlockSpec((n_steps, 1, tm), lambda i: (0, 0, 0)),
            pl.BlockSpec((N_GROUPS, LANES), const2),
            pl.BlockSpec((N_GROUPS, LANES), const2),
            pl.BlockSpec((1, TILE_LANES), const2),
            pl.BlockSpec((1, LANES), const2),
        ],
        out_shape=[
            jax.ShapeDtypeStruct((n_steps, 1, tm), jnp.int32),
            jax.ShapeDtypeStruct((N_GROUPS, LANES), jnp.int32),
            jax.ShapeDtypeStruct((N_GROUPS, LANES), jnp.int32),
            jax.ShapeDtypeStruct((1, TILE_LANES), jnp.int32),
            jax.ShapeDtypeStruct((1, LANES), jnp.int32),
        ],
        scratch_shapes=[
            pltpu.VMEM((N_GROUPS, 1), F32),
            pltpu.VMEM((n_steps, 1, tm), jnp.int32),
            pltpu.VMEM((n_steps, 1, tm), jnp.int32),
        ],
        compiler_params=pltpu.CompilerParams(
            dimension_semantics=("arbitrary",), vmem_limit_bytes=VMEM_LIMIT),
        name="route",
    )(h, g, router)


def _dispatch_kernel(start_ref, cnt_ref, nv_ref, dest_ref, h_ref, xs_hbm,
                     stage, zbuf, sem, fsem, *, tm, n_tiles):
    i = pl.program_id(0)
    last = pl.num_programs(0) - 1
    slot = i % 3

    def wait_rows(s):
        pltpu.make_async_copy(stage.at[s], xs_hbm.at[pl.ds(0, tm)], sem.at[s]).wait()

    def fill_ops(op):
        def tail(j, c):
            op(pltpu.make_async_copy(zbuf, xs_hbm.at[pl.ds(j * PAIR_TM, PAIR_TM)], fsem))
            return c
        lax.fori_loop(nv_ref[0], n_tiles, tail, 0)
        for gid in PAIR_IDS:
            lo = start_ref[gid] + cnt_ref[gid]
            hi = start_ref[gid] + (cnt_ref[gid] + PAIR_TM - 1) // PAIR_TM * PAIR_TM

            def row(r, c):
                op(pltpu.make_async_copy(zbuf.at[pl.ds(0, 1)], xs_hbm.at[pl.ds(r, 1)], fsem))
                return c
            lax.fori_loop(lo, hi, row, 0)

    @pl.when(i == 0)
    def _():
        zbuf[...] = jnp.zeros(zbuf.shape, F32)
        fill_ops(lambda c: c.start())

    @pl.when(i >= 3)
    def _():
        wait_rows(slot)

    stage[slot] = h_ref[...]
    for r in range(tm):
        pltpu.make_async_copy(stage.at[slot, pl.ds(r, 1)], xs_hbm.at[pl.ds(dest_ref[0, 0, r], 1)],
                              sem.at[slot]).start()

    @pl.when(i == last)
    def _():
        for s in range(3):
            wait_rows(s)
        fill_ops(lambda c: c.wait())


def _dispatch(h, start, cnt, nv, dest, n_tiles):
    rows = h.shape[0]
    tm = ROW_TM
    n_steps = rows // tm
    assert n_steps >= 3
    return pl.pallas_call(
        functools.partial(_dispatch_kernel, tm=tm, n_tiles=n_tiles),
        grid_spec=pltpu.PrefetchScalarGridSpec(
            num_scalar_prefetch=3,
            grid=(n_steps,),
            in_specs=[
                pl.BlockSpec((1, 1, tm), lambda i, s, c, n: (i, 0, 0), memory_space=pltpu.SMEM),
                pl.BlockSpec((tm, D_MODEL), lambda i, s, c, n: (i, 0)),
            ],
            out_specs=pl.BlockSpec(memory_space=pl.ANY),
            scratch_shapes=[
                pltpu.VMEM((3, tm, D_MODEL), F32),
                pltpu.VMEM((PAIR_TM, D_MODEL), F32),
                pltpu.SemaphoreType.DMA((3,)),
                pltpu.SemaphoreType.DMA,
            ],
        ),
        out_shape=jax.ShapeDtypeStruct((n_tiles * PAIR_TM, D_MODEL), F32),
        compiler_params=pltpu.CompilerParams(
            dimension_semantics=("arbitrary",), vmem_limit_bytes=VMEM_LIMIT),
        name="dispatch_rows",
    )(start, cnt, nv, dest, h)


def _collect_kernel(dest_ref, ys_hbm, out_ref, stage, sem, *, tm):
    i = pl.program_id(0)
    n_steps = pl.num_programs(0) - 1

    @pl.when(i < n_steps)
    def _():
        slot = i % 2
        for r in range(tm):
            pltpu.make_async_copy(ys_hbm.at[pl.ds(dest_ref[0, 0, r], 1)],
                                  stage.at[slot, pl.ds(r, 1)], sem.at[slot]).start()

    @pl.when(i >= 1)
    def _():
        slot = (i - 1) % 2
        pltpu.make_async_copy(ys_hbm.at[pl.ds(0, tm)], stage.at[slot], sem.at[slot]).wait()
        out_ref[...] = stage[slot]


def _collect(ys, dest):
    n_steps, _, tm = dest.shape
    return pl.pallas_call(
        functools.partial(_collect_kernel, tm=tm),
        grid=(n_steps + 1,),
        in_specs=[
            pl.BlockSpec((1, 1, tm), lambda i: (jnp.minimum(i, n_steps - 1), 0, 0),
                         memory_space=pltpu.SMEM),
            pl.BlockSpec(memory_space=pl.ANY),
        ],
        out_specs=pl.BlockSpec((tm, D_MODEL), lambda i: (jnp.maximum(i - 1, 0), 0)),
        out_shape=jax.ShapeDtypeStruct((n_steps * tm, D_MODEL), F32),
        scratch_shapes=[
            pltpu.VMEM((2, tm, D_MODEL), F32),
            pltpu.SemaphoreType.DMA((2,)),
        ],
        compiler_params=pltpu.CompilerParams(
            dimension_semantics=("arbitrary",), vmem_limit_bytes=VMEM_LIMIT),
        name="collect_rows",
    )(dest, ys)


def _pair_ffn_kernel(tg_ref, nv_ref, x_ref, g_ref, router_ref,
                     wga_ref, wua_ref, wda_ref, wgb_ref, wub_ref, wdb_ref, *rest, final_norm):
    if final_norm:
        gf_ref, out_ref = rest
    else:
        (out_ref,) = rest
    i = pl.program_id(0)

    @pl.when(i < nv_ref[0])
    def _():
        x = x_ref[...]
        nb16 = _rms(x, g_ref[...]).astype(BF16)
        logits = _mm(nb16, router_ref[...])
        lane = lax.broadcasted_iota(jnp.int32, logits.shape, 1)
        ea_id = tg_ref[i] // N_EXPERTS
        eb_id = tg_ref[i] % N_EXPERTS
        la = jnp.sum(jnp.where(lane == ea_id, logits, 0.0), axis=-1, keepdims=True)
        lb = jnp.sum(jnp.where(lane == eb_id, logits, 0.0), axis=-1, keepdims=True)
        m = jnp.maximum(la, lb)
        ea = jnp.exp(la - m)
        eb = jnp.exp(lb - m)
        den = ea + eb

        def expert(wg_ref, wu_ref, wd_ref):
            gate = _mm(nb16, wg_ref[...])
            up = _mm(nb16, wu_ref[...])
            act = (gate * _sigmoid(gate)) * up
            return _mm(act, wd_ref[...])

        y = (ea / den) * expert(wga_ref, wua_ref, wda_ref)
        y = y + (eb / den) * expert(wgb_ref, wub_ref, wdb_ref)
        res = x + y
        if final_norm:
            res = _rms(res, gf_ref[...])
        out_ref[...] = res

    @pl.when(i >= nv_ref[0])
    def _():
        out_ref[...] = jnp.zeros(out_ref.shape, F32)


def _pair_ffn(xs, tile_grp, nvalid, g, router, wg, wu, wd, g_final=None):
    tm = PAIR_TM
    n_tiles = xs.shape[0] // tm
    final_norm = g_final is not None
    const2 = lambda i, tg, nv: (0, 0)
    wa = lambda i, tg, nv: (tg[i] // N_EXPERTS, 0, 0)
    wb = lambda i, tg, nv: (tg[i] % N_EXPERTS, 0, 0)
    in_specs = [
        pl.BlockSpec((tm, D_MODEL), lambda i, tg, nv: (jnp.minimum(i, nv[0] - 1), 0)),
        pl.BlockSpec((1, D_MODEL), const2),
        pl.BlockSpec((D_MODEL, LANES), const2),
        pl.BlockSpec((None, D_MODEL, D_FF_EXPERT), wa),
        pl.BlockSpec((None, D_MODEL, D_FF_EXPERT), wa),
        pl.BlockSpec((None, D_FF_EXPERT, D_MODEL), wa),
        pl.BlockSpec((None, D_MODEL, D_FF_EXPERT), wb),
        pl.BlockSpec((None, D_MODEL, D_FF_EXPERT), wb),
        pl.BlockSpec((None, D_FF_EXPERT, D_MODEL), wb),
    ]
    args = [xs, g, router, wg, wu, wd, wg, wu, wd]
    if final_norm:
        in_specs.append(pl.BlockSpec((1, D_MODEL), const2))
        args.append(g_final)
    return pl.pallas_call(
        functools.partial(_pair_ffn_kernel, final_norm=final_norm),
        grid_spec=pltpu.PrefetchScalarGridSpec(
            num_scalar_prefetch=2,
            grid=(n_tiles,),
            in_specs=in_specs,
            out_specs=pl.BlockSpec((tm, D_MODEL), lambda i, tg, nv: (i, 0)),
        ),
        out_shape=jax.ShapeDtypeStruct((n_tiles * tm, D_MODEL), F32),
        compiler_params=pltpu.CompilerParams(
            dimension_semantics=("arbitrary",), vmem_limit_bytes=VMEM_LIMIT),
        name="pair_ffn",
    )(tile_grp, nvalid, *args)


def _sparse_moe(h, g, router, wg, wu, wd, g_final=None):
    rows = h.shape[0]
    n_tiles = rows // PAIR_TM + N_PAIRS
    assert n_tiles <= TILE_LANES
    dest, cnt, start, tile_grp, nv = _route(h, g, router, ROW_TM)
    start, cnt, nv = start[:, 0], cnt[:, 0], nv[0, :1]
    xs = _dispatch(h, start, cnt, nv, dest, n_tiles)
    ys = _pair_ffn(xs, tile_grp[0, :n_tiles], nv, g, router, wg, wu, wd, g_final=g_final)
    return _collect(ys, dest)


def kernel(x_prompt, x_sample, state_conv, state_pool, cache_swa_k, cache_swa_v, norm_mix, norm_ffn, norm_final, even_w_in, conv_w, conv_b, conv_ln_g, conv_ln_b, pool_w, pool_scale, even_w_out, ffn_w_gate, ffn_w_up, ffn_w_down, odd_w_in, sgu_ln_g, sgu_ln_b, sgu_w, sgu_b, attn_sinks, odd_w_out, router_w, exp_w_gate, exp_w_up, exp_w_down):
    bp, tp, _ = x_prompt.shape
    bs, ts, _ = x_sample.shape
    depth = norm_mix.shape[0]
    tm_p = 1024
    kvw = N_KV_HEADS * HEAD_DIM

    h_p = x_prompt.reshape(bp * tp, D_MODEL)
    h_s = x_sample.reshape(bs * ts, D_MODEL)
    b16 = lambda w: w.astype(BF16)
    r2 = lambda v: v.reshape(1, -1)
    pad_router = lambda r: jnp.pad(r, ((0, 0), (0, LANES - N_EXPERTS)))

    conv_p, conv_s, pool_p, pool_s = [], [], [], []
    k_p, v_p, k_s, v_s, sgu_s = [], [], [], [], []
    for layer in range(depth):
        i = layer // 2
        last = layer == depth - 1
        gf = r2(norm_final) if last else None
        if layer % 2 == 0:
            small = (conv_w[i], r2(conv_b[i]), r2(conv_ln_g[i]), r2(conv_ln_b[i]))
            gn = r2(norm_ffn[layer])
            fused = not last
            if fused:
                h_p, cst, pst = _even_layer(
                    h_p, bp, tp, r2(norm_mix[layer]), b16(even_w_in[i]), *small, b16(pool_w[i]),
                    r2(pool_scale[i]), b16(even_w_out[i]), gn, b16(ffn_w_gate[i]),
                    b16(ffn_w_up[i]), b16(ffn_w_down[i]), EVEN_TM)
            else:
                h_p, cst, pst = _even_mixer(
                    h_p, bp, tp, 0, r2(norm_mix[layer]), b16(even_w_in[i]), *small, b16(pool_w[i]),
                    r2(pool_scale[i]), b16(even_w_out[i]),
                    jnp.zeros((bp, CONV_HIST, D_CONV), F32), jnp.zeros((bp, POOL_HIST, D_POOL), F32),
                    tm_p, nseq=1, hp=False)
            conv_p.append(cst)
            pool_p.append(pst)
            h_s, cst, pst = _even_mixer(
                h_s, bs, ts, PAST_LEN, r2(norm_mix[layer]), even_w_in[i], *small, pool_w[i],
                r2(pool_scale[i]), even_w_out[i], state_conv[i], state_pool[i],
                ts, nseq=bs, hp=True)
            conv_s.append(cst)
            pool_s.append(pst)
            if not fused:
                h_p = _ffn(h_p, gn, b16(ffn_w_gate[i]), b16(ffn_w_up[i]), b16(ffn_w_down[i]), tm_p,
                           g_final=gf)
            h_s = _ffn(h_s, gn, ffn_w_gate[i], ffn_w_up[i], ffn_w_down[i], bs * ts, g_final=gf,
                       hp_experts=True)
        else:
            bias = jnp.repeat(sgu_b[i].T, SGU_HEAD, axis=1)
            small = (r2(sgu_ln_g[i]), r2(sgu_ln_b[i]), sgu_w[i], bias)
            zkv = jnp.zeros((bp, WINDOW, kvw), F32)
            h_p, kn, vn = _odd_mixer(
                h_p, bp, tp, attn_sinks[i], r2(norm_mix[layer]), b16(odd_w_in[i]), *small,
                b16(odd_w_out[i]), zkv, zkv, tm_p, CHUNK, True, False, nseq=1, hp=False)
            k_p.append(kn.reshape(bp, -1, N_KV_HEADS, HEAD_DIM))
            v_p.append(vn.reshape(bp, -1, N_KV_HEADS, HEAD_DIM))
            h_s, kn, vn, sv = _odd_mixer(
                h_s, bs, ts, attn_sinks[i], r2(norm_mix[layer]), odd_w_in[i], *small, odd_w_out[i],
                cache_swa_k[i].reshape(bs, WINDOW, kvw), cache_swa_v[i].reshape(bs, WINDOW, kvw),
                ts, ts, False, True, nseq=bs, hp=True)
            k_s.append(kn.reshape(bs, ts, N_KV_HEADS, HEAD_DIM))
            v_s.append(vn.reshape(bs, ts, N_KV_HEADS, HEAD_DIM))
            sgu_s.append(sv.reshape(bs, ts, D_SGU))
            gn = r2(norm_ffn[layer])
            wg, wu, wd = b16(exp_w_gate[i]), b16(exp_w_up[i]), b16(exp_w_down[i])
            h_p = _sparse_moe(h_p, gn, pad_router(b16(router_w[i])), wg, wu, wd, g_final=gf)
            if last:
                h_s = _ffn(h_s, gn, wg, wu, wd, bs * ts, router=pad_router(router_w[i]),
                           g_final=gf, hp_router=True)
            else:
                h_s = _ffn(h_s, gn, exp_w_gate[i], exp_w_up[i], exp_w_down[i], bs * ts,
                           router=pad_router(router_w[i]), g_final=gf, hp_router=True,
                           hp_experts=True)

    return (h_p.reshape(bp, tp, D_MODEL), h_s.reshape(bs, ts, D_MODEL),
            jnp.stack(conv_p), jnp.stack(conv_s),
            jnp.stack(pool_p), jnp.stack(pool_s),
            jnp.stack(k_p), jnp.stack(v_p),
            jnp.stack(k_s), jnp.stack(v_s),
            jnp.stack(sgu_s))
```
